```python
import jax, jax.numpy as jnp
from jax import lax
import numpy as np

D_MODEL = 1024
BATCH = 32
SEQ = 2048
DEPTH = 2

GRID_W = 64
CTX_LEN = 256
EPS = 1e-6
POOL_WINDOWS = (2, 4, 8, 16)
POOL_GROUPS = len(POOL_WINDOWS)
POOL_GROUP_DIM = D_MODEL // 8
POOL_WIDTH = POOL_GROUPS * POOL_GROUP_DIM
SGU_HEADS = 4
SGU_HEAD_DIM = D_MODEL // 8
SGU_WIDTH = SGU_HEADS * SGU_HEAD_DIM
CHUNK = 128
AB_IN = 2 * POOL_WIDTH + 3 * SGU_WIDTH
AB_MIX = POOL_WIDTH + SGU_WIDTH
FNET_HEADS = 4
FNET_HEAD_DIM = D_MODEL // 8
FNET_WIDTH = FNET_HEADS * FNET_HEAD_DIM
MLA_HEADS = 8
QK_NOPE = 64
QK_ROPE = 32
V_DIM = 64
Q_LORA = D_MODEL // 4
KV_LORA = D_MODEL // 8
MLA_WIDTH = MLA_HEADS * V_DIM
CD_IN = 2 * FNET_WIDTH + Q_LORA + KV_LORA + QK_ROPE + MLA_WIDTH
CD_MIX = FNET_WIDTH + MLA_WIDTH
ROPE_AXIS = QK_ROPE // 2
ROPE_BASE = 10000.0
Q_BLOCK = 128
ALPHA = (2 * DEPTH) ** 0.25
BETA = (8 * DEPTH) ** -0.25
N_EVEN = (DEPTH + 1) // 2
N_ODD = DEPTH // 2

kernel_name = "hybrid_pool_sgu_fnet_mla_diffusion_block"


def _layer_norm(x, g, b):
    xf = x.astype(jnp.float32)
    mu = jnp.mean(xf, axis=-1, keepdims=True)
    var = jnp.mean(jnp.square(xf - mu), axis=-1, keepdims=True)
    return ((xf - mu) * lax.rsqrt(var + EPS) * g + b).astype(x.dtype)


def _rms_norm(x, g):
    xf = x.astype(jnp.float32)
    y = xf * lax.rsqrt(jnp.mean(jnp.square(xf), axis=-1, keepdims=True) + EPS)
    return (y * g).astype(x.dtype)


def _adaln(cond, w_mod, b_mod):
    m = jax.nn.silu(cond) @ w_mod + b_mod
    return jnp.split(m, 3, axis=-1)


def _post_norm(x, y, gate, g, b):
    return _layer_norm(ALPHA * x + gate * y, g, b)


def _multiscale_pool(a, pool_w, pool_scale):
    bn, L, _ = a.shape
    cs = jnp.cumsum(a.astype(jnp.float32), axis=1)
    cs = jnp.pad(cs, ((0, 0), (1, 0), (0, 0)))
    t = jnp.arange(L)
    outs = []
    for g, w in enumerate(POOL_WINDOWS):
        lo = jnp.clip(t - w // 2, 0, L)
        hi = jnp.clip(t + w - w // 2, 0, L)
        seg = cs[:, :, g * POOL_GROUP_DIM:(g + 1) * POOL_GROUP_DIM]
        s = jnp.take(seg, hi, axis=1) - jnp.take(seg, lo, axis=1)
        outs.append(s / (hi - lo).astype(jnp.float32)[None, :, None])
    pooled = jnp.concatenate(outs, axis=-1).astype(a.dtype)
    d = (pooled - a).reshape(bn, L, POOL_GROUPS, POOL_GROUP_DIM)
    y = jnp.einsum('blgc,gcd->blgd', d, pool_w).reshape(bn, L, POOL_WIDTH)
    return y * pool_scale


def _chunk_sgu(u, v, w_s, b_s):
    bn, L, _ = v.shape
    nc = L // CHUNK
    vh = v.reshape(bn, L, SGU_HEADS, SGU_HEAD_DIM).astype(jnp.float32)
    mu = jnp.mean(vh, axis=-1, keepdims=True)
    var = jnp.mean(jnp.square(vh - mu), axis=-1, keepdims=True)
    vn = ((vh - mu) * lax.rsqrt(var + EPS)).astype(v.dtype)
    vn = vn.reshape(bn, nc, CHUNK, SGU_HEADS, SGU_HEAD_DIM)
    f = jnp.einsum('hij,bnjhc->bnihc', w_s, vn) + jnp.swapaxes(b_s, 0, 1)[:, :, None]
    return u * f.reshape(bn, L, SGU_WIDTH)


def _ab_mixer(h, w_in, pool_w, pool_scale, sgu_w, sgu_b, w_out):
    z = h @ w_in
    a, ga, u, v, gb = jnp.split(z, [POOL_WIDTH, 2 * POOL_WIDTH, 2 * POOL_WIDTH + SGU_WIDTH,
                                    2 * POOL_WIDTH + 2 * SGU_WIDTH], axis=-1)
    ya = _multiscale_pool(a, pool_w, pool_scale) * jax.nn.silu(ga)
    yb = _chunk_sgu(u, v, sgu_w, sgu_b) * jax.nn.silu(gb)
    return jnp.concatenate([ya, yb], axis=-1) @ w_out


def _fourier_mix(f, w_f):
    bn, L, _ = f.shape
    fh = f.reshape(bn, L, FNET_HEADS, FNET_HEAD_DIM).astype(jnp.float32)
    spec = jnp.fft.fft2(fh, axes=(1, 3), norm="ortho").real
    return spec.astype(f.dtype).reshape(bn, L, FNET_WIDTH) @ w_f


def _axial_tables(L, dtype):
    rows = L // GRID_W
    row = jnp.repeat(jnp.arange(rows), GRID_W).astype(jnp.float32)
    col = jnp.tile(jnp.arange(GRID_W), rows).astype(jnp.float32)
    inv = ROPE_BASE ** (-jnp.arange(0, ROPE_AXIS, 2, dtype=jnp.float32) / ROPE_AXIS)
    ang_r = (row[:, None] * inv)[:, None, :]
    ang_c = (col[:, None] * inv)[:, None, :]
    return (jnp.cos(ang_r).astype(dtype), jnp.sin(ang_r).astype(dtype),
            jnp.cos(ang_c).astype(dtype), jnp.sin(ang_c).astype(dtype))


def _rotate(x, cos, sin):
    half = x.shape[-1] // 2
    x1, x2 = x[..., :half], x[..., half:]
    return jnp.concatenate([x1 * cos - x2 * sin, x1 * sin + x2 * cos], axis=-1)


def _axial_rope(x, rope):
    cos_r, sin_r, cos_c, sin_c = rope
    return jnp.concatenate([_rotate(x[..., :ROPE_AXIS], cos_r, sin_r),
                            _rotate(x[..., ROPE_AXIS:], cos_c, sin_c)], axis=-1)


def _split_cd(z):
    o1 = FNET_WIDTH
    o2 = o1 + FNET_WIDTH
    o3 = o2 + Q_LORA
    o4 = o3 + KV_LORA
    o5 = o4 + QK_ROPE
    return jnp.split(z, [o1, o2, o3, o4, o5], axis=-1)


def _mla_q(cq, q_norm, w_q_up, rope):
    bn, L, _ = cq.shape
    q = (_rms_norm(cq, q_norm) @ w_q_up).reshape(bn, L, MLA_HEADS, QK_NOPE + QK_ROPE)
    if rope is not None:
        q = jnp.concatenate([q[..., :QK_NOPE], _axial_rope(q[..., QK_NOPE:], rope)], axis=-1)
    return q


def _mla_kv(ckv, kr, kv_norm, w_kv_up, rope):
    bn, L, _ = ckv.shape
    kv = (_rms_norm(ckv, kv_norm) @ w_kv_up).reshape(bn, L, MLA_HEADS, QK_NOPE + V_DIM)
    k_nope, v = kv[..., :QK_NOPE], kv[..., QK_NOPE:]
    kr = kr[:, :, None, :]
    if rope is not None:
        kr = _axial_rope(kr, rope)
    k = jnp.concatenate([k_nope, jnp.broadcast_to(kr, (bn, L, MLA_HEADS, QK_ROPE))], axis=-1)
    return k, v


def _block_attention(q, k, v):
    bn, L, H, dk = q.shape
    nb = L // Q_BLOCK
    scale = dk ** -0.5
    qb = q.reshape(bn, nb, Q_BLOCK, H, dk).transpose(1, 0, 2, 3, 4)

    def one(qi):
        s = jnp.einsum('bqhd,bkhd->bhqk', qi, k, preferred_element_type=jnp.float32) * scale
        p = jax.nn.softmax(s, axis=-1).astype(v.dtype)
        return jnp.einsum('bhqk,bkhd->bqhd', p, v)

    o = lax.map(one, qb)
    return o.transpose(1, 0, 2, 3, 4).reshape(bn, L, H * V_DIM)


def _cd_out(f_in, f_gate, attn, d_gate, fnet_w, w_out):
    yc = _fourier_mix(f_in, fnet_w) * jax.nn.silu(f_gate)
    yd = attn * jax.nn.silu(d_gate)
    return jnp.concatenate([yc, yd], axis=-1) @ w_out


def setup_inputs(seed: int = 0) -> dict:
    key = jax.random.key(seed)
    ks = jax.random.split(key, 32)

    def nrm(k, shape, scale):
        return jax.random.normal(k, shape, jnp.float32) * scale

    D = D_MODEL
    H = MLA_HEADS
    return {
        "x": nrm(ks[0], (BATCH, SEQ, D), 1.0),
        "c": nrm(ks[1], (BATCH, D), 1.0),
        "ctx": nrm(ks[2], (BATCH, CTX_LEN, D), 1.0),
        "c_ctx": nrm(ks[3], (D,), 1.0),
        "ab_w_mod": nrm(ks[4], (N_EVEN, D, 3 * D), 0.5 * D ** -0.5),
        "ab_b_mod": nrm(ks[5], (N_EVEN, 3 * D), 0.01),
        "ab_w_in": nrm(ks[6], (N_EVEN, D, AB_IN), D ** -0.5),
        "ab_pool_w": nrm(ks[7], (N_EVEN, POOL_GROUPS, POOL_GROUP_DIM, POOL_GROUP_DIM), POOL_GROUP_DIM ** -0.5),
        "ab_pool_scale": 1.0 + nrm(ks[8], (N_EVEN, POOL_WIDTH), 0.1),
        "ab_sgu_w": nrm(ks[9], (N_EVEN, SGU_HEADS, CHUNK, CHUNK), CHUNK ** -0.5),
        "ab_sgu_b": 1.0 + nrm(ks[10], (N_EVEN, SGU_HEADS, CHUNK), 0.01),
        "ab_w_out": nrm(ks[11], (N_EVEN, AB_MIX, D), BETA * AB_MIX ** -0.5),
        "ab_ln_g": 1.0 + nrm(ks[12], (N_EVEN, D), 0.1),
        "ab_ln_b": nrm(ks[13], (N_EVEN, D), 0.01),
        "cd_w_mod": nrm(ks[14], (N_ODD, D, 3 * D), 0.5 * D ** -0.5),
        "cd_b_mod": nrm(ks[15], (N_ODD, 3 * D), 0.01),
        "cd_w_in": nrm(ks[16], (N_ODD, D, CD_IN), D ** -0.5),
        "cd_fnet_w": nrm(ks[17], (N_ODD, FNET_WIDTH, FNET_WIDTH), FNET_WIDTH ** -0.5),
        "cd_q_norm": 1.0 + nrm(ks[18], (N_ODD, Q_LORA), 0.1),
        "cd_kv_norm": 1.0 + nrm(ks[19], (N_ODD, KV_LORA), 0.1),
        "cd_w_q_up": nrm(ks[20], (N_ODD, Q_LORA, H * (QK_NOPE + QK_ROPE)), Q_LORA ** -0.5),
        "cd_w_kv_up": nrm(ks[21], (N_ODD, KV_LORA, H * (QK_NOPE + V_DIM)), KV_LORA ** -0.5),
        "cd_w_out": nrm(ks[22], (N_ODD, CD_MIX, D), BETA * CD_MIX ** -0.5),
        "cd_ln_g": 1.0 + nrm(ks[23], (N_ODD, D), 0.1),
        "cd_ln_b": nrm(ks[24], (N_ODD, D), 0.01),
    }


def reference(x, c, ctx, c_ctx,
              ab_w_mod, ab_b_mod, ab_w_in, ab_pool_w, ab_pool_scale, ab_sgu_w, ab_sgu_b,
              ab_w_out, ab_ln_g, ab_ln_b,
              cd_w_mod, cd_b_mod, cd_w_in, cd_fnet_w, cd_q_norm, cd_kv_norm, cd_w_q_up,
              cd_w_kv_up, cd_w_out, cd_ln_g, cd_ln_b):
    L = x.shape[1]
    rope = _axial_tables(L, x.dtype)
    for i in range(DEPTH):
        last = i == DEPTH - 1
        j = i // 2
        if i % 2 == 0:
            sh, sc, gt = _adaln(c, ab_w_mod[j], ab_b_mod[j])
            h = x * (1.0 + sc[:, None]) + sh[:, None]
            y = _ab_mixer(h, ab_w_in[j], ab_pool_w[j], ab_pool_scale[j], ab_sgu_w[j], ab_sgu_b[j], ab_w_out[j])
            if not last:
                csh, csc, cgt = _adaln(c_ctx, ab_w_mod[j], ab_b_mod[j])
                hc = ctx * (1.0 + csc) + csh
                yc = _ab_mixer(hc, ab_w_in[j], ab_pool_w[j], ab_pool_scale[j], ab_sgu_w[j], ab_sgu_b[j], ab_w_out[j])
                ctx = _post_norm(ctx, yc, cgt, ab_ln_g[j], ab_ln_b[j])
            x = _post_norm(x, y, gt[:, None], ab_ln_g[j], ab_ln_b[j])
        else:
            csh, csc, cgt = _adaln(c_ctx, cd_w_mod[j], cd_b_mod[j])
            hc = ctx * (1.0 + csc) + csh
            fc_in, fc_gate, cq_c, ckv_c, kr_c, dc_gate = _split_cd(hc @ cd_w_in[j])
            k_ctx, v_ctx = _mla_kv(ckv_c, kr_c, cd_kv_norm[j], cd_w_kv_up[j], None)
            sh, sc, gt = _adaln(c, cd_w_mod[j], cd_b_mod[j])
            h = x * (1.0 + sc[:, None]) + sh[:, None]
            f_in, f_gate, cq, ckv, kr, d_gate = _split_cd(h @ cd_w_in[j])
            k_lat, v_lat = _mla_kv(ckv, kr, cd_kv_norm[j], cd_w_kv_up[j], rope)
            q = _mla_q(cq, cd_q_norm[j], cd_w_q_up[j], rope)
            attn = _block_attention(q, jnp.concatenate([k_ctx, k_lat], axis=1),
                                    jnp.concatenate([v_ctx, v_lat], axis=1))
            y = _cd_out(f_in, f_gate, attn, d_gate, cd_fnet_w[j], cd_w_out[j])
            if not last:
                qc = _mla_q(cq_c, cd_q_norm[j], cd_w_q_up[j], None)
                attn_c = _block_attention(qc, k_ctx, v_ctx)
                yc = _cd_out(fc_in, fc_gate, attn_c, dc_gate, cd_fnet_w[j], cd_w_out[j])
                ctx = _post_norm(ctx, yc, cgt, cd_ln_g[j], cd_ln_b[j])
            x = _post_norm(x, y, gt[:, None], cd_ln_g[j], cd_ln_b[j])
    return x
```

```python
import functools

import numpy as np
import jax
import jax.numpy as jnp
from jax import lax
from jax.experimental import pallas as pl
from jax.experimental.pallas import tpu as pltpu

D_MODEL = 1024
GRID_W = 64
EPS = 1e-6
DEPTH = 2
POOL_WINDOWS = (2, 4, 8, 16)
GROUP_DIM = 128
POOL_WIDTH = 512
SGU_HEADS = 4
SGU_WIDTH = 512
CHUNK = 128
FNET_HEADS = 4
FNET_WIDTH = 512
MLA_HEADS = 8
QK_NOPE = 64
QK_ROPE = 32
V_DIM = 64
Q_LORA = 256
KV_LORA = 128
MLA_WIDTH = 512
ROPE_AXIS = QK_ROPE // 2
ROPE_BASE = 10000.0
ALPHA = (2 * DEPTH) ** 0.25

LANES = 128
HALO = 16
HEAD_PAD = LANES
VMEM_LIMIT = 56 * 1024 * 1024

F32 = jnp.float32
BF16 = jnp.bfloat16


def _silu(v):
    return v * (1.0 / (1.0 + jnp.exp(-v)))


def _dot(a, b):
    return jnp.dot(a, b, preferred_element_type=F32)


def _layer_norm_rows(r, g, b):
    mu = jnp.mean(r, axis=-1, keepdims=True)
    rc = r - mu
    var = jnp.mean(rc * rc, axis=-1, keepdims=True)
    return rc * lax.rsqrt(var + EPS) * g + b


def _rms_rows(v, g):
    return v * lax.rsqrt(jnp.mean(v * v, axis=-1, keepdims=True) + EPS) * g


def _mod_kernel(c_ref, w_ref, b_ref, o_ref):
    s = _silu(c_ref[...]).astype(BF16)
    o_ref[...] = _dot(s, w_ref[...].astype(BF16)) + b_ref[...]


def _modulation(cond, w_mod, b_mod):
    rows, d = cond.shape
    n = w_mod.shape[1]
    tn = 768
    out = pl.pallas_call(
        _mod_kernel,
        grid=(n // tn,),
        in_specs=[pl.BlockSpec((rows, d), lambda j: (0, 0)),
                  pl.BlockSpec((d, tn), lambda j: (0, j)),
                  pl.BlockSpec((1, tn), lambda j: (0, j))],
        out_specs=pl.BlockSpec((rows, tn), lambda j: (0, j)),
        out_shape=jax.ShapeDtypeStruct((rows, n), F32),
    )(cond, w_mod, b_mod.reshape(1, n))
    return out.reshape(rows, 1, n)


def _split_mod(mod_ref):
    d = D_MODEL
    return mod_ref[0, :, 0:d], 1.0 + mod_ref[0, :, d:2 * d], mod_ref[0, :, 2 * d:3 * d]


def _ab_kernel(x_ref, xp_ref, xn_ref, mod_ref, win_ref, poolw_ref, pscale_ref, sguw_ref, sgub_ref,
               wout_ref, lng_ref, lnb_ref, o_ref, h_ref, a_ref, mix_ref, *, tl, seq_len):
    t = pl.program_id(1)
    nt = pl.num_programs(1)
    shift, scale1, gate = _split_mod(mod_ref)
    x = x_ref[0]
    h_ref[HALO:HALO + tl, :] = (x * scale1 + shift).astype(BF16)
    h_ref[0:HALO, :] = (xp_ref[0] * scale1 + shift).astype(BF16)
    h_ref[HALO + tl:2 * HALO + tl, :] = (xn_ref[0] * scale1 + shift).astype(BF16)

    a_ext = _dot(h_ref[...], win_ref[:, 0:POOL_WIDTH])
    a_ref[HALO:HALO + tl, :] = a_ext[HALO:HALO + tl]
    a_ref[0:HALO, :] = jnp.where(t > 0, a_ext[0:HALO], 0.0)
    a_ref[HALO + tl:2 * HALO + tl, :] = jnp.where(t < nt - 1, a_ext[HALO + tl:2 * HALO + tl], 0.0)

    h = h_ref[HALO:HALO + tl, :]
    gate_a = _silu(_dot(h, win_ref[:, POOL_WIDTH:2 * POOL_WIDTH]))
    pos = t * tl + lax.broadcasted_iota(jnp.int32, (tl, LANES), 0)
    for g, w in enumerate(POOL_WINDOWS):
        cols = slice(g * GROUP_DIM, (g + 1) * GROUP_DIM)
        first = HALO - w // 2
        s = a_ref[first:first + tl, cols]
        for j in range(1, w):
            s = s + a_ref[first + j:first + j + tl, cols]
        count = jnp.minimum(pos + (w - w // 2), seq_len) - jnp.maximum(pos - w // 2, 0)
        diff = s / count.astype(F32) - a_ref[HALO:HALO + tl, cols]
        ya = _dot(diff.astype(BF16), poolw_ref[g]) * pscale_ref[:, cols] * gate_a[:, cols]
        mix_ref[:, cols] = ya.astype(BF16)

    base = 2 * POOL_WIDTH
    u = _dot(h, win_ref[:, base:base + SGU_WIDTH])
    v = _dot(h, win_ref[:, base + SGU_WIDTH:base + 2 * SGU_WIDTH])
    gate_b = _silu(_dot(h, win_ref[:, base + 2 * SGU_WIDTH:base + 3 * SGU_WIDTH]))
    for hd in range(SGU_HEADS):
        cols = slice(hd * GROUP_DIM, (hd + 1) * GROUP_DIM)
        vh = v[:, cols]
        mu = jnp.mean(vh, axis=-1, keepdims=True)
        vc = vh - mu
        var = jnp.mean(vc * vc, axis=-1, keepdims=True)
        vn = (vc * lax.rsqrt(var + EPS)).astype(BF16)
        f = jnp.concatenate(
            [_dot(sguw_ref[hd], vn[ci * CHUNK:(ci + 1) * CHUNK, :]) + sgub_ref[hd]
             for ci in range(tl // CHUNK)], axis=0)
        yb = u[:, cols] * f * gate_b[:, cols]
        mix_ref[:, POOL_WIDTH + hd * GROUP_DIM:POOL_WIDTH + (hd + 1) * GROUP_DIM] = yb.astype(BF16)

    y = _dot(mix_ref[...], wout_ref[...])
    o_ref[0] = _layer_norm_rows(ALPHA * x + gate * y, lng_ref[...], lnb_ref[...])


def _ab_layer(x, mod, mod_row, w_in, pool_w, pool_scale, sgu_w, sgu_b, w_out, ln_g, ln_b, *, tl):
    bsz, seq, d = x.shape
    nt = seq // tl
    hb = tl // HALO
    last_hb = seq // HALO - 1
    const2 = lambda b, t: (0, 0)
    const3 = lambda b, t: (0, 0, 0)
    kern = functools.partial(_ab_kernel, tl=tl, seq_len=seq)
    return pl.pallas_call(
        kern,
        grid=(bsz, nt),
        in_specs=[
            pl.BlockSpec((1, tl, d), lambda b, t: (b, t, 0)),
            pl.BlockSpec((1, HALO, d), lambda b, t: (b, jnp.maximum(t * hb - 1, 0), 0)),
            pl.BlockSpec((1, HALO, d), lambda b, t: (b, jnp.minimum((t + 1) * hb, last_hb), 0)),
            pl.BlockSpec((1, 1, 3 * d), lambda b, t: (mod_row(b), 0, 0)),
            pl.BlockSpec(w_in.shape, const2),
            pl.BlockSpec(pool_w.shape, const3),
            pl.BlockSpec(pool_scale.shape, const2),
            pl.BlockSpec(sgu_w.shape, const3),
            pl.BlockSpec(sgu_b.shape, const3),
            pl.BlockSpec(w_out.shape, const2),
            pl.BlockSpec(ln_g.shape, const2),
            pl.BlockSpec(ln_b.shape, const2),
        ],
        out_specs=pl.BlockSpec((1, tl, d), lambda b, t: (b, t, 0)),
        out_shape=jax.ShapeDtypeStruct(x.shape, F32),
        scratch_shapes=[pltpu.VMEM((tl + 2 * HALO, d), BF16),
                        pltpu.VMEM((tl + 2 * HALO, POOL_WIDTH), F32),
                        pltpu.VMEM((tl, POOL_WIDTH + SGU_WIDTH), BF16)],
        compiler_params=pltpu.CompilerParams(dimension_semantics=("parallel", "arbitrary"),
                                             vmem_limit_bytes=VMEM_LIMIT),
    )(x, x, x, mod, w_in, pool_w, pool_scale, sgu_w, sgu_b, w_out, ln_g, ln_b)


def _store_kv(ckv, kr_lanes, kvnorm_ref, wkt_ref, wv_ref, kt_ref, v_ref):
    ckv_n = _rms_rows(ckv, kvnorm_ref[...]).astype(BF16)
    hk = jnp.concatenate([ckv_n, kr_lanes.astype(BF16)], axis=-1)
    kt = lax.dot_general(wkt_ref[...], hk, (((1,), (1,)), ((), ())),
                         preferred_element_type=F32).astype(BF16)
    vv = _dot(ckv_n, wv_ref[...]).astype(BF16)
    for hd in range(MLA_HEADS):
        kt_ref[0, hd] = kt[hd * HEAD_PAD:(hd + 1) * HEAD_PAD, :]
        v_ref[0, hd] = vv[:, hd * HEAD_PAD:(hd + 1) * HEAD_PAD]


def _cd_proj_ctx_kernel(x_ref, mod_ref, w_ref, kvnorm_ref, wkt_ref, wv_ref, kt_ref, v_ref):
    shift, scale1, _ = _split_mod(mod_ref)
    h = (x_ref[0] * scale1 + shift).astype(BF16)
    z = _dot(h, w_ref[...])
    _store_kv(z[:, 0:KV_LORA], z[:, KV_LORA:KV_LORA + LANES], kvnorm_ref, wkt_ref, wv_ref, kt_ref, v_ref)


def _cd_proj_kernel(x_ref, mod_ref, w_ref, cs_ref, qnorm_ref, wq_ref, kvnorm_ref, wkt_ref, wv_ref,
                    cosq_ref, sinq_ref, cosk_ref, sink_ref,
                    xcs_ref, gf_ref, gd_ref, q_ref, kt_ref, v_ref):
    shift, scale1, _ = _split_mod(mod_ref)
    h = (x_ref[0] * scale1 + shift).astype(BF16)
    fw = FNET_WIDTH
    f_in = _dot(h, w_ref[:, 0:fw]).astype(BF16)
    xcs = _dot(f_in, cs_ref[...]).astype(BF16)
    xcs_ref[0, 0] = xcs[:, 0:fw]
    xcs_ref[0, 1] = xcs[:, fw:2 * fw]
    gf_ref[0] = _silu(_dot(h, w_ref[:, fw:2 * fw])).astype(BF16)
    gd_ref[0] = _silu(_dot(h, w_ref[:, 2 * fw:2 * fw + MLA_WIDTH])).astype(BF16)

    o = 2 * fw + MLA_WIDTH
    small = _dot(h, w_ref[:, o:o + Q_LORA + KV_LORA + 2 * LANES])
    cq = small[:, 0:Q_LORA]
    ckv = small[:, Q_LORA:Q_LORA + KV_LORA]
    kr = small[:, Q_LORA + KV_LORA:Q_LORA + KV_LORA + LANES]
    kr_sw = small[:, Q_LORA + KV_LORA + LANES:Q_LORA + KV_LORA + 2 * LANES]

    qn = _rms_rows(cq, qnorm_ref[...]).astype(BF16)
    q2 = _dot(qn, wq_ref[...])
    width = MLA_HEADS * HEAD_PAD
    cosq, sinq = cosq_ref[...], sinq_ref[...]
    for hd in range(MLA_HEADS):
        qa = q2[:, hd * HEAD_PAD:(hd + 1) * HEAD_PAD]
        qb = q2[:, width + hd * HEAD_PAD:width + (hd + 1) * HEAD_PAD]
        q_ref[0, hd] = (qa * cosq + qb * sinq).astype(BF16)

    kr_rot = kr * cosk_ref[...] + kr_sw * sink_ref[...]
    _store_kv(ckv, kr_rot, kvnorm_ref, wkt_ref, wv_ref, kt_ref, v_ref)


def _const_spec(a):
    nd = a.ndim
    return pl.BlockSpec(a.shape, lambda b, t: (0,) * nd)


def _cd_proj_ctx(ctx, mod, mod_row, w_kv, kv_norm, wkt, wv):
    bsz, seq, d = ctx.shape
    hp = HEAD_PAD
    return pl.pallas_call(
        _cd_proj_ctx_kernel,
        grid=(bsz, 1),
        in_specs=[pl.BlockSpec((1, seq, d), lambda b, t: (b, 0, 0)),
                  pl.BlockSpec((1, 1, 3 * d), lambda b, t: (mod_row(b), 0, 0)),
                  _const_spec(w_kv), _const_spec(kv_norm), _const_spec(wkt), _const_spec(wv)],
        out_specs=[pl.BlockSpec((1, MLA_HEADS, hp, seq), lambda b, t: (b, 0, 0, 0)),
                   pl.BlockSpec((1, MLA_HEADS, seq, hp), lambda b, t: (b, 0, 0, 0))],
        out_shape=[jax.ShapeDtypeStruct((bsz, MLA_HEADS, hp, seq), BF16),
                   jax.ShapeDtypeStruct((bsz, MLA_HEADS, seq, hp), BF16)],
        compiler_params=pltpu.CompilerParams(dimension_semantics=("parallel", "arbitrary"),
                                             vmem_limit_bytes=VMEM_LIMIT),
    )(ctx, mod, w_kv, kv_norm, wkt, wv)


def _cd_proj(x, mod, w1, cs, q_norm, wq2, kv_norm, wkt, wv, cosq, sinq, cosk, sink, *, tl):
    bsz, seq, d = x.shape
    hp = HEAD_PAD
    fw = FNET_WIDTH
    tab = pl.BlockSpec((tl, LANES), lambda b, t: (t, 0))
    return pl.pallas_call(
        _cd_proj_kernel,
        grid=(bsz, seq // tl),
        in_specs=[pl.BlockSpec((1, tl, d), lambda b, t: (b, t, 0)),
                  pl.BlockSpec((1, 1, 3 * d), lambda b, t: (b, 0, 0)),
                  _const_spec(w1), _const_spec(cs), _const_spec(q_norm), _const_spec(wq2),
                  _const_spec(kv_norm), _const_spec(wkt), _const_spec(wv), tab, tab, tab, tab],
        out_specs=[pl.BlockSpec((1, 2, tl, fw), lambda b, t: (b, 0, t, 0)),
                   pl.BlockSpec((1, tl, fw), lambda b, t: (b, t, 0)),
                   pl.BlockSpec((1, tl, MLA_WIDTH), lambda b, t: (b, t, 0)),
                   pl.BlockSpec((1, MLA_HEADS, tl, hp), lambda b, t: (b, 0, t, 0)),
                   pl.BlockSpec((1, MLA_HEADS, hp, tl), lambda b, t: (b, 0, 0, t)),
                   pl.BlockSpec((1, MLA_HEADS, tl, hp), lambda b, t: (b, 0, t, 0))],
        out_shape=[jax.ShapeDtypeStruct((bsz, 2, seq, fw), BF16),
                   jax.ShapeDtypeStruct((bsz, seq, fw), BF16),
                   jax.ShapeDtypeStruct((bsz, seq, MLA_WIDTH), BF16),
                   jax.ShapeDtypeStruct((bsz, MLA_HEADS, seq, hp), BF16),
                   jax.ShapeDtypeStruct((bsz, MLA_HEADS, hp, seq), BF16),
                   jax.ShapeDtypeStruct((bsz, MLA_HEADS, seq, hp), BF16)],
        compiler_params=pltpu.CompilerParams(dimension_semantics=("parallel", "arbitrary"),
                                             vmem_limit_bytes=VMEM_LIMIT),
    )(x, mod, w1, cs, q_norm, wq2, kv_norm, wkt, wv, cosq, sinq, cosk, sink)


DFT_FINE = 32


def _cd_main_kernel(q_ref, ktc_ref, ktl_ref, vc_ref, vl_ref, xcs_ref, ac_ref, as_ref, bc_ref, bs_ref,
                    gf_ref, gd_ref, x_ref, mod_ref, fw_ref, wout_ref, lng_ref, lnb_ref,
                    o_ref, cls_ref, attn_ref, *, tq, seq_len):
    bc, bs = bc_ref[...], bs_ref[...]
    for i in range(tq // DFT_FINE):
        ac, asn = ac_ref[i:i + 1, :], as_ref[i:i + 1, :]
        rows = slice(i * DFT_FINE, (i + 1) * DFT_FINE)
        cls_ref[rows, 0:seq_len] = (ac * bc - asn * bs).astype(BF16)
        cls_ref[rows, seq_len:2 * seq_len] = (asn * bc + ac * bs).astype(BF16)
    spec = _dot(cls_ref[...], xcs_ref[0])
    yc = _dot(spec.astype(BF16), fw_ref[...]) * gf_ref[0].astype(F32)

    for hd in range(MLA_HEADS):
        qh = q_ref[0, hd]
        s_c = _dot(qh, ktc_ref[0, hd])
        s_l = _dot(qh, ktl_ref[0, hd])
        m = jnp.maximum(jnp.max(s_c, axis=-1, keepdims=True), jnp.max(s_l, axis=-1, keepdims=True))
        p_c = jnp.exp(s_c - m)
        p_l = jnp.exp(s_l - m)
        denom = jnp.sum(p_c, axis=-1, keepdims=True) + jnp.sum(p_l, axis=-1, keepdims=True)
        o_h = _dot(p_c.astype(BF16), vc_ref[0, hd]) + _dot(p_l.astype(BF16), vl_ref[0, hd])
        o_h = o_h * (1.0 / denom)
        pair = slice((hd // 2) * LANES, (hd // 2 + 1) * LANES)
        if hd % 2 == 0:
            attn_ref[:, pair] = o_h
        else:
            attn_ref[:, pair] += o_h
    yd = attn_ref[...] * gd_ref[0].astype(F32)

    mix = jnp.concatenate([yc.astype(BF16), yd.astype(BF16)], axis=-1)
    y = _dot(mix, wout_ref[...])
    _, _, gate = _split_mod(mod_ref)
    o_ref[0] = _layer_norm_rows(ALPHA * x_ref[0] + gate * y, lng_ref[...], lnb_ref[...])


def _cd_main(q, ktc, ktl, vc, vl, xcs, a_cos, a_sin, b_cos, b_sin, gf, gd, x, mod, fnet_w, w_out,
             ln_g, ln_b, *, tq):
    bsz, seq, d = x.shape
    lc = ktc.shape[-1]
    hp = HEAD_PAD
    nh = MLA_HEADS
    per_b4 = lambda b, t: (b, 0, 0, 0)
    kern = functools.partial(_cd_main_kernel, tq=tq, seq_len=seq)
    coarse = tq // DFT_FINE
    return pl.pallas_call(
        kern,
        grid=(bsz, seq // tq),
        in_specs=[pl.BlockSpec((1, nh, tq, hp), lambda b, t: (b, 0, t, 0)),
                  pl.BlockSpec((1, nh, hp, lc), per_b4),
                  pl.BlockSpec((1, nh, hp, seq), per_b4),
                  pl.BlockSpec((1, nh, lc, hp), per_b4),
                  pl.BlockSpec((1, nh, seq, hp), per_b4),
                  pl.BlockSpec((1, 2 * seq, FNET_WIDTH), lambda b, t: (b, 0, 0)),
                  pl.BlockSpec((coarse, seq), lambda b, t: (t, 0)),
                  pl.BlockSpec((coarse, seq), lambda b, t: (t, 0)),
                  _const_spec(b_cos), _const_spec(b_sin),
                  pl.BlockSpec((1, tq, FNET_WIDTH), lambda b, t: (b, t, 0)),
                  pl.BlockSpec((1, tq, MLA_WIDTH), lambda b, t: (b, t, 0)),
                  pl.BlockSpec((1, tq, d), lambda b, t: (b, t, 0)),
                  pl.BlockSpec((1, 1, 3 * d), lambda b, t: (b, 0, 0)),
                  _const_spec(fnet_w), _const_spec(w_out), _const_spec(ln_g), _const_spec(ln_b)],
        out_specs=pl.BlockSpec((1, tq, d), lambda b, t: (b, t, 0)),
        out_shape=jax.ShapeDtypeStruct(x.shape, F32),
        scratch_shapes=[pltpu.VMEM((tq, 2 * seq), BF16),
                        pltpu.VMEM((tq, MLA_WIDTH), F32)],
        compiler_params=pltpu.CompilerParams(dimension_semantics=("parallel", "arbitrary"),
                                             vmem_limit_bytes=VMEM_LIMIT),
    )(q, ktc, ktl, vc, vl, xcs, a_cos, a_sin, b_cos, b_sin, gf, gd, x, mod, fnet_w, w_out, ln_g, ln_b)


def _rope_tables(seq):
    pos = np.arange(seq)
    row = (pos // GRID_W).astype(np.float64)
    col = (pos % GRID_W).astype(np.float64)
    half = ROPE_AXIS // 2
    inv = (ROPE_BASE ** (-np.arange(0, ROPE_AXIS, 2, dtype=np.float32) / ROPE_AXIS)).astype(np.float32)
    cos = np.ones((seq, LANES), np.float32)
    sin = np.zeros((seq, LANES), np.float32)
    for j in range(QK_ROPE):
        coord = row if j < ROPE_AXIS else col
        ang = (coord.astype(np.float32) * inv[j % half]).astype(np.float32)
        cos[:, QK_NOPE + j] = np.cos(ang)
        sign = -1.0 if (j % ROPE_AXIS) < half else 1.0
        sin[:, QK_NOPE + j] = sign * np.sin(ang)
    return cos, sin


def _rope_partner(j):
    half = ROPE_AXIS // 2
    return j + half if (j % ROPE_AXIS) < half else j - half


def _dft_tables(seq, tq):
    l = np.arange(seq, dtype=np.float64)
    k1 = np.arange(seq // DFT_FINE, dtype=np.float64)[:, None]
    k0 = np.arange(DFT_FINE, dtype=np.float64)[:, None]
    ang_a = 2.0 * np.pi * ((k1 * DFT_FINE * l) % seq) / seq
    ang_b = 2.0 * np.pi * ((k0 * l) % seq) / seq
    norm = 1.0 / np.sqrt(seq * GROUP_DIM)
    return (np.cos(ang_a).astype(np.float32), np.sin(ang_a).astype(np.float32),
            (np.cos(ang_b) * norm).astype(np.float32), (np.sin(ang_b) * norm).astype(np.float32))


def _channel_dft():
    c = np.arange(GROUP_DIM, dtype=np.float64)
    ang = 2.0 * np.pi * ((c[:, None] * c[None, :]) % GROUP_DIM) / GROUP_DIM
    eye = np.eye(FNET_HEADS)
    return np.concatenate([np.kron(eye, np.cos(ang)), -np.kron(eye, np.sin(ang))], axis=1).astype(np.float32)


def kernel(x, c, ctx, c_ctx, ab_w_mod, ab_b_mod, ab_w_in, ab_pool_w, ab_pool_scale, ab_sgu_w, ab_sgu_b,
           ab_w_out, ab_ln_g, ab_ln_b, cd_w_mod, cd_b_mod, cd_w_in, cd_fnet_w, cd_q_norm, cd_kv_norm,
           cd_w_q_up, cd_w_kv_up, cd_w_out, cd_ln_g, cd_ln_b):
    bsz, seq, d = x.shape
    nh, hp = MLA_HEADS, HEAD_PAD

    rows = -(-(bsz + 1) // 8) * 8
    cond = jnp.zeros((rows, d), F32).at[:bsz].set(c).at[bsz].set(c_ctx)
    mod_ab = _modulation(cond, ab_w_mod[0], ab_b_mod[0])
    mod_cd = _modulation(cond, cd_w_mod[0], cd_b_mod[0])
    lat_row = lambda b: b
    ctx_row = lambda b: bsz

    ab_args = (ab_w_in[0].astype(BF16), ab_pool_w[0].astype(BF16), ab_pool_scale[0].reshape(1, -1),
               ab_sgu_w[0].astype(BF16),
               jnp.broadcast_to(ab_sgu_b[0][:, :, None], (SGU_HEADS, CHUNK, GROUP_DIM)),
               ab_w_out[0].astype(BF16), ab_ln_g[0].reshape(1, -1), ab_ln_b[0].reshape(1, -1))
    x1 = _ab_layer(x, mod_ab, lat_row, *ab_args, tl=512)
    ctx1 = _ab_layer(ctx, mod_ab, ctx_row, *ab_args, tl=ctx.shape[1])

    w = cd_w_in[0]
    o_cq = 2 * FNET_WIDTH
    o_ckv = o_cq + Q_LORA
    o_kr = o_ckv + KV_LORA
    o_dg = o_kr + QK_ROPE
    partner = np.array([_rope_partner(j) for j in range(QK_ROPE)])
    zeros = lambda n: jnp.zeros((d, n), F32)
    kr_cols = w[:, o_kr:o_dg]
    kr_lanes = jnp.concatenate([zeros(QK_NOPE), kr_cols, zeros(LANES - QK_NOPE - QK_ROPE)], axis=1)
    kr_lanes_sw = jnp.concatenate([zeros(QK_NOPE), kr_cols[:, partner], zeros(LANES - QK_NOPE - QK_ROPE)], axis=1)
    w1 = jnp.concatenate([w[:, 0:o_cq], w[:, o_dg:], w[:, o_cq:o_kr], kr_lanes, kr_lanes_sw], axis=1).astype(BF16)
    w_kv_ctx = jnp.concatenate([w[:, o_ckv:o_kr], kr_lanes], axis=1).astype(BF16)

    wq = cd_w_q_up[0].reshape(Q_LORA, nh, QK_NOPE + QK_ROPE)
    pad_q = jnp.zeros((Q_LORA, nh, hp - QK_NOPE - QK_ROPE), F32)
    wq_pad = jnp.concatenate([wq, pad_q], axis=-1)
    wq_sw = jnp.concatenate([jnp.zeros((Q_LORA, nh, QK_NOPE), F32), wq[:, :, QK_NOPE + partner], pad_q], axis=-1)
    wq2 = jnp.concatenate([wq_pad.reshape(Q_LORA, nh * hp), wq_sw.reshape(Q_LORA, nh * hp)], axis=1).astype(BF16)

    wkv = cd_w_kv_up[0].reshape(KV_LORA, nh, QK_NOPE + V_DIM)
    wk_pad = jnp.concatenate([wkv[:, :, :QK_NOPE], jnp.zeros((KV_LORA, nh, hp - QK_NOPE), F32)], axis=-1)
    lane = np.arange(hp)
    sel = np.diag(((lane >= QK_NOPE) & (lane < QK_NOPE + QK_ROPE)).astype(np.float32))
    wkt = jnp.concatenate([wk_pad.transpose(1, 2, 0).reshape(nh * hp, KV_LORA),
                           jnp.asarray(np.tile(sel, (nh, 1)))], axis=1).astype(BF16)
    wv_w = wkv[:, :, QK_NOPE:]
    zv = jnp.zeros((KV_LORA, nh, hp - V_DIM), F32)
    even = (np.arange(nh) % 2 == 0)[None, :, None]
    wv = jnp.where(even, jnp.concatenate([wv_w, zv], axis=-1),
                   jnp.concatenate([zv, wv_w], axis=-1)).reshape(KV_LORA, nh * hp).astype(BF16)

    q_norm = cd_q_norm[0].reshape(1, -1)
    kv_norm = cd_kv_norm[0].reshape(1, -1)
    cos, sin = _rope_tables(seq)
    qscale = np.float32((QK_NOPE + QK_ROPE) ** -0.5)
    cs = jnp.asarray(_channel_dft()).astype(BF16)

    ktc, vc = _cd_proj_ctx(ctx1, mod_cd, ctx_row, w_kv_ctx, kv_norm, wkt, wv)
    xcs, gf, gd, q, ktl, vl = _cd_proj(x1, mod_cd, w1, cs, q_norm, wq2, kv_norm, wkt, wv,
                                       jnp.asarray(cos * qscale), jnp.asarray(sin * qscale),
                                       jnp.asarray(cos), jnp.asarray(sin), tl=512)
    tq = 256
    a_cos, a_sin, b_cos, b_sin = (jnp.asarray(t) for t in _dft_tables(seq, tq))
    return _cd_main(q, ktc, ktl, vc, vl, xcs.reshape(bsz, 2 * seq, FNET_WIDTH), a_cos, a_sin, b_cos, b_sin,
                    gf, gd, x1, mod_cd, cd_fnet_w[0].astype(BF16), cd_w_out[0].astype(BF16),
                    cd_ln_g[0].reshape(1, -1), cd_ln_b[0].reshape(1, -1), tq=tq)
```

```python
import functools

import numpy as np
import jax
import jax.numpy as jnp
from jax import lax
from jax.experimental import pallas as pl
from jax.experimental.pallas import tpu as pltpu

D_MODEL = 1024
GRID_W = 64
EPS = 1e-6
DEPTH = 2
POOL_WINDOWS = (2, 4, 8, 16)
GROUP_DIM = 128
POOL_WIDTH = 512
SGU_HEADS = 4
SGU_WIDTH = 512
CHUNK = 128
FNET_HEADS = 4
FNET_WIDTH = 512
MLA_HEADS = 8
QK_NOPE = 64
QK_ROPE = 32
V_DIM = 64
Q_LORA = 256
KV_LORA = 128
MLA_WIDTH = 512
ROPE_AXIS = QK_ROPE // 2
ROPE_BASE = 10000.0
ALPHA = (2 * DEPTH) ** 0.25

LANES = 128
HALO = 16
HEAD_PAD = LANES
VMEM_LIMIT = 56 * 1024 * 1024

F32 = jnp.float32
BF16 = jnp.bfloat16


def _silu(v):
    return v * (1.0 / (1.0 + jnp.exp(-v)))


def _dot(a, b):
    return jnp.dot(a, b, preferred_element_type=F32)


def _layer_norm_rows(r, g, b):
    mu = jnp.mean(r, axis=-1, keepdims=True)
    rc = r - mu
    var = jnp.mean(rc * rc, axis=-1, keepdims=True)
    return rc * lax.rsqrt(var + EPS) * g + b


def _rms_rows(v, g):
    return v * lax.rsqrt(jnp.mean(v * v, axis=-1, keepdims=True) + EPS) * g


def _mod_kernel(c_ref, w_ref, b_ref, o_ref):
    s = _silu(c_ref[...]).astype(BF16)
    o_ref[...] = _dot(s, w_ref[...].astype(BF16)) + b_ref[...]


def _modulation(cond, w_mod, b_mod):
    rows, d = cond.shape
    n = w_mod.shape[1]
    tn = 768
    out = pl.pallas_call(
        _mod_kernel,
        grid=(n // tn,),
        in_specs=[pl.BlockSpec((rows, d), lambda j: (0, 0)),
                  pl.BlockSpec((d, tn), lambda j: (0, j)),
                  pl.BlockSpec((1, tn), lambda j: (0, j))],
        out_specs=pl.BlockSpec((rows, tn), lambda j: (0, j)),
        out_shape=jax.ShapeDtypeStruct((rows, n), F32),
        name="modulation",
    )(cond, w_mod, b_mod.reshape(1, n))
    return out.reshape(rows, 1, n)


def _split_mod(mod_ref):
    d = D_MODEL
    return mod_ref[0, :, 0:d], 1.0 + mod_ref[0, :, d:2 * d], mod_ref[0, :, 2 * d:3 * d]


def _ab_kernel(x_ref, xp_ref, xn_ref, mod_ref, win_ref, poolw_ref, pscale_ref, sguw_ref, sgub_ref,
               wout_ref, lng_ref, lnb_ref, o_ref, h_ref, a_ref, mix_ref, *, tl, seq_len):
    t = pl.program_id(1)
    nt = pl.num_programs(1)
    shift, scale1, gate = _split_mod(mod_ref)
    x = x_ref[0]
    h_ref[HALO:HALO + tl, :] = (x * scale1 + shift).astype(BF16)
    h_ref[0:HALO, :] = (xp_ref[0] * scale1 + shift).astype(BF16)
    h_ref[HALO + tl:2 * HALO + tl, :] = (xn_ref[0] * scale1 + shift).astype(BF16)

    a_ext = _dot(h_ref[...], win_ref[:, 0:POOL_WIDTH])
    a_ref[HALO:HALO + tl, :] = a_ext[HALO:HALO + tl]
    a_ref[0:HALO, :] = jnp.where(t > 0, a_ext[0:HALO], 0.0)
    a_ref[HALO + tl:2 * HALO + tl, :] = jnp.where(t < nt - 1, a_ext[HALO + tl:2 * HALO + tl], 0.0)

    h = h_ref[HALO:HALO + tl, :]
    gate_a = _silu(_dot(h, win_ref[:, POOL_WIDTH:2 * POOL_WIDTH]))
    pos = t * tl + lax.broadcasted_iota(jnp.int32, (tl, LANES), 0)
    for g, w in enumerate(POOL_WINDOWS):
        cols = slice(g * GROUP_DIM, (g + 1) * GROUP_DIM)
        first = HALO - w // 2
        s = a_ref[first:first + tl, cols]
        for j in range(1, w):
            s = s + a_ref[first + j:first + j + tl, cols]
        count = jnp.minimum(pos + (w - w // 2), seq_len) - jnp.maximum(pos - w // 2, 0)
        diff = s / count.astype(F32) - a_ref[HALO:HALO + tl, cols]
        ya = _dot(diff.astype(BF16), poolw_ref[g]) * pscale_ref[:, cols] * gate_a[:, cols]
        mix_ref[:, cols] = ya.astype(BF16)

    base = 2 * POOL_WIDTH
    u = _dot(h, win_ref[:, base:base + SGU_WIDTH])
    v = _dot(h, win_ref[:, base + SGU_WIDTH:base + 2 * SGU_WIDTH])
    gate_b = _silu(_dot(h, win_ref[:, base + 2 * SGU_WIDTH:base + 3 * SGU_WIDTH]))
    for hd in range(SGU_HEADS):
        cols = slice(hd * GROUP_DIM, (hd + 1) * GROUP_DIM)
        vh = v[:, cols]
        mu = jnp.mean(vh, axis=-1, keepdims=True)
        vc = vh - mu
        var = jnp.mean(vc * vc, axis=-1, keepdims=True)
        vn = (vc * lax.rsqrt(var + EPS)).astype(BF16)
        f = jnp.concatenate(
            [_dot(sguw_ref[hd], vn[ci * CHUNK:(ci + 1) * CHUNK, :]) + sgub_ref[hd]
             for ci in range(tl // CHUNK)], axis=0)
        yb = u[:, cols] * f * gate_b[:, cols]
        mix_ref[:, POOL_WIDTH + hd * GROUP_DIM:POOL_WIDTH + (hd + 1) * GROUP_DIM] = yb.astype(BF16)

    y = _dot(mix_ref[...], wout_ref[...])
    o_ref[0] = _layer_norm_rows(ALPHA * x + gate * y, lng_ref[...], lnb_ref[...])


def _ab_layer(x, mod, mod_row, w_in, pool_w, pool_scale, sgu_w, sgu_b, w_out, ln_g, ln_b, *, tl):
    bsz, seq, d = x.shape
    nt = seq // tl
    hb = tl // HALO
    last_hb = seq // HALO - 1
    const2 = lambda b, t: (0, 0)
    const3 = lambda b, t: (0, 0, 0)
    kern = functools.partial(_ab_kernel, tl=tl, seq_len=seq)
    return pl.pallas_call(
        kern,
        grid=(bsz, nt),
        in_specs=[
            pl.BlockSpec((1, tl, d), lambda b, t: (b, t, 0)),
            pl.BlockSpec((1, HALO, d), lambda b, t: (b, jnp.maximum(t * hb - 1, 0), 0)),
            pl.BlockSpec((1, HALO, d), lambda b, t: (b, jnp.minimum((t + 1) * hb, last_hb), 0)),
            pl.BlockSpec((1, 1, 3 * d), lambda b, t: (mod_row(b), 0, 0)),
            pl.BlockSpec(w_in.shape, const2),
            pl.BlockSpec(pool_w.shape, const3),
            pl.BlockSpec(pool_scale.shape, const2),
            pl.BlockSpec(sgu_w.shape, const3),
            pl.BlockSpec(sgu_b.shape, const3),
            pl.BlockSpec(w_out.shape, const2),
            pl.BlockSpec(ln_g.shape, const2),
            pl.BlockSpec(ln_b.shape, const2),
        ],
        out_specs=pl.BlockSpec((1, tl, d), lambda b, t: (b, t, 0)),
        out_shape=jax.ShapeDtypeStruct(x.shape, F32),
        scratch_shapes=[pltpu.VMEM((tl + 2 * HALO, d), BF16),
                        pltpu.VMEM((tl + 2 * HALO, POOL_WIDTH), F32),
                        pltpu.VMEM((tl, POOL_WIDTH + SGU_WIDTH), BF16)],
        compiler_params=pltpu.CompilerParams(dimension_semantics=("parallel", "arbitrary"),
                                             vmem_limit_bytes=VMEM_LIMIT),
        name="ab_layer",
    )(x, x, x, mod, w_in, pool_w, pool_scale, sgu_w, sgu_b, w_out, ln_g, ln_b)


def _dot_nt(a, b):
    return lax.dot_general(a, b, (((1,), (1,)), ((), ())), preferred_element_type=F32)


def _store_kv(ckv, kr_lanes, kvnorm_ref, wk_ref, wvt_ref, k_ref, vt_ref):
    ckv_n = _rms_rows(ckv, kvnorm_ref[...]).astype(BF16)
    hk = jnp.concatenate([ckv_n, kr_lanes.astype(BF16)], axis=-1)
    kk = _dot(hk, wk_ref[...]).astype(BF16)
    for hd in range(MLA_HEADS):
        k_ref[0, hd] = kk[:, hd * HEAD_PAD:(hd + 1) * HEAD_PAD]
    vt_ref[0] = _dot_nt(wvt_ref[...], ckv_n).astype(BF16)


def _cd_proj_ctx_kernel(x_ref, mod_ref, w_ref, kvnorm_ref, wk_ref, wvt_ref, k_ref, vt_ref):
    shift, scale1, _ = _split_mod(mod_ref)
    h = (x_ref[0] * scale1 + shift).astype(BF16)
    z = _dot(h, w_ref[...])
    _store_kv(z[:, 0:KV_LORA], z[:, KV_LORA:KV_LORA + LANES], kvnorm_ref, wk_ref, wvt_ref, k_ref, vt_ref)


def _cd_proj_kernel(x_ref, mod_ref, w_ref, cs_ref, qnorm_ref, wqt_ref, kvnorm_ref, wk_ref, wvt_ref,
                    cosqt_ref, sinqt_ref, cosk_ref, sink_ref,
                    xcs_ref, gf_ref, gd_ref, qt_ref, k_ref, vt_ref):
    shift, scale1, _ = _split_mod(mod_ref)
    h = (x_ref[0] * scale1 + shift).astype(BF16)
    fw = FNET_WIDTH
    f_in = _dot(h, w_ref[:, 0:fw]).astype(BF16)
    xcs = _dot(f_in, cs_ref[...]).astype(BF16)
    xcs_ref[0, 0] = xcs[:, 0:fw]
    xcs_ref[0, 1] = xcs[:, fw:2 * fw]
    gf_ref[0] = _silu(_dot(h, w_ref[:, fw:2 * fw])).astype(BF16)
    gd_ref[0] = _silu(_dot(h, w_ref[:, 2 * fw:2 * fw + MLA_WIDTH])).astype(BF16)

    o = 2 * fw + MLA_WIDTH
    small = _dot(h, w_ref[:, o:o + Q_LORA + KV_LORA + 2 * LANES])
    cq = small[:, 0:Q_LORA]
    ckv = small[:, Q_LORA:Q_LORA + KV_LORA]
    kr = small[:, Q_LORA + KV_LORA:Q_LORA + KV_LORA + LANES]
    kr_sw = small[:, Q_LORA + KV_LORA + LANES:Q_LORA + KV_LORA + 2 * LANES]

    qn = _rms_rows(cq, qnorm_ref[...]).astype(BF16)
    q2t = _dot_nt(wqt_ref[...], qn)
    width = MLA_HEADS * HEAD_PAD
    cosqt, sinqt = cosqt_ref[...], sinqt_ref[...]
    for hd in range(MLA_HEADS):
        qa = q2t[hd * HEAD_PAD:(hd + 1) * HEAD_PAD, :]
        qb = q2t[width + hd * HEAD_PAD:width + (hd + 1) * HEAD_PAD, :]
        qt_ref[0, hd] = (qa * cosqt + qb * sinqt).astype(BF16)

    kr_rot = kr * cosk_ref[...] + kr_sw * sink_ref[...]
    _store_kv(ckv, kr_rot, kvnorm_ref, wk_ref, wvt_ref, k_ref, vt_ref)


def _const_spec(a):
    nd = a.ndim
    return pl.BlockSpec(a.shape, lambda b, t: (0,) * nd)


def _cd_proj_ctx(ctx, mod, mod_row, w_kv, kv_norm, wk, wvt):
    bsz, seq, d = ctx.shape
    hp = HEAD_PAD
    return pl.pallas_call(
        _cd_proj_ctx_kernel,
        grid=(bsz, 1),
        in_specs=[pl.BlockSpec((1, seq, d), lambda b, t: (b, 0, 0)),
                  pl.BlockSpec((1, 1, 3 * d), lambda b, t: (mod_row(b), 0, 0)),
                  _const_spec(w_kv), _const_spec(kv_norm), _const_spec(wk), _const_spec(wvt)],
        out_specs=[pl.BlockSpec((1, MLA_HEADS, seq, hp), lambda b, t: (b, 0, 0, 0)),
                   pl.BlockSpec((1, MLA_WIDTH, seq), lambda b, t: (b, 0, 0))],
        out_shape=[jax.ShapeDtypeStruct((bsz, MLA_HEADS, seq, hp), BF16),
                   jax.ShapeDtypeStruct((bsz, MLA_WIDTH, seq), BF16)],
        compiler_params=pltpu.CompilerParams(dimension_semantics=("parallel", "arbitrary"),
                                             vmem_limit_bytes=VMEM_LIMIT),
        name="cd_proj_ctx",
    )(ctx, mod, w_kv, kv_norm, wk, wvt)


def _cd_proj(x, mod, w1, cs, q_norm, wqt, kv_norm, wk, wvt, cosqt, sinqt, cosk, sink, *, tl):
    bsz, seq, d = x.shape
    hp = HEAD_PAD
    fw = FNET_WIDTH
    tab = pl.BlockSpec((tl, LANES), lambda b, t: (t, 0))
    tab_t = pl.BlockSpec((LANES, tl), lambda b, t: (0, t))
    return pl.pallas_call(
        _cd_proj_kernel,
        grid=(bsz, seq // tl),
        in_specs=[pl.BlockSpec((1, tl, d), lambda b, t: (b, t, 0)),
                  pl.BlockSpec((1, 1, 3 * d), lambda b, t: (b, 0, 0)),
                  _const_spec(w1), _const_spec(cs), _const_spec(q_norm), _const_spec(wqt),
                  _const_spec(kv_norm), _const_spec(wk), _const_spec(wvt), tab_t, tab_t, tab, tab],
        out_specs=[pl.BlockSpec((1, 2, tl, fw), lambda b, t: (b, 0, t, 0)),
                   pl.BlockSpec((1, tl, fw), lambda b, t: (b, t, 0)),
                   pl.BlockSpec((1, tl, MLA_WIDTH), lambda b, t: (b, t, 0)),
                   pl.BlockSpec((1, MLA_HEADS, hp, tl), lambda b, t: (b, 0, 0, t)),
                   pl.BlockSpec((1, MLA_HEADS, tl, hp), lambda b, t: (b, 0, t, 0)),
                   pl.BlockSpec((1, MLA_WIDTH, tl), lambda b, t: (b, 0, t))],
        out_shape=[jax.ShapeDtypeStruct((bsz, 2, seq, fw), BF16),
                   jax.ShapeDtypeStruct((bsz, seq, fw), BF16),
                   jax.ShapeDtypeStruct((bsz, seq, MLA_WIDTH), BF16),
                   jax.ShapeDtypeStruct((bsz, MLA_HEADS, hp, seq), BF16),
                   jax.ShapeDtypeStruct((bsz, MLA_HEADS, seq, hp), BF16),
                   jax.ShapeDtypeStruct((bsz, MLA_WIDTH, seq), BF16)],
        compiler_params=pltpu.CompilerParams(dimension_semantics=("parallel", "arbitrary"),
                                             vmem_limit_bytes=VMEM_LIMIT),
        name="cd_proj",
    )(x, mod, w1, cs, q_norm, wqt, kv_norm, wk, wvt, cosqt, sinqt, cosk, sink)


DFT_FINE = 32


KEY_CHUNK = 256
SUBLANES = 8


def _fold_rows(s, op):
    while s.shape[0] > SUBLANES:
        half = s.shape[0] // 2
        s = op(s[:half], s[half:])
    return s


def _cd_main_kernel(qt_ref, kc_ref, kl_ref, vtc_ref, vtl_ref, xcs_ref, ac_ref, as_ref, bc_ref, bs_ref,
                    gf_ref, gd_ref, x_ref, mod_ref, fw_ref, wout_ref, lng_ref, lnb_ref,
                    o_ref, cls_ref, attnt_ref, s0_ref, s1_ref, *, tq, seq_len):
    bc, bs = bc_ref[...], bs_ref[...]
    for i in range(tq // DFT_FINE):
        ac, asn = ac_ref[i:i + 1, :], as_ref[i:i + 1, :]
        rows = slice(i * DFT_FINE, (i + 1) * DFT_FINE)
        cls_ref[rows, 0:seq_len] = (ac * bc - asn * bs).astype(BF16)
        cls_ref[rows, seq_len:2 * seq_len] = (asn * bc + ac * bs).astype(BF16)
    spec = _dot(cls_ref[...], xcs_ref[0])
    yc = _dot(spec.astype(BF16), fw_ref[...]) * gf_ref[0].astype(F32)

    lc = kc_ref.shape[2]
    n_chunks = (lc + seq_len) // KEY_CHUNK
    ctx_chunks = lc // KEY_CHUNK

    def chunk_of(ctx_ref, lat_ref, hd, c, lanes):
        ref, c = (ctx_ref, c) if c < ctx_chunks else (lat_ref, c - ctx_chunks)
        span = slice(c * KEY_CHUNK, (c + 1) * KEY_CHUNK)
        return ref[0, hd, :, span] if lanes else ref[0, hd, span, :]

    def score_chunk(hd, parity, c, m8):
        s = _dot(chunk_of(kc_ref, kl_ref, hd, c, False), qt_ref[0, hd])
        s_refs[parity][c * KEY_CHUNK:(c + 1) * KEY_CHUNK, :] = s
        part = _fold_rows(s, jnp.maximum)
        return part if m8 is None else jnp.maximum(m8, part)

    def prob_chunk(hd, parity, c, m, l8, acc):
        p = jnp.exp2(s_refs[parity][c * KEY_CHUNK:(c + 1) * KEY_CHUNK, :] - m)
        part = _fold_rows(p, jnp.add)
        o_t = _dot(chunk_of(vtc_ref, vtl_ref, hd, c, True), p.astype(BF16))
        return (part if l8 is None else l8 + part), (o_t if acc is None else acc + o_t)

    def head_step(hd, parity, m8, more):
        m = jnp.max(m8, axis=0, keepdims=True)
        m8, l8, acc = None, None, None
        for c in range(n_chunks):
            l8, acc = prob_chunk(hd, parity, c, m, l8, acc)
            if more:
                m8 = score_chunk(hd + 1, 1 - parity, c, m8)
        attnt_ref[hd] = acc * (1.0 / jnp.sum(l8, axis=0, keepdims=True))
        return m8

    def head_pair(j, m8):
        m8 = head_step(2 * j, 0, m8, True)
        return head_step(2 * j + 1, 1, m8, True)

    s_refs = (s0_ref, s1_ref)
    m8 = None
    for c in range(n_chunks):
        m8 = score_chunk(0, 0, c, m8)
    m8 = lax.fori_loop(0, MLA_HEADS // 2 - 1, head_pair, m8)
    m8 = head_step(MLA_HEADS - 2, 0, m8, True)
    head_step(MLA_HEADS - 1, 1, m8, False)
    yd = attnt_ref[...].reshape(MLA_WIDTH, tq).T * gd_ref[0].astype(F32)

    mix = jnp.concatenate([yc.astype(BF16), yd.astype(BF16)], axis=-1)
    y = _dot(mix, wout_ref[...])
    _, _, gate = _split_mod(mod_ref)
    o_ref[0] = _layer_norm_rows(ALPHA * x_ref[0] + gate * y, lng_ref[...], lnb_ref[...])


def _cd_main(qt, kc, kl, vtc, vtl, xcs, a_cos, a_sin, b_cos, b_sin, gf, gd, x, mod, fnet_w, w_out,
             ln_g, ln_b, *, tq):
    bsz, seq, d = x.shape
    lc = kc.shape[2]
    hp = HEAD_PAD
    nh = MLA_HEADS
    per_b4 = lambda b, t: (b, 0, 0, 0)
    per_b3 = lambda b, t: (b, 0, 0)
    kern = functools.partial(_cd_main_kernel, tq=tq, seq_len=seq)
    coarse = tq // DFT_FINE
    return pl.pallas_call(
        kern,
        grid=(bsz, seq // tq),
        in_specs=[pl.BlockSpec((1, nh, hp, tq), lambda b, t: (b, 0, 0, t)),
                  pl.BlockSpec((1, nh, lc, hp), per_b4),
                  pl.BlockSpec((1, nh, seq, hp), per_b4),
                  pl.BlockSpec((1, nh, V_DIM, lc), per_b4),
                  pl.BlockSpec((1, nh, V_DIM, seq), per_b4),
                  pl.BlockSpec((1, 2 * seq, FNET_WIDTH), per_b3),
                  pl.BlockSpec((coarse, seq), lambda b, t: (t, 0)),
                  pl.BlockSpec((coarse, seq), lambda b, t: (t, 0)),
                  _const_spec(b_cos), _const_spec(b_sin),
                  pl.BlockSpec((1, tq, FNET_WIDTH), lambda b, t: (b, t, 0)),
                  pl.BlockSpec((1, tq, MLA_WIDTH), lambda b, t: (b, t, 0)),
                  pl.BlockSpec((1, tq, d), lambda b, t: (b, t, 0)),
                  pl.BlockSpec((1, 1, 3 * d), lambda b, t: (b, 0, 0)),
                  _const_spec(fnet_w), _const_spec(w_out), _const_spec(ln_g), _const_spec(ln_b)],
        out_specs=pl.BlockSpec((1, tq, d), lambda b, t: (b, t, 0)),
        out_shape=jax.ShapeDtypeStruct(x.shape, F32),
        scratch_shapes=[pltpu.VMEM((tq, 2 * seq), BF16),
                        pltpu.VMEM((nh, V_DIM, tq), F32),
                        pltpu.VMEM((lc + seq, tq), F32),
                        pltpu.VMEM((lc + seq, tq), F32)],
        compiler_params=pltpu.CompilerParams(dimension_semantics=("parallel", "arbitrary"),
                                             vmem_limit_bytes=VMEM_LIMIT),
        name="cd_main",
    )(qt, kc, kl, vtc, vtl, xcs, a_cos, a_sin, b_cos, b_sin, gf, gd, x, mod, fnet_w, w_out, ln_g, ln_b)


def _rope_tables(seq):
    pos = np.arange(seq)
    row = (pos // GRID_W).astype(np.float64)
    col = (pos % GRID_W).astype(np.float64)
    half = ROPE_AXIS // 2
    inv = (ROPE_BASE ** (-np.arange(0, ROPE_AXIS, 2, dtype=np.float32) / ROPE_AXIS)).astype(np.float32)
    cos = np.ones((seq, LANES), np.float32)
    sin = np.zeros((seq, LANES), np.float32)
    for j in range(QK_ROPE):
        coord = row if j < ROPE_AXIS else col
        ang = (coord.astype(np.float32) * inv[j % half]).astype(np.float32)
        cos[:, QK_NOPE + j] = np.cos(ang)
        sign = -1.0 if (j % ROPE_AXIS) < half else 1.0
        sin[:, QK_NOPE + j] = sign * np.sin(ang)
    return cos, sin


def _rope_partner(j):
    half = ROPE_AXIS // 2
    return j + half if (j % ROPE_AXIS) < half else j - half


def _dft_tables(seq, tq):
    l = np.arange(seq, dtype=np.float64)
    k1 = np.arange(seq // DFT_FINE, dtype=np.float64)[:, None]
    k0 = np.arange(DFT_FINE, dtype=np.float64)[:, None]
    ang_a = 2.0 * np.pi * ((k1 * DFT_FINE * l) % seq) / seq
    ang_b = 2.0 * np.pi * ((k0 * l) % seq) / seq
    norm = 1.0 / np.sqrt(seq * GROUP_DIM)
    return (np.cos(ang_a).astype(np.float32), np.sin(ang_a).astype(np.float32),
            (np.cos(ang_b) * norm).astype(np.float32), (np.sin(ang_b) * norm).astype(np.float32))


def _channel_dft():
    c = np.arange(GROUP_DIM, dtype=np.float64)
    ang = 2.0 * np.pi * ((c[:, None] * c[None, :]) % GROUP_DIM) / GROUP_DIM
    eye = np.eye(FNET_HEADS)
    return np.concatenate([np.kron(eye, np.cos(ang)), -np.kron(eye, np.sin(ang))], axis=1).astype(np.float32)


def kernel(x, c, ctx, c_ctx, ab_w_mod, ab_b_mod, ab_w_in, ab_pool_w, ab_pool_scale, ab_sgu_w, ab_sgu_b,
           ab_w_out, ab_ln_g, ab_ln_b, cd_w_mod, cd_b_mod, cd_w_in, cd_fnet_w, cd_q_norm, cd_kv_norm,
           cd_w_q_up, cd_w_kv_up, cd_w_out, cd_ln_g, cd_ln_b):
    bsz, seq, d = x.shape
    nh, hp = MLA_HEADS, HEAD_PAD

    rows = -(-(bsz + 1) // 8) * 8
    cond = jnp.zeros((rows, d), F32).at[:bsz].set(c).at[bsz].set(c_ctx)
    mod_ab = _modulation(cond, ab_w_mod[0], ab_b_mod[0])
    mod_cd = _modulation(cond, cd_w_mod[0], cd_b_mod[0])
    lat_row = lambda b: b
    ctx_row = lambda b: bsz

    ab_args = (ab_w_in[0].astype(BF16), ab_pool_w[0].astype(BF16), ab_pool_scale[0].reshape(1, -1),
               ab_sgu_w[0].astype(BF16),
               jnp.broadcast_to(ab_sgu_b[0][:, :, None], (SGU_HEADS, CHUNK, GROUP_DIM)),
               ab_w_out[0].astype(BF16), ab_ln_g[0].reshape(1, -1), ab_ln_b[0].reshape(1, -1))
    x1 = _ab_layer(x, mod_ab, lat_row, *ab_args, tl=512)
    ctx1 = _ab_layer(ctx, mod_ab, ctx_row, *ab_args, tl=ctx.shape[1])

    w = cd_w_in[0]
    o_cq = 2 * FNET_WIDTH
    o_ckv = o_cq + Q_LORA
    o_kr = o_ckv + KV_LORA
    o_dg = o_kr + QK_ROPE
    partner = np.array([_rope_partner(j) for j in range(QK_ROPE)])
    zeros = lambda n: jnp.zeros((d, n), F32)
    kr_cols = w[:, o_kr:o_dg]
    kr_lanes = jnp.concatenate([zeros(QK_NOPE), kr_cols, zeros(LANES - QK_NOPE - QK_ROPE)], axis=1)
    kr_lanes_sw = jnp.concatenate([zeros(QK_NOPE), kr_cols[:, partner], zeros(LANES - QK_NOPE - QK_ROPE)], axis=1)
    w1 = jnp.concatenate([w[:, 0:o_cq], w[:, o_dg:], w[:, o_cq:o_kr], kr_lanes, kr_lanes_sw], axis=1).astype(BF16)
    w_kv_ctx = jnp.concatenate([w[:, o_ckv:o_kr], kr_lanes], axis=1).astype(BF16)

    wq = cd_w_q_up[0].reshape(Q_LORA, nh, QK_NOPE + QK_ROPE)
    pad_q = jnp.zeros((Q_LORA, nh, hp - QK_NOPE - QK_ROPE), F32)
    wq_pad = jnp.concatenate([wq, pad_q], axis=-1)
    wq_sw = jnp.concatenate([jnp.zeros((Q_LORA, nh, QK_NOPE), F32), wq[:, :, QK_NOPE + partner], pad_q], axis=-1)
    wqt = jnp.concatenate([wq_pad.reshape(Q_LORA, nh * hp), wq_sw.reshape(Q_LORA, nh * hp)],
                          axis=1).T.astype(BF16)

    wkv = cd_w_kv_up[0].reshape(KV_LORA, nh, QK_NOPE + V_DIM)
    wk_pad = jnp.concatenate([wkv[:, :, :QK_NOPE], jnp.zeros((KV_LORA, nh, hp - QK_NOPE), F32)], axis=-1)
    lane = np.arange(hp)
    sel = np.diag(((lane >= QK_NOPE) & (lane < QK_NOPE + QK_ROPE)).astype(np.float32))
    wk = jnp.concatenate([wk_pad.reshape(KV_LORA, nh * hp),
                          jnp.asarray(np.tile(sel, (1, nh)))], axis=0).astype(BF16)
    wvt = wkv[:, :, QK_NOPE:].reshape(KV_LORA, nh * V_DIM).T.astype(BF16)

    q_norm = cd_q_norm[0].reshape(1, -1)
    kv_norm = cd_kv_norm[0].reshape(1, -1)
    cos, sin = _rope_tables(seq)
    qscale = np.float32((QK_NOPE + QK_ROPE) ** -0.5 * np.log2(np.e))
    cs = jnp.asarray(_channel_dft()).astype(BF16)

    kc, vtc = _cd_proj_ctx(ctx1, mod_cd, ctx_row, w_kv_ctx, kv_norm, wk, wvt)
    xcs, gf, gd, qt, kl, vtl = _cd_proj(x1, mod_cd, w1, cs, q_norm, wqt, kv_norm, wk, wvt,
                                        jnp.asarray((cos * qscale).T.copy()), jnp.asarray((sin * qscale).T.copy()),
                                        jnp.asarray(cos), jnp.asarray(sin), tl=512)
    tq = 256
    a_cos, a_sin, b_cos, b_sin = (jnp.asarray(t) for t in _dft_tables(seq, tq))
    vtc = vtc.reshape(bsz, nh, V_DIM, -1)
    vtl = vtl.reshape(bsz, nh, V_DIM, -1)
    return _cd_main(qt, kc, kl, vtc, vtl, xcs.reshape(bsz, 2 * seq, FNET_WIDTH), a_cos, a_sin, b_cos, b_sin,
                    gf, gd, x1, mod_cd, cd_fnet_w[0].astype(BF16), cd_w_out[0].astype(BF16),
                    cd_ln_g[0].reshape(1, -1), cd_ln_b[0].reshape(1, -1), tq=tq)
```

```python
import functools

import numpy as np
import jax
import jax.numpy as jnp
from jax import lax
from jax.experimental import pallas as pl
from jax.experimental.pallas import tpu as pltpu

D_MODEL = 1024
GRID_W = 64
EPS = 1e-6
DEPTH = 2
POOL_WINDOWS = (2, 4, 8, 16)
GROUP_DIM = 128
POOL_WIDTH = 512
SGU_HEADS = 4
SGU_WIDTH = 512
CHUNK = 128
FNET_HEADS = 4
FNET_WIDTH = 512
MLA_HEADS = 8
QK_NOPE = 64
QK_ROPE = 32
V_DIM = 64
Q_LORA = 256
KV_LORA = 128
MLA_WIDTH = 512
ROPE_AXIS = QK_ROPE // 2
ROPE_BASE = 10000.0
ALPHA = (2 * DEPTH) ** 0.25
Q_SCALE = float((QK_NOPE + QK_ROPE) ** -0.5 * np.log2(np.e))

LANES = 128
HALO = 16
HEAD_PAD = LANES
OUT_SLAB = 256
VMEM_LIMIT = 56 * 1024 * 1024

F32 = jnp.float32
BF16 = jnp.bfloat16


def _silu(v):
    half = 0.5 * v
    return half * (1.0 + jnp.tanh(half))


def _dot(a, b):
    return jnp.dot(a, b, preferred_element_type=F32)


def _deep_norm_rows(x, y, gate_scaled, g, b):
    r = x + gate_scaled * y
    mu = jnp.mean(r, axis=-1, keepdims=True)
    rc = r - mu
    var = jnp.mean(rc * rc, axis=-1, keepdims=True)
    return rc * lax.rsqrt(var + EPS / (ALPHA * ALPHA)) * g + b


def _rms_rows(v, g):
    return v * lax.rsqrt(jnp.mean(v * v, axis=-1, keepdims=True) + EPS) * g


def _mod_kernel(c_ref, w_ref, b_ref, o_ref):
    s = _silu(c_ref[...]).astype(BF16)
    o_ref[...] = _dot(s, w_ref[...].astype(BF16)) + b_ref[...]


def _modulation(cond, w_mod, b_mod):
    rows, d = cond.shape
    n = w_mod.shape[1]
    tn = 768
    out = pl.pallas_call(
        _mod_kernel,
        grid=(n // tn,),
        in_specs=[pl.BlockSpec((rows, d), lambda j: (0, 0)),
                  pl.BlockSpec((d, tn), lambda j: (0, j)),
                  pl.BlockSpec((1, tn), lambda j: (0, j))],
        out_specs=pl.BlockSpec((rows, tn), lambda j: (0, j)),
        out_shape=jax.ShapeDtypeStruct((rows, n), F32),
        name="modulation",
    )(cond, w_mod, b_mod.reshape(1, n))
    return out.reshape(rows, 1, n)


def _split_mod(mod_ref):
    d = D_MODEL
    return mod_ref[0, :, 0:d], 1.0 + mod_ref[0, :, d:2 * d], mod_ref[0, :, 2 * d:3 * d]


def _ab_kernel(x_ref, xp_ref, xn_ref, mod_ref, win_ref, poolw_ref, pscale_ref, sguw_ref, sgub_ref,
               wout_ref, lng_ref, lnb_ref, o_ref, h_ref, a_ref, mix_ref, *, tl, seq_len):
    t = pl.program_id(1)
    nt = pl.num_programs(1)
    shift, scale1, gate = _split_mod(mod_ref)
    x = x_ref[0]
    h_ref[HALO:HALO + tl, :] = (x * scale1 + shift).astype(BF16)
    h_ref[0:HALO, :] = (xp_ref[0] * scale1 + shift).astype(BF16)
    h_ref[HALO + tl:2 * HALO + tl, :] = (xn_ref[0] * scale1 + shift).astype(BF16)

    a_ext = _dot(h_ref[...], win_ref[:, 0:POOL_WIDTH])
    a_ref[HALO:HALO + tl, :] = a_ext[HALO:HALO + tl]
    a_ref[0:HALO, :] = jnp.where(t > 0, a_ext[0:HALO], 0.0)
    a_ref[HALO + tl:2 * HALO + tl, :] = jnp.where(t < nt - 1, a_ext[HALO + tl:2 * HALO + tl], 0.0)

    base = 2 * POOL_WIDTH
    slabs = [slice(r0, r0 + OUT_SLAB) for r0 in range(0, tl, OUT_SLAB)]
    wide = []
    for rows in slabs:
        h = h_ref[HALO + rows.start:HALO + rows.stop, :]
        v = _dot(h, win_ref[:, base + SGU_WIDTH:base + 2 * SGU_WIDTH])
        u = _dot(h, win_ref[:, base:base + SGU_WIDTH])
        gate_a = _silu(_dot(h, win_ref[:, POOL_WIDTH:2 * POOL_WIDTH]))
        gate_b = _silu(_dot(h, win_ref[:, base + 2 * SGU_WIDTH:base + 3 * SGU_WIDTH]))
        wide.append((v, u, gate_a, gate_b))

    for rows, (v, u, gate_a, gate_b) in zip(slabs, wide):
        n = rows.stop - rows.start
        pos = t * tl + rows.start + lax.broadcasted_iota(jnp.int32, (SUBLANES, LANES), 0)
        for g, w in enumerate(POOL_WINDOWS):
            cols = slice(g * GROUP_DIM, (g + 1) * GROUP_DIM)
            first = HALO + rows.start - w // 2
            s = a_ref[first:first + n, cols]
            for j in range(1, w):
                s = s + a_ref[first + j:first + j + n, cols]
            edges = []
            for r0 in (0, n - SUBLANES):
                p = pos + r0
                count = jnp.minimum(p + (w - w // 2), seq_len) - jnp.maximum(p - w // 2, 0)
                edges.append(s[r0:r0 + SUBLANES] / count.astype(F32))
            pooled = jnp.concatenate([edges[0], s[SUBLANES:n - SUBLANES] * (1.0 / w), edges[1]], axis=0)
            diff = pooled - a_ref[HALO + rows.start:HALO + rows.stop, cols]
            ya = _dot(diff.astype(BF16), poolw_ref[g]) * pscale_ref[:, cols] * gate_a[:, cols]
            mix_ref[rows, cols] = ya.astype(BF16)

        for hd in range(SGU_HEADS):
            cols = slice(hd * GROUP_DIM, (hd + 1) * GROUP_DIM)
            vh = v[:, cols]
            mu = jnp.mean(vh, axis=-1, keepdims=True)
            vc = vh - mu
            var = jnp.mean(vc * vc, axis=-1, keepdims=True)
            vn = (vc * lax.rsqrt(var + EPS)).astype(BF16)
            f = jnp.concatenate(
                [_dot(sguw_ref[hd], vn[ci * CHUNK:(ci + 1) * CHUNK, :]) + sgub_ref[hd]
                 for ci in range(n // CHUNK)], axis=0)
            yb = u[:, cols] * f * gate_b[:, cols]
            mix_ref[rows, POOL_WIDTH + hd * GROUP_DIM:POOL_WIDTH + (hd + 1) * GROUP_DIM] = yb.astype(BF16)

        y = _dot(mix_ref[rows, :], wout_ref[...])
        o_ref[0, rows, :] = _deep_norm_rows(x_ref[0, rows, :], y, gate * (1.0 / ALPHA), lng_ref[...], lnb_ref[...])


def _ab_layer(x, mod, mod_row, w_in, pool_w, pool_scale, sgu_w, sgu_b, w_out, ln_g, ln_b, *, tl):
    bsz, seq, d = x.shape
    nt = seq // tl
    hb = tl // HALO
    last_hb = seq // HALO - 1
    const2 = lambda b, t: (0, 0)
    const3 = lambda b, t: (0, 0, 0)
    kern = functools.partial(_ab_kernel, tl=tl, seq_len=seq)
    return pl.pallas_call(
        kern,
        grid=(bsz, nt),
        in_specs=[
            pl.BlockSpec((1, tl, d), lambda b, t: (b, t, 0)),
            pl.BlockSpec((1, HALO, d), lambda b, t: (b, jnp.maximum(t * hb - 1, 0), 0)),
            pl.BlockSpec((1, HALO, d), lambda b, t: (b, jnp.minimum((t + 1) * hb, last_hb), 0)),
            pl.BlockSpec((1, 1, 3 * d), lambda b, t: (mod_row(b), 0, 0)),
            pl.BlockSpec(w_in.shape, const2),
            pl.BlockSpec(pool_w.shape, const3),
            pl.BlockSpec(pool_scale.shape, const2),
            pl.BlockSpec(sgu_w.shape, const3),
            pl.BlockSpec(sgu_b.shape, const3),
            pl.BlockSpec(w_out.shape, const2),
            pl.BlockSpec(ln_g.shape, const2),
            pl.BlockSpec(ln_b.shape, const2),
        ],
        out_specs=pl.BlockSpec((1, tl, d), lambda b, t: (b, t, 0)),
        out_shape=jax.ShapeDtypeStruct(x.shape, F32),
        scratch_shapes=[pltpu.VMEM((tl + 2 * HALO, d), BF16),
                        pltpu.VMEM((tl + 2 * HALO, POOL_WIDTH), F32),
                        pltpu.VMEM((tl, POOL_WIDTH + SGU_WIDTH), BF16)],
        compiler_params=pltpu.CompilerParams(dimension_semantics=("parallel", "arbitrary"),
                                             vmem_limit_bytes=VMEM_LIMIT),
        name="ab_layer",
    )(x, x, x, mod, w_in, pool_w, pool_scale, sgu_w, sgu_b, w_out, ln_g, ln_b)


def _dot_nt(a, b):
    return lax.dot_general(a, b, (((1,), (1,)), ((), ())), preferred_element_type=F32)


def _store_kv(ckv, kr_lanes, kvnorm_ref, wk_ref, wvt_ref, k_ref, vt_ref):
    ckv_n = _rms_rows(ckv, kvnorm_ref[...]).astype(BF16)
    hk = jnp.concatenate([ckv_n, kr_lanes.astype(BF16)], axis=-1)
    kk = _dot(hk, wk_ref[...]).astype(BF16)
    for hd in range(MLA_HEADS):
        k_ref[0, hd] = kk[:, hd * HEAD_PAD:(hd + 1) * HEAD_PAD]
    vt_ref[0] = _dot_nt(wvt_ref[...], ckv_n).astype(BF16)


def _cd_proj_ctx_kernel(x_ref, mod_ref, w_ref, kvnorm_ref, wk_ref, wvt_ref, k_ref, vt_ref):
    shift, scale1, _ = _split_mod(mod_ref)
    h = (x_ref[0] * scale1 + shift).astype(BF16)
    z = _dot(h, w_ref[...])
    _store_kv(z[:, 0:KV_LORA], z[:, KV_LORA:KV_LORA + LANES], kvnorm_ref, wk_ref, wvt_ref, k_ref, vt_ref)


def _cd_proj_kernel(x_ref, mod_ref, w_ref, cs_ref, qnorm_ref, wqt_ref, kvnorm_ref, wk_ref, wvt_ref,
                    cosqt_ref, sinqt_ref, cosk_ref, sink_ref,
                    xcs_ref, gf_ref, gd_ref, qt_ref, k_ref, vt_ref):
    shift, scale1, _ = _split_mod(mod_ref)
    h = (x_ref[0] * scale1 + shift).astype(BF16)
    fw = FNET_WIDTH
    o = 2 * fw + MLA_WIDTH
    small = _dot(h, w_ref[:, o:o + Q_LORA + KV_LORA + LANES])
    cq = small[:, 0:Q_LORA]
    ckv = small[:, Q_LORA:Q_LORA + KV_LORA]
    kr = small[:, Q_LORA + KV_LORA:Q_LORA + KV_LORA + LANES]
    f_in = _dot(h, w_ref[:, 0:fw]).astype(BF16)
    gf_ref[0] = _silu(_dot(h, w_ref[:, fw:2 * fw])).astype(BF16)
    gd_ref[0] = _silu(_dot(h, w_ref[:, 2 * fw:2 * fw + MLA_WIDTH])).astype(BF16)

    qn = _rms_rows(cq, qnorm_ref[...]).astype(BF16)
    q2t = _dot_nt(wqt_ref[...], qn)
    head_dim = QK_NOPE + QK_ROPE
    swap0 = MLA_HEADS * head_dim
    cosqt, sinqt = cosqt_ref[...], sinqt_ref[...]
    tl = q2t.shape[1]
    pad = jnp.zeros((HEAD_PAD - head_dim, tl), F32)
    for hd in range(MLA_HEADS):
        nope = q2t[hd * head_dim:hd * head_dim + QK_NOPE, :]
        rope = q2t[hd * head_dim + QK_NOPE:(hd + 1) * head_dim, :]
        rope_sw = q2t[swap0 + hd * QK_ROPE:swap0 + (hd + 1) * QK_ROPE, :]
        qt_ref[0, hd] = jnp.concatenate(
            [nope * Q_SCALE, rope * cosqt + rope_sw * sinqt, pad], axis=0).astype(BF16)

    for hd in range(FNET_HEADS):
        cols = slice(hd * GROUP_DIM, (hd + 1) * GROUP_DIM)
        xcs = _dot(f_in[:, cols], cs_ref[...]).astype(BF16)
        xcs_ref[0, 0, :, cols] = xcs[:, 0:GROUP_DIM]
        xcs_ref[0, 1, :, cols] = xcs[:, GROUP_DIM:2 * GROUP_DIM]

    kr_rot = kr * cosk_ref[...] + pltpu.roll(kr, LANES // 2, 1) * sink_ref[...]
    _store_kv(ckv, kr_rot, kvnorm_ref, wk_ref, wvt_ref, k_ref, vt_ref)


def _const_spec(a):
    nd = a.ndim
    return pl.BlockSpec(a.shape, lambda b, t: (0,) * nd)


def _cd_proj_ctx(ctx, mod, mod_row, w_kv, kv_norm, wk, wvt):
    bsz, seq, d = ctx.shape
    hp = HEAD_PAD
    return pl.pallas_call(
        _cd_proj_ctx_kernel,
        grid=(bsz, 1),
        in_specs=[pl.BlockSpec((1, seq, d), lambda b, t: (b, 0, 0)),
                  pl.BlockSpec((1, 1, 3 * d), lambda b, t: (mod_row(b), 0, 0)),
                  _const_spec(w_kv), _const_spec(kv_norm), _const_spec(wk), _const_spec(wvt)],
        out_specs=[pl.BlockSpec((1, MLA_HEADS, seq, hp), lambda b, t: (b, 0, 0, 0)),
                   pl.BlockSpec((1, MLA_WIDTH, seq), lambda b, t: (b, 0, 0))],
        out_shape=[jax.ShapeDtypeStruct((bsz, MLA_HEADS, seq, hp), BF16),
                   jax.ShapeDtypeStruct((bsz, MLA_WIDTH, seq), BF16)],
        compiler_params=pltpu.CompilerParams(dimension_semantics=("parallel", "arbitrary"),
                                             vmem_limit_bytes=VMEM_LIMIT),
        name="cd_proj_ctx",
    )(ctx, mod, w_kv, kv_norm, wk, wvt)


def _cd_proj(x, mod, w1, cs, q_norm, wqt, kv_norm, wk, wvt, cosqt, sinqt, cosk, sink, *, tl):
    bsz, seq, d = x.shape
    hp = HEAD_PAD
    fw = FNET_WIDTH
    tab = pl.BlockSpec((tl, LANES), lambda b, t: (t, 0))
    tab_t = pl.BlockSpec((QK_ROPE, tl), lambda b, t: (0, t))
    return pl.pallas_call(
        _cd_proj_kernel,
        grid=(bsz, seq // tl),
        in_specs=[pl.BlockSpec((1, tl, d), lambda b, t: (b, t, 0)),
                  pl.BlockSpec((1, 1, 3 * d), lambda b, t: (b, 0, 0)),
                  _const_spec(w1), _const_spec(cs), _const_spec(q_norm), _const_spec(wqt),
                  _const_spec(kv_norm), _const_spec(wk), _const_spec(wvt), tab_t, tab_t, tab, tab],
        out_specs=[pl.BlockSpec((1, 2, tl, fw), lambda b, t: (b, 0, t, 0)),
                   pl.BlockSpec((1, tl, fw), lambda b, t: (b, t, 0)),
                   pl.BlockSpec((1, tl, MLA_WIDTH), lambda b, t: (b, t, 0)),
                   pl.BlockSpec((1, MLA_HEADS, hp, tl), lambda b, t: (b, 0, 0, t)),
                   pl.BlockSpec((1, MLA_HEADS, tl, hp), lambda b, t: (b, 0, t, 0)),
                   pl.BlockSpec((1, MLA_WIDTH, tl), lambda b, t: (b, 0, t))],
        out_shape=[jax.ShapeDtypeStruct((bsz, 2, seq, fw), BF16),
                   jax.ShapeDtypeStruct((bsz, seq, fw), BF16),
                   jax.ShapeDtypeStruct((bsz, seq, MLA_WIDTH), BF16),
                   jax.ShapeDtypeStruct((bsz, MLA_HEADS, hp, seq), BF16),
                   jax.ShapeDtypeStruct((bsz, MLA_HEADS, seq, hp), BF16),
                   jax.ShapeDtypeStruct((bsz, MLA_WIDTH, seq), BF16)],
        compiler_params=pltpu.CompilerParams(dimension_semantics=("parallel", "arbitrary"),
                                             vmem_limit_bytes=VMEM_LIMIT),
        name="cd_proj",
    )(x, mod, w1, cs, q_norm, wqt, kv_norm, wk, wvt, cosqt, sinqt, cosk, sink)


DFT_FINE = 32


KEY_CHUNK = 256
SUBLANES = 8
BF16_SUBLANES = 16


def _fold_rows(s, op):
    while s.shape[0] > SUBLANES:
        half = s.shape[0] // 2
        s = op(s[:half], s[half:])
    return s


def _cd_main_kernel(qt_ref, kc_ref, kl_ref, vtc_ref, vtl_ref, xcs_ref, ac_ref, as_ref, bc_ref, bs_ref,
                    gf_ref, gd_ref, x_ref, mod_ref, fw_ref, wout_ref, lng_ref, lnb_ref,
                    o_ref, cls_ref, attnt_ref, s0_ref, s1_ref, *, tq, seq_len):
    bc, bs = bc_ref[...], bs_ref[...]
    for i in range(tq // DFT_FINE):
        ac, asn = ac_ref[i:i + 1, :], as_ref[i:i + 1, :]
        rows = slice(i * DFT_FINE, (i + 1) * DFT_FINE)
        cls_ref[rows, 0:seq_len] = (ac * bc - asn * bs).astype(BF16)
        cls_ref[rows, seq_len:2 * seq_len] = (asn * bc + ac * bs).astype(BF16)
    spec = _dot(cls_ref[...], xcs_ref[0])
    yc = _dot(spec.astype(BF16), fw_ref[...]) * gf_ref[0].astype(F32)

    lc = kc_ref.shape[2]
    n_chunks = (lc + seq_len) // KEY_CHUNK
    ctx_chunks = lc // KEY_CHUNK

    def chunk_of(ctx_ref, lat_ref, hd, c, lanes):
        ref, c = (ctx_ref, c) if c < ctx_chunks else (lat_ref, c - ctx_chunks)
        span = slice(c * KEY_CHUNK, (c + 1) * KEY_CHUNK)
        return ref[0, hd, :, span] if lanes else ref[0, hd, span, :]

    def score_chunk(s_ref, hd, c, m8):
        s = _dot(chunk_of(kc_ref, kl_ref, hd, c, False), qt_ref[0, hd])
        s_ref[c * KEY_CHUNK:(c + 1) * KEY_CHUNK, :] = s
        part = _fold_rows(s, jnp.maximum)
        return part if m8 is None else jnp.maximum(m8, part)

    ones_rows = jnp.ones((BF16_SUBLANES, KEY_CHUNK), BF16)

    def prob_chunk(s_ref, hd, c, m, acc):
        p = jnp.exp2(s_ref[c * KEY_CHUNK:(c + 1) * KEY_CHUNK, :] - m)
        vt_ext = jnp.concatenate([chunk_of(vtc_ref, vtl_ref, hd, c, True), ones_rows], axis=0)
        o_t = _dot(vt_ext, p.astype(BF16))
        return o_t if acc is None else acc + o_t

    def head_step(s_ref, hd, m8, more):
        m = jnp.max(m8, axis=0, keepdims=True)
        m8, acc = None, None
        for c in range(n_chunks):
            acc = prob_chunk(s_ref, hd, c, m, acc)
            if more:
                m8 = score_chunk(s_ref, hd + 2, c, m8)
        attnt_ref[hd] = acc[0:V_DIM] * (1.0 / acc[V_DIM:V_DIM + 1])
        return m8

    def head_pair(j, carry):
        return (head_step(s0_ref, 2 * j, carry[0], True), head_step(s1_ref, 2 * j + 1, carry[1], True))

    m8_even, m8_odd = None, None
    for c in range(n_chunks):
        m8_even = score_chunk(s0_ref, 0, c, m8_even)
        m8_odd = score_chunk(s1_ref, 1, c, m8_odd)
    m8_even, m8_odd = lax.fori_loop(0, MLA_HEADS // 2 - 1, head_pair, (m8_even, m8_odd))
    head_step(s0_ref, MLA_HEADS - 2, m8_even, False)
    head_step(s1_ref, MLA_HEADS - 1, m8_odd, False)
    yd = attnt_ref[...].reshape(MLA_WIDTH, tq).T * gd_ref[0].astype(F32)

    mix = jnp.concatenate([yc.astype(BF16), yd.astype(BF16)], axis=-1)
    y = _dot(mix, wout_ref[...])
    _, _, gate = _split_mod(mod_ref)
    o_ref[0] = _deep_norm_rows(x_ref[0], y, gate * (1.0 / ALPHA), lng_ref[...], lnb_ref[...])


def _cd_main(qt, kc, kl, vtc, vtl, xcs, a_cos, a_sin, b_cos, b_sin, gf, gd, x, mod, fnet_w, w_out,
             ln_g, ln_b, *, tq):
    bsz, seq, d = x.shape
    lc = kc.shape[2]
    hp = HEAD_PAD
    nh = MLA_HEADS
    per_b4 = lambda b, t: (b, 0, 0, 0)
    per_b3 = lambda b, t: (b, 0, 0)
    kern = functools.partial(_cd_main_kernel, tq=tq, seq_len=seq)
    coarse = tq // DFT_FINE
    return pl.pallas_call(
        kern,
        grid=(bsz, seq // tq),
        in_specs=[pl.BlockSpec((1, nh, hp, tq), lambda b, t: (b, 0, 0, t)),
                  pl.BlockSpec((1, nh, lc, hp), per_b4),
                  pl.BlockSpec((1, nh, seq, hp), per_b4),
                  pl.BlockSpec((1, nh, V_DIM, lc), per_b4),
                  pl.BlockSpec((1, nh, V_DIM, seq), per_b4),
                  pl.BlockSpec((1, 2 * seq, FNET_WIDTH), per_b3),
                  pl.BlockSpec((coarse, seq), lambda b, t: (t, 0)),
                  pl.BlockSpec((coarse, seq), lambda b, t: (t, 0)),
                  _const_spec(b_cos), _const_spec(b_sin),
                  pl.BlockSpec((1, tq, FNET_WIDTH), lambda b, t: (b, t, 0)),
                  pl.BlockSpec((1, tq, MLA_WIDTH), lambda b, t: (b, t, 0)),
                  pl.BlockSpec((1, tq, d), lambda b, t: (b, t, 0)),
                  pl.BlockSpec((1, 1, 3 * d), lambda b, t: (b, 0, 0)),
                  _const_spec(fnet_w), _const_spec(w_out), _const_spec(ln_g), _const_spec(ln_b)],
        out_specs=pl.BlockSpec((1, tq, d), lambda b, t: (b, t, 0)),
        out_shape=jax.ShapeDtypeStruct(x.shape, F32),
        scratch_shapes=[pltpu.VMEM((tq, 2 * seq), BF16),
                        pltpu.VMEM((nh, V_DIM, tq), F32),
                        pltpu.VMEM((lc + seq, tq), F32),
                        pltpu.VMEM((lc + seq, tq), F32)],
        compiler_params=pltpu.CompilerParams(dimension_semantics=("parallel", "arbitrary"),
                                             vmem_limit_bytes=VMEM_LIMIT),
        name="cd_main",
    )(qt, kc, kl, vtc, vtl, xcs, a_cos, a_sin, b_cos, b_sin, gf, gd, x, mod, fnet_w, w_out, ln_g, ln_b)


def _rope_tables(seq):
    pos = np.arange(seq)
    row = (pos // GRID_W).astype(np.float64)
    col = (pos % GRID_W).astype(np.float64)
    half = ROPE_AXIS // 2
    inv = (ROPE_BASE ** (-np.arange(0, ROPE_AXIS, 2, dtype=np.float32) / ROPE_AXIS)).astype(np.float32)
    cos = np.ones((seq, LANES), np.float32)
    sin = np.zeros((seq, LANES), np.float32)
    for j in range(QK_ROPE):
        coord = row if j < ROPE_AXIS else col
        ang = (coord.astype(np.float32) * inv[j % half]).astype(np.float32)
        cos[:, QK_NOPE + j] = np.cos(ang)
        sign = -1.0 if (j % ROPE_AXIS) < half else 1.0
        sin[:, QK_NOPE + j] = sign * np.sin(ang)
    return cos, sin


def _rope_partner(j):
    half = ROPE_AXIS // 2
    return j + half if (j % ROPE_AXIS) < half else j - half


def _dft_tables(seq, tq):
    l = np.arange(seq, dtype=np.float64)
    k1 = np.arange(seq // DFT_FINE, dtype=np.float64)[:, None]
    k0 = np.arange(DFT_FINE, dtype=np.float64)[:, None]
    ang_a = 2.0 * np.pi * ((k1 * DFT_FINE * l) % seq) / seq
    ang_b = 2.0 * np.pi * ((k0 * l) % seq) / seq
    norm = 1.0 / np.sqrt(seq * GROUP_DIM)
    return (np.cos(ang_a).astype(np.float32), np.sin(ang_a).astype(np.float32),
            (np.cos(ang_b) * norm).astype(np.float32), (np.sin(ang_b) * norm).astype(np.float32))


def _channel_dft():
    c = np.arange(GROUP_DIM, dtype=np.float64)
    ang = 2.0 * np.pi * ((c[:, None] * c[None, :]) % GROUP_DIM) / GROUP_DIM
    return np.concatenate([np.cos(ang), -np.sin(ang)], axis=1).astype(np.float32)


def kernel(x, c, ctx, c_ctx, ab_w_mod, ab_b_mod, ab_w_in, ab_pool_w, ab_pool_scale, ab_sgu_w, ab_sgu_b,
           ab_w_out, ab_ln_g, ab_ln_b, cd_w_mod, cd_b_mod, cd_w_in, cd_fnet_w, cd_q_norm, cd_kv_norm,
           cd_w_q_up, cd_w_kv_up, cd_w_out, cd_ln_g, cd_ln_b):
    bsz, seq, d = x.shape
    nh, hp = MLA_HEADS, HEAD_PAD

    rows = -(-(bsz + 1) // 8) * 8
    cond = jnp.zeros((rows, d), F32).at[:bsz].set(c).at[bsz].set(c_ctx)
    mod_ab = _modulation(cond, ab_w_mod[0], ab_b_mod[0])
    mod_cd = _modulation(cond, cd_w_mod[0], cd_b_mod[0])
    lat_row = lambda b: b
    ctx_row = lambda b: bsz

    ab_args = (ab_w_in[0].astype(BF16), ab_pool_w[0].astype(BF16), ab_pool_scale[0].reshape(1, -1),
               ab_sgu_w[0].astype(BF16),
               jnp.broadcast_to(ab_sgu_b[0][:, :, None], (SGU_HEADS, CHUNK, GROUP_DIM)),
               ab_w_out[0].astype(BF16), ab_ln_g[0].reshape(1, -1), ab_ln_b[0].reshape(1, -1))
    x1 = _ab_layer(x, mod_ab, lat_row, *ab_args, tl=512)
    ctx1 = _ab_layer(ctx, mod_ab, ctx_row, *ab_args, tl=ctx.shape[1])

    w = cd_w_in[0]
    o_cq = 2 * FNET_WIDTH
    o_ckv = o_cq + Q_LORA
    o_kr = o_ckv + KV_LORA
    o_dg = o_kr + QK_ROPE
    partner = np.array([_rope_partner(j) for j in range(QK_ROPE)])
    zeros = lambda n: jnp.zeros((d, n), F32)
    kr_cols = w[:, o_kr:o_dg]
    kr_lanes = jnp.concatenate([zeros(QK_NOPE), kr_cols, zeros(LANES - QK_NOPE - QK_ROPE)], axis=1)
    kr_both = jnp.concatenate([kr_cols[:, partner], zeros(QK_NOPE - QK_ROPE), kr_cols,
                               zeros(LANES - QK_NOPE - QK_ROPE)], axis=1)
    w1 = jnp.concatenate([w[:, 0:o_cq], w[:, o_dg:], w[:, o_cq:o_kr], kr_both], axis=1).astype(BF16)
    w_kv_ctx = jnp.concatenate([w[:, o_ckv:o_kr], kr_lanes], axis=1).astype(BF16)

    wq = cd_w_q_up[0].reshape(Q_LORA, nh, QK_NOPE + QK_ROPE)
    wqt = jnp.concatenate([wq.reshape(Q_LORA, -1), wq[:, :, QK_NOPE + partner].reshape(Q_LORA, -1)],
                          axis=1).T.astype(BF16)

    wkv = cd_w_kv_up[0].reshape(KV_LORA, nh, QK_NOPE + V_DIM)
    wk_pad = jnp.concatenate([wkv[:, :, :QK_NOPE], jnp.zeros((KV_LORA, nh, hp - QK_NOPE), F32)], axis=-1)
    lane = np.arange(hp)
    sel = np.diag(((lane >= QK_NOPE) & (lane < QK_NOPE + QK_ROPE)).astype(np.float32))
    wk = jnp.concatenate([wk_pad.reshape(KV_LORA, nh * hp),
                          jnp.asarray(np.tile(sel, (1, nh)))], axis=0).astype(BF16)
    wvt = wkv[:, :, QK_NOPE:].reshape(KV_LORA, nh * V_DIM).T.astype(BF16)

    q_norm = cd_q_norm[0].reshape(1, -1)
    kv_norm = cd_kv_norm[0].reshape(1, -1)
    cos, sin = _rope_tables(seq)
    rope_lanes = slice(QK_NOPE, QK_NOPE + QK_ROPE)
    cosq_t = (cos[:, rope_lanes] * np.float32(Q_SCALE)).T.copy()
    sinq_t = (sin[:, rope_lanes] * np.float32(Q_SCALE)).T.copy()
    cs = jnp.asarray(_channel_dft()).astype(BF16)

    kc, vtc = _cd_proj_ctx(ctx1, mod_cd, ctx_row, w_kv_ctx, kv_norm, wk, wvt)
    xcs, gf, gd, qt, kl, vtl = _cd_proj(x1, mod_cd, w1, cs, q_norm, wqt, kv_norm, wk, wvt,
                                        jnp.asarray(cosq_t), jnp.asarray(sinq_t),
                                        jnp.asarray(cos), jnp.asarray(sin), tl=512)
    tq = 256
    a_cos, a_sin, b_cos, b_sin = (jnp.asarray(t) for t in _dft_tables(seq, tq))
    vtc = vtc.reshape(bsz, nh, V_DIM, -1)
    vtl = vtl.reshape(bsz, nh, V_DIM, -1)
    return _cd_main(qt, kc, kl, vtc, vtl, xcs.reshape(bsz, 2 * seq, FNET_WIDTH), a_cos, a_sin, b_cos, b_sin,
                    gf, gd, x1, mod_cd, cd_fnet_w[0].astype(BF16), cd_w_out[0].astype(BF16),
                    cd_ln_g[0].reshape(1, -1), cd_ln_b[0].reshape(1, -1), tq=tq)
```

```python
import functools

import numpy as np
import jax
import jax.numpy as jnp
from jax import lax
from jax.experimental import pallas as pl
from jax.experimental.pallas import tpu as pltpu

D_MODEL = 1024
GRID_W = 64
EPS = 1e-6
DEPTH = 2
POOL_WINDOWS = (2, 4, 8, 16)
GROUP_DIM = 128
POOL_WIDTH = 512
SGU_HEADS = 4
SGU_WIDTH = 512
CHUNK = 128
FNET_HEADS = 4
FNET_WIDTH = 512
MLA_HEADS = 8
QK_NOPE = 64
QK_ROPE = 32
V_DIM = 64
Q_LORA = 256
KV_LORA = 128
MLA_WIDTH = 512
ROPE_AXIS = QK_ROPE // 2
ROPE_BASE = 10000.0
ALPHA = (2 * DEPTH) ** 0.25
Q_SCALE = float((QK_NOPE + QK_ROPE) ** -0.5 * np.log2(np.e))

LANES = 128
HALO = 16
HEAD_PAD = LANES
OUT_SLAB = 256
VMEM_LIMIT = 56 * 1024 * 1024

F32 = jnp.float32
BF16 = jnp.bfloat16


def _silu(v):
    half = 0.5 * v
    return half * (1.0 + jnp.tanh(half))


def _dot(a, b):
    return jnp.dot(a, b, preferred_element_type=F32)


def _deep_norm_rows(x, y, gate_scaled, g, b):
    r = x + gate_scaled * y
    mu = jnp.mean(r, axis=-1, keepdims=True)
    rc = r - mu
    var = jnp.mean(rc * rc, axis=-1, keepdims=True)
    return rc * lax.rsqrt(var + EPS / (ALPHA * ALPHA)) * g + b


def _rms_rows(v, g):
    return v * lax.rsqrt(jnp.mean(v * v, axis=-1, keepdims=True) + EPS) * g


def _mod_kernel(c_ref, w_ref, b_ref, o_ref):
    s = _silu(c_ref[...]).astype(BF16)
    o_ref[...] = _dot(s, w_ref[...].astype(BF16)) + b_ref[...]


def _modulation(cond, w_mod, b_mod):
    rows, d = cond.shape
    n = w_mod.shape[1]
    tn = 768
    out = pl.pallas_call(
        _mod_kernel,
        grid=(n // tn,),
        in_specs=[pl.BlockSpec((rows, d), lambda j: (0, 0)),
                  pl.BlockSpec((d, tn), lambda j: (0, j)),
                  pl.BlockSpec((1, tn), lambda j: (0, j))],
        out_specs=pl.BlockSpec((rows, tn), lambda j: (0, j)),
        out_shape=jax.ShapeDtypeStruct((rows, n), F32),
        name="modulation",
    )(cond, w_mod, b_mod.reshape(1, n))
    return out.reshape(rows, 1, n)


def _split_mod(mod_ref):
    d = D_MODEL
    return mod_ref[0, :, 0:d], 1.0 + mod_ref[0, :, d:2 * d], mod_ref[0, :, 2 * d:3 * d]


def _ab_kernel(x_ref, xp_ref, xn_ref, mod_ref, win_ref, poolw_ref, pscale_ref, sguw_ref, sgub_ref,
               wout_ref, lng_ref, lnb_ref, o_ref, h_ref, a_ref, mix_ref, *, tl, seq_len):
    t = pl.program_id(1)
    nt = pl.num_programs(1)
    shift, scale1, gate = _split_mod(mod_ref)
    x = x_ref[0]
    h_ref[HALO:HALO + tl, :] = (x * scale1 + shift).astype(BF16)
    h_ref[0:HALO, :] = (xp_ref[0] * scale1 + shift).astype(BF16)
    h_ref[HALO + tl:2 * HALO + tl, :] = (xn_ref[0] * scale1 + shift).astype(BF16)

    a_ext = _dot(h_ref[...], win_ref[:, 0:POOL_WIDTH])
    a_ref[HALO:HALO + tl, :] = a_ext[HALO:HALO + tl]
    a_ref[0:HALO, :] = jnp.where(t > 0, a_ext[0:HALO], 0.0)
    a_ref[HALO + tl:2 * HALO + tl, :] = jnp.where(t < nt - 1, a_ext[HALO + tl:2 * HALO + tl], 0.0)

    base = 2 * POOL_WIDTH
    slabs = [slice(r0, r0 + OUT_SLAB) for r0 in range(0, tl, OUT_SLAB)]
    wide = []
    for rows in slabs:
        h = h_ref[HALO + rows.start:HALO + rows.stop, :]
        v = _dot(h, win_ref[:, base + SGU_WIDTH:base + 2 * SGU_WIDTH])
        u = _dot(h, win_ref[:, base:base + SGU_WIDTH])
        gate_a = _silu(_dot(h, win_ref[:, POOL_WIDTH:2 * POOL_WIDTH]))
        gate_b = _silu(_dot(h, win_ref[:, base + 2 * SGU_WIDTH:base + 3 * SGU_WIDTH]))
        wide.append((v, u, gate_a, gate_b))

    for rows, (v, u, gate_a, gate_b) in zip(slabs, wide):
        n = rows.stop - rows.start
        pos = t * tl + rows.start + lax.broadcasted_iota(jnp.int32, (SUBLANES, LANES), 0)
        for g, w in enumerate(POOL_WINDOWS):
            cols = slice(g * GROUP_DIM, (g + 1) * GROUP_DIM)
            first = HALO + rows.start - w // 2
            s = a_ref[first:first + n, cols]
            for j in range(1, w):
                s = s + a_ref[first + j:first + j + n, cols]
            edges = []
            for r0 in (0, n - SUBLANES):
                p = pos + r0
                count = jnp.minimum(p + (w - w // 2), seq_len) - jnp.maximum(p - w // 2, 0)
                edges.append(s[r0:r0 + SUBLANES] / count.astype(F32))
            pooled = jnp.concatenate([edges[0], s[SUBLANES:n - SUBLANES] * (1.0 / w), edges[1]], axis=0)
            diff = pooled - a_ref[HALO + rows.start:HALO + rows.stop, cols]
            ya = _dot(diff.astype(BF16), poolw_ref[g]) * pscale_ref[:, cols] * gate_a[:, cols]
            mix_ref[rows, cols] = ya.astype(BF16)

        for hd in range(SGU_HEADS):
            cols = slice(hd * GROUP_DIM, (hd + 1) * GROUP_DIM)
            vh = v[:, cols]
            mu = jnp.mean(vh, axis=-1, keepdims=True)
            vc = vh - mu
            var = jnp.mean(vc * vc, axis=-1, keepdims=True)
            vn = (vc * lax.rsqrt(var + EPS)).astype(BF16)
            f = jnp.concatenate(
                [_dot(sguw_ref[hd], vn[ci * CHUNK:(ci + 1) * CHUNK, :]) + sgub_ref[hd]
                 for ci in range(n // CHUNK)], axis=0)
            yb = u[:, cols] * f * gate_b[:, cols]
            mix_ref[rows, POOL_WIDTH + hd * GROUP_DIM:POOL_WIDTH + (hd + 1) * GROUP_DIM] = yb.astype(BF16)

        y = _dot(mix_ref[rows, :], wout_ref[...])
        o_ref[0, rows, :] = _deep_norm_rows(x_ref[0, rows, :], y, gate * (1.0 / ALPHA), lng_ref[...], lnb_ref[...])


def _ab_layer(x, mod, mod_row, w_in, pool_w, pool_scale, sgu_w, sgu_b, w_out, ln_g, ln_b, *, tl):
    bsz, seq, d = x.shape
    nt = seq // tl
    hb = tl // HALO
    last_hb = seq // HALO - 1
    const2 = lambda b, t: (0, 0)
    const3 = lambda b, t: (0, 0, 0)
    kern = functools.partial(_ab_kernel, tl=tl, seq_len=seq)
    return pl.pallas_call(
        kern,
        grid=(bsz, nt),
        in_specs=[
            pl.BlockSpec((1, tl, d), lambda b, t: (b, t, 0)),
            pl.BlockSpec((1, HALO, d), lambda b, t: (b, jnp.maximum(t * hb - 1, 0), 0)),
            pl.BlockSpec((1, HALO, d), lambda b, t: (b, jnp.minimum((t + 1) * hb, last_hb), 0)),
            pl.BlockSpec((1, 1, 3 * d), lambda b, t: (mod_row(b), 0, 0)),
            pl.BlockSpec(w_in.shape, const2),
            pl.BlockSpec(pool_w.shape, const3),
            pl.BlockSpec(pool_scale.shape, const2),
            pl.BlockSpec(sgu_w.shape, const3),
            pl.BlockSpec(sgu_b.shape, const3),
            pl.BlockSpec(w_out.shape, const2),
            pl.BlockSpec(ln_g.shape, const2),
            pl.BlockSpec(ln_b.shape, const2),
        ],
        out_specs=pl.BlockSpec((1, tl, d), lambda b, t: (b, t, 0)),
        out_shape=jax.ShapeDtypeStruct(x.shape, F32),
        scratch_shapes=[pltpu.VMEM((tl + 2 * HALO, d), BF16),
                        pltpu.VMEM((tl + 2 * HALO, POOL_WIDTH), F32),
                        pltpu.VMEM((tl, POOL_WIDTH + SGU_WIDTH), BF16)],
        compiler_params=pltpu.CompilerParams(dimension_semantics=("parallel", "arbitrary"),
                                             vmem_limit_bytes=VMEM_LIMIT),
        name="ab_layer",
    )(x, x, x, mod, w_in, pool_w, pool_scale, sgu_w, sgu_b, w_out, ln_g, ln_b)


def _dot_nt(a, b):
    return lax.dot_general(a, b, (((1,), (1,)), ((), ())), preferred_element_type=F32)


def _store_kv(ckv, kr_lanes, kvnorm_ref, wk_ref, wvt_ref, k_ref, vt_ref):
    ckv_n = _rms_rows(ckv, kvnorm_ref[...]).astype(BF16)
    hk = jnp.concatenate([ckv_n, kr_lanes.astype(BF16)], axis=-1)
    kk = _dot(hk, wk_ref[...]).astype(BF16)
    for hd in range(MLA_HEADS):
        k_ref[0, hd] = kk[:, hd * HEAD_PAD:(hd + 1) * HEAD_PAD]
    vt_ref[0] = _dot_nt(wvt_ref[...], ckv_n).astype(BF16)


def _cd_proj_ctx_kernel(x_ref, mod_ref, w_ref, kvnorm_ref, wk_ref, wvt_ref, k_ref, vt_ref):
    shift, scale1, _ = _split_mod(mod_ref)
    h = (x_ref[0] * scale1 + shift).astype(BF16)
    z = _dot(h, w_ref[...])
    _store_kv(z[:, 0:KV_LORA], z[:, KV_LORA:KV_LORA + LANES], kvnorm_ref, wk_ref, wvt_ref, k_ref, vt_ref)


def _cd_proj_kernel(x_ref, mod_ref, w_ref, cs_ref, qnorm_ref, wqt_ref, kvnorm_ref, wk_ref, wvt_ref,
                    cosqt_ref, sinqt_ref, cosk_ref, sink_ref,
                    xcs_ref, gf_ref, gd_ref, qt_ref, k_ref, vt_ref):
    shift, scale1, _ = _split_mod(mod_ref)
    h = (x_ref[0] * scale1 + shift).astype(BF16)
    fw = FNET_WIDTH
    o = 2 * fw + MLA_WIDTH
    small = _dot(h, w_ref[:, o:o + Q_LORA + KV_LORA + LANES])
    cq = small[:, 0:Q_LORA]
    ckv = small[:, Q_LORA:Q_LORA + KV_LORA]
    kr = small[:, Q_LORA + KV_LORA:Q_LORA + KV_LORA + LANES]
    f_in = _dot(h, w_ref[:, 0:fw]).astype(BF16)
    gf_ref[0] = _silu(_dot(h, w_ref[:, fw:2 * fw])).astype(BF16)
    gd_ref[0] = _silu(_dot(h, w_ref[:, 2 * fw:2 * fw + MLA_WIDTH])).astype(BF16)

    qn = _rms_rows(cq, qnorm_ref[...]).astype(BF16)
    q2t = _dot_nt(wqt_ref[...], qn)
    head_dim = QK_NOPE + QK_ROPE
    swap0 = MLA_HEADS * head_dim
    cosqt, sinqt = cosqt_ref[...], sinqt_ref[...]
    tl = q2t.shape[1]
    pad = jnp.zeros((HEAD_PAD - head_dim, tl), F32)
    for hd in range(MLA_HEADS):
        nope = q2t[hd * head_dim:hd * head_dim + QK_NOPE, :]
        rope = q2t[hd * head_dim + QK_NOPE:(hd + 1) * head_dim, :]
        rope_sw = q2t[swap0 + hd * QK_ROPE:swap0 + (hd + 1) * QK_ROPE, :]
        qt_ref[0, hd] = jnp.concatenate(
            [nope * Q_SCALE, rope * cosqt + rope_sw * sinqt, pad], axis=0).astype(BF16)

    for hd in range(FNET_HEADS):
        cols = slice(hd * GROUP_DIM, (hd + 1) * GROUP_DIM)
        xcs = _dot(f_in[:, cols], cs_ref[...]).astype(BF16)
        xcs_ref[0, 0, :, cols] = xcs[:, 0:GROUP_DIM]
        xcs_ref[0, 1, :, cols] = xcs[:, GROUP_DIM:2 * GROUP_DIM]

    kr_rot = kr * cosk_ref[...] + pltpu.roll(kr, LANES // 2, 1) * sink_ref[...]
    _store_kv(ckv, kr_rot, kvnorm_ref, wk_ref, wvt_ref, k_ref, vt_ref)


def _const_spec(a):
    nd = a.ndim
    return pl.BlockSpec(a.shape, lambda b, t: (0,) * nd)


def _cd_proj_ctx(ctx, mod, mod_row, w_kv, kv_norm, wk, wvt):
    bsz, seq, d = ctx.shape
    hp = HEAD_PAD
    return pl.pallas_call(
        _cd_proj_ctx_kernel,
        grid=(bsz, 1),
        in_specs=[pl.BlockSpec((1, seq, d), lambda b, t: (b, 0, 0)),
                  pl.BlockSpec((1, 1, 3 * d), lambda b, t: (mod_row(b), 0, 0)),
                  _const_spec(w_kv), _const_spec(kv_norm), _const_spec(wk), _const_spec(wvt)],
        out_specs=[pl.BlockSpec((1, MLA_HEADS, seq, hp), lambda b, t: (b, 0, 0, 0)),
                   pl.BlockSpec((1, MLA_WIDTH, seq), lambda b, t: (b, 0, 0))],
        out_shape=[jax.ShapeDtypeStruct((bsz, MLA_HEADS, seq, hp), BF16),
                   jax.ShapeDtypeStruct((bsz, MLA_WIDTH, seq), BF16)],
        compiler_params=pltpu.CompilerParams(dimension_semantics=("parallel", "arbitrary"),
                                             vmem_limit_bytes=VMEM_LIMIT),
        name="cd_proj_ctx",
    )(ctx, mod, w_kv, kv_norm, wk, wvt)


def _cd_proj(x, mod, w1, cs, q_norm, wqt, kv_norm, wk, wvt, cosqt, sinqt, cosk, sink, *, tl):
    bsz, seq, d = x.shape
    hp = HEAD_PAD
    fw = FNET_WIDTH
    tab = pl.BlockSpec((tl, LANES), lambda b, t: (t, 0))
    tab_t = pl.BlockSpec((QK_ROPE, tl), lambda b, t: (0, t))
    return pl.pallas_call(
        _cd_proj_kernel,
        grid=(bsz, seq // tl),
        in_specs=[pl.BlockSpec((1, tl, d), lambda b, t: (b, t, 0)),
                  pl.BlockSpec((1, 1, 3 * d), lambda b, t: (b, 0, 0)),
                  _const_spec(w1), _const_spec(cs), _const_spec(q_norm), _const_spec(wqt),
                  _const_spec(kv_norm), _const_spec(wk), _const_spec(wvt), tab_t, tab_t, tab, tab],
        out_specs=[pl.BlockSpec((1, 2, tl, fw), lambda b, t: (b, 0, t, 0)),
                   pl.BlockSpec((1, tl, fw), lambda b, t: (b, t, 0)),
                   pl.BlockSpec((1, tl, MLA_WIDTH), lambda b, t: (b, t, 0)),
                   pl.BlockSpec((1, MLA_HEADS, hp, tl), lambda b, t: (b, 0, 0, t)),
                   pl.BlockSpec((1, MLA_HEADS, tl, hp), lambda b, t: (b, 0, t, 0)),
                   pl.BlockSpec((1, MLA_WIDTH, tl), lambda b, t: (b, 0, t))],
        out_shape=[jax.ShapeDtypeStruct((bsz, 2, seq, fw), BF16),
                   jax.ShapeDtypeStruct((bsz, seq, fw), BF16),
                   jax.ShapeDtypeStruct((bsz, seq, MLA_WIDTH), BF16),
                   jax.ShapeDtypeStruct((bsz, MLA_HEADS, hp, seq), BF16),
                   jax.ShapeDtypeStruct((bsz, MLA_HEADS, seq, hp), BF16),
                   jax.ShapeDtypeStruct((bsz, MLA_WIDTH, seq), BF16)],
        compiler_params=pltpu.CompilerParams(dimension_semantics=("parallel", "arbitrary"),
                                             vmem_limit_bytes=VMEM_LIMIT),
        name="cd_proj",
    )(x, mod, w1, cs, q_norm, wqt, kv_norm, wk, wvt, cosqt, sinqt, cosk, sink)


DFT_FINE = 32
FOLD_BLOCK = 256


KEY_CHUNK = 256
SUBLANES = 8
BF16_SUBLANES = 16


def _fold_rows(s, op):
    while s.shape[0] > SUBLANES:
        half = s.shape[0] // 2
        s = op(s[:half], s[half:])
    return s


def _fold_sequence(xcs_ref, mirror_ref, fold_ref, seq_len):
    half = seq_len // 2
    blk = FOLD_BLOCK
    n_blk = seq_len // blk
    for part, sign in ((0, 1.0), (1, -1.0)):
        base = part * seq_len
        for b in range(half // blk):
            low = xcs_ref[0, base + b * blk:base + (b + 1) * blk, :].astype(F32)
            if b == 0:
                upper = xcs_ref[0, base + (n_blk - 1) * blk:base + n_blk * blk, :]
                mirrored = _dot(mirror_ref[:, 0:blk], upper)
            else:
                upper = xcs_ref[0, base + (n_blk - 1 - b) * blk:base + (n_blk + 1 - b) * blk, :]
                mirrored = _dot(mirror_ref[...], upper)
            fold_ref[part * half + b * blk:part * half + (b + 1) * blk, :] = (low + sign * mirrored).astype(BF16)
    fold_ref[half:half + BF16_SUBLANES, :] = jnp.where(
        lax.broadcasted_iota(jnp.int32, (BF16_SUBLANES, FNET_WIDTH), 0) == 0,
        xcs_ref[0, half:half + BF16_SUBLANES, :], fold_ref[half:half + BF16_SUBLANES, :])


def _cd_main_kernel(qt_ref, qtn_ref, kc_ref, kl_ref, vtc_ref, vtl_ref, xcs_ref, mirror_ref, ac_ref, as_ref,
                    bc_ref, bs_ref, gf_ref, gd_ref, x_ref, mod_ref, fw_ref, wout_ref, lng_ref, lnb_ref,
                    o_ref, cls_ref, fold_ref, attnt_ref, s0_ref, s1_ref, m_ref, *, tq, seq_len):
    lc = kc_ref.shape[2]
    n_chunks = (lc + seq_len) // KEY_CHUNK
    ctx_chunks = lc // KEY_CHUNK
    half = seq_len // 2

    def chunk_of(ctx_ref, lat_ref, hd, c, lanes):
        ref, c = (ctx_ref, c) if c < ctx_chunks else (lat_ref, c - ctx_chunks)
        span = slice(c * KEY_CHUNK, (c + 1) * KEY_CHUNK)
        return ref[0, hd, :, span] if lanes else ref[0, hd, span, :]

    def score_chunk(s_ref, q_ref, hd, c, m8):
        s = _dot(chunk_of(kc_ref, kl_ref, hd, c, False), q_ref[0, hd])
        s_ref[c * KEY_CHUNK:(c + 1) * KEY_CHUNK, :] = s
        part = _fold_rows(s, jnp.maximum)
        return part if m8 is None else jnp.maximum(m8, part)

    ones_rows = jnp.ones((BF16_SUBLANES, KEY_CHUNK), BF16)

    def prob_chunk(s_ref, hd, c, m, acc):
        p = jnp.exp2(s_ref[c * KEY_CHUNK:(c + 1) * KEY_CHUNK, :] - m)
        vt_ext = jnp.concatenate([chunk_of(vtc_ref, vtl_ref, hd, c, True), ones_rows], axis=0)
        o_t = _dot(vt_ext, p.astype(BF16))
        return o_t if acc is None else acc + o_t

    def head_step(s_ref, hd, m8, q_ref, q_head):
        m = jnp.max(m8, axis=0, keepdims=True)
        m8, acc = None, None
        for c in range(n_chunks):
            acc = prob_chunk(s_ref, hd, c, m, acc)
            m8 = score_chunk(s_ref, q_ref, q_head, c, m8)
        attnt_ref[hd] = acc[0:V_DIM] * (1.0 / acc[V_DIM:V_DIM + 1])
        return m8

    @pl.when(pl.program_id(1) == 0)
    def _():
        _fold_sequence(xcs_ref, mirror_ref, fold_ref, seq_len)
        m8_even, m8_odd = None, None
        for c in range(n_chunks):
            m8_even = score_chunk(s0_ref, qt_ref, 0, c, m8_even)
            m8_odd = score_chunk(s1_ref, qt_ref, 1, c, m8_odd)
        m_ref[0] = m8_even
        m_ref[1] = m8_odd

    bc, bs = bc_ref[...], bs_ref[...]
    for i in range(tq // DFT_FINE):
        ac, asn = ac_ref[i:i + 1, :], as_ref[i:i + 1, :]
        rows = slice(i * DFT_FINE, (i + 1) * DFT_FINE)
        cls_ref[rows, 0:half] = (ac * bc - asn * bs).astype(BF16)
        cls_ref[rows, half:2 * half] = (asn * bc + ac * bs).astype(BF16)
    spec = _dot(cls_ref[...], fold_ref[...])
    yc = _dot(spec.astype(BF16), fw_ref[...]) * gf_ref[0].astype(F32)

    def head_pair(j, carry):
        return (head_step(s0_ref, 2 * j, carry[0], qt_ref, 2 * j + 2),
                head_step(s1_ref, 2 * j + 1, carry[1], qt_ref, 2 * j + 3))

    m8_even, m8_odd = lax.fori_loop(0, MLA_HEADS // 2 - 1, head_pair, (m_ref[0], m_ref[1]))
    m_ref[0] = head_step(s0_ref, MLA_HEADS - 2, m8_even, qtn_ref, 0)
    m_ref[1] = head_step(s1_ref, MLA_HEADS - 1, m8_odd, qtn_ref, 1)

    yd = attnt_ref[...].reshape(MLA_WIDTH, tq).T * gd_ref[0].astype(F32)
    y = _dot(jnp.concatenate([yc.astype(BF16), yd.astype(BF16)], axis=-1), wout_ref[...])
    _, _, gate = _split_mod(mod_ref)
    o_ref[0] = _deep_norm_rows(x_ref[0], y, gate * (1.0 / ALPHA), lng_ref[...], lnb_ref[...])


def _cd_main(qt, kc, kl, vtc, vtl, xcs, mirror, a_cos, a_sin, b_cos, b_sin, gf, gd, x, mod, fnet_w, w_out,
             ln_g, ln_b, *, tq):
    bsz, seq, d = x.shape
    lc = kc.shape[2]
    hp = HEAD_PAD
    nh = MLA_HEADS
    per_b4 = lambda b, t: (b, 0, 0, 0)
    per_b3 = lambda b, t: (b, 0, 0)
    kern = functools.partial(_cd_main_kernel, tq=tq, seq_len=seq)
    coarse = tq // DFT_FINE
    last_t = seq // tq - 1
    return pl.pallas_call(
        kern,
        grid=(bsz, seq // tq),
        in_specs=[pl.BlockSpec((1, nh, hp, tq), lambda b, t: (b, 0, 0, t)),
                  pl.BlockSpec((1, nh, hp, tq), lambda b, t: (b, 0, 0, jnp.minimum(t + 1, last_t))),
                  pl.BlockSpec((1, nh, lc, hp), per_b4),
                  pl.BlockSpec((1, nh, seq, hp), per_b4),
                  pl.BlockSpec((1, nh, V_DIM, lc), per_b4),
                  pl.BlockSpec((1, nh, V_DIM, seq), per_b4),
                  pl.BlockSpec((1, 2 * seq, FNET_WIDTH), per_b3),
                  _const_spec(mirror),
                  pl.BlockSpec((coarse, seq // 2), lambda b, t: (t, 0)),
                  pl.BlockSpec((coarse, seq // 2), lambda b, t: (t, 0)),
                  _const_spec(b_cos), _const_spec(b_sin),
                  pl.BlockSpec((1, tq, FNET_WIDTH), lambda b, t: (b, t, 0)),
                  pl.BlockSpec((1, tq, MLA_WIDTH), lambda b, t: (b, t, 0)),
                  pl.BlockSpec((1, tq, d), lambda b, t: (b, t, 0)),
                  pl.BlockSpec((1, 1, 3 * d), lambda b, t: (b, 0, 0)),
                  _const_spec(fnet_w), _const_spec(w_out), _const_spec(ln_g), _const_spec(ln_b)],
        out_specs=pl.BlockSpec((1, tq, d), lambda b, t: (b, t, 0)),
        out_shape=jax.ShapeDtypeStruct(x.shape, F32),
        scratch_shapes=[pltpu.VMEM((tq, seq), BF16),
                        pltpu.VMEM((seq, FNET_WIDTH), BF16),
                        pltpu.VMEM((nh, V_DIM, tq), F32),
                        pltpu.VMEM((lc + seq, tq), F32),
                        pltpu.VMEM((lc + seq, tq), F32),
                        pltpu.VMEM((2, SUBLANES, tq), F32)],
        compiler_params=pltpu.CompilerParams(dimension_semantics=("parallel", "arbitrary"),
                                             vmem_limit_bytes=VMEM_LIMIT),
        name="cd_main",
    )(qt, qt, kc, kl, vtc, vtl, xcs, mirror, a_cos, a_sin, b_cos, b_sin, gf, gd, x, mod, fnet_w, w_out, ln_g, ln_b)


def _rope_tables(seq):
    pos = np.arange(seq)
    row = (pos // GRID_W).astype(np.float64)
    col = (pos % GRID_W).astype(np.float64)
    half = ROPE_AXIS // 2
    inv = (ROPE_BASE ** (-np.arange(0, ROPE_AXIS, 2, dtype=np.float32) / ROPE_AXIS)).astype(np.float32)
    cos = np.ones((seq, LANES), np.float32)
    sin = np.zeros((seq, LANES), np.float32)
    for j in range(QK_ROPE):
        coord = row if j < ROPE_AXIS else col
        ang = (coord.astype(np.float32) * inv[j % half]).astype(np.float32)
        cos[:, QK_NOPE + j] = np.cos(ang)
        sign = -1.0 if (j % ROPE_AXIS) < half else 1.0
        sin[:, QK_NOPE + j] = sign * np.sin(ang)
    return cos, sin


def _rope_partner(j):
    half = ROPE_AXIS // 2
    return j + half if (j % ROPE_AXIS) < half else j - half


def _dft_tables(seq):
    l = np.arange(seq // 2, dtype=np.float64)
    k1 = np.arange(seq // DFT_FINE, dtype=np.float64)[:, None]
    k0 = np.arange(DFT_FINE, dtype=np.float64)[:, None]
    ang_a = 2.0 * np.pi * ((k1 * DFT_FINE * l) % seq) / seq
    ang_b = 2.0 * np.pi * ((k0 * l) % seq) / seq
    norm = 1.0 / np.sqrt(seq * GROUP_DIM)
    b_sin = np.sin(ang_b) * norm
    b_sin[:, 0] = np.where(np.arange(DFT_FINE) % 2 == 0, norm, -norm)
    return (np.cos(ang_a).astype(np.float32), np.sin(ang_a).astype(np.float32),
            (np.cos(ang_b) * norm).astype(np.float32), b_sin.astype(np.float32))


def _mirror_matrix():
    m = np.zeros((FOLD_BLOCK, 2 * FOLD_BLOCK), np.float32)
    i = np.arange(1, FOLD_BLOCK)
    m[i, FOLD_BLOCK - i] = 1.0
    m[0, FOLD_BLOCK] = 1.0
    return m


def _channel_dft():
    c = np.arange(GROUP_DIM, dtype=np.float64)
    ang = 2.0 * np.pi * ((c[:, None] * c[None, :]) % GROUP_DIM) / GROUP_DIM
    return np.concatenate([np.cos(ang), -np.sin(ang)], axis=1).astype(np.float32)


def kernel(x, c, ctx, c_ctx, ab_w_mod, ab_b_mod, ab_w_in, ab_pool_w, ab_pool_scale, ab_sgu_w, ab_sgu_b,
           ab_w_out, ab_ln_g, ab_ln_b, cd_w_mod, cd_b_mod, cd_w_in, cd_fnet_w, cd_q_norm, cd_kv_norm,
           cd_w_q_up, cd_w_kv_up, cd_w_out, cd_ln_g, cd_ln_b):
    bsz, seq, d = x.shape
    nh, hp = MLA_HEADS, HEAD_PAD

    rows = -(-(bsz + 1) // 8) * 8
    cond = jnp.zeros((rows, d), F32).at[:bsz].set(c).at[bsz].set(c_ctx)
    mod_ab = _modulation(cond, ab_w_mod[0], ab_b_mod[0])
    mod_cd = _modulation(cond, cd_w_mod[0], cd_b_mod[0])
    lat_row = lambda b: b
    ctx_row = lambda b: bsz

    ab_args = (ab_w_in[0].astype(BF16), ab_pool_w[0].astype(BF16), ab_pool_scale[0].reshape(1, -1),
               ab_sgu_w[0].astype(BF16),
               jnp.broadcast_to(ab_sgu_b[0][:, :, None], (SGU_HEADS, CHUNK, GROUP_DIM)),
               ab_w_out[0].astype(BF16), ab_ln_g[0].reshape(1, -1), ab_ln_b[0].reshape(1, -1))
    x1 = _ab_layer(x, mod_ab, lat_row, *ab_args, tl=512)
    ctx1 = _ab_layer(ctx, mod_ab, ctx_row, *ab_args, tl=ctx.shape[1])

    w = cd_w_in[0]
    o_cq = 2 * FNET_WIDTH
    o_ckv = o_cq + Q_LORA
    o_kr = o_ckv + KV_LORA
    o_dg = o_kr + QK_ROPE
    partner = np.array([_rope_partner(j) for j in range(QK_ROPE)])
    zeros = lambda n: jnp.zeros((d, n), F32)
    kr_cols = w[:, o_kr:o_dg]
    kr_lanes = jnp.concatenate([zeros(QK_NOPE), kr_cols, zeros(LANES - QK_NOPE - QK_ROPE)], axis=1)
    kr_both = jnp.concatenate([kr_cols[:, partner], zeros(QK_NOPE - QK_ROPE), kr_cols,
                               zeros(LANES - QK_NOPE - QK_ROPE)], axis=1)
    w1 = jnp.concatenate([w[:, 0:o_cq], w[:, o_dg:], w[:, o_cq:o_kr], kr_both], axis=1).astype(BF16)
    w_kv_ctx = jnp.concatenate([w[:, o_ckv:o_kr], kr_lanes], axis=1).astype(BF16)

    wq = cd_w_q_up[0].reshape(Q_LORA, nh, QK_NOPE + QK_ROPE)
    wqt = jnp.concatenate([wq.reshape(Q_LORA, -1), wq[:, :, QK_NOPE + partner].reshape(Q_LORA, -1)],
                          axis=1).T.astype(BF16)

    wkv = cd_w_kv_up[0].reshape(KV_LORA, nh, QK_NOPE + V_DIM)
    wk_pad = jnp.concatenate([wkv[:, :, :QK_NOPE], jnp.zeros((KV_LORA, nh, hp - QK_NOPE), F32)], axis=-1)
    lane = np.arange(hp)
    sel = np.diag(((lane >= QK_NOPE) & (lane < QK_NOPE + QK_ROPE)).astype(np.float32))
    wk = jnp.concatenate([wk_pad.reshape(KV_LORA, nh * hp),
                          jnp.asarray(np.tile(sel, (1, nh)))], axis=0).astype(BF16)
    wvt = wkv[:, :, QK_NOPE:].reshape(KV_LORA, nh * V_DIM).T.astype(BF16)

    q_norm = cd_q_norm[0].reshape(1, -1)
    kv_norm = cd_kv_norm[0].reshape(1, -1)
    cos, sin = _rope_tables(seq)
    rope_lanes = slice(QK_NOPE, QK_NOPE + QK_ROPE)
    cosq_t = (cos[:, rope_lanes] * np.float32(Q_SCALE)).T.copy()
    sinq_t = (sin[:, rope_lanes] * np.float32(Q_SCALE)).T.copy()
    cs = jnp.asarray(_channel_dft()).astype(BF16)

    kc, vtc = _cd_proj_ctx(ctx1, mod_cd, ctx_row, w_kv_ctx, kv_norm, wk, wvt)
    xcs, gf, gd, qt, kl, vtl = _cd_proj(x1, mod_cd, w1, cs, q_norm, wqt, kv_norm, wk, wvt,
                                        jnp.asarray(cosq_t), jnp.asarray(sinq_t),
                                        jnp.asarray(cos), jnp.asarray(sin), tl=512)
    tq = 256
    a_cos, a_sin, b_cos, b_sin = (jnp.asarray(t) for t in _dft_tables(seq))
    mirror = jnp.asarray(_mirror_matrix()).astype(BF16)
    vtc = vtc.reshape(bsz, nh, V_DIM, -1)
    vtl = vtl.reshape(bsz, nh, V_DIM, -1)
    return _cd_main(qt, kc, kl, vtc, vtl, xcs.reshape(bsz, 2 * seq, FNET_WIDTH), mirror,
                    a_cos, a_sin, b_cos, b_sin,
                    gf, gd, x1, mod_cd, cd_fnet_w[0].astype(BF16), cd_w_out[0].astype(BF16),
                    cd_ln_g[0].reshape(1, -1), cd_ln_b[0].reshape(1, -1), tq=tq)
```

```python
import functools

import numpy as np
import jax
import jax.numpy as jnp
from jax import lax
from jax.experimental import pallas as pl
from jax.experimental.pallas import tpu as pltpu

D_MODEL = 1024
GRID_W = 64
EPS = 1e-6
DEPTH = 2
POOL_WINDOWS = (2, 4, 8, 16)
GROUP_DIM = 128
POOL_WIDTH = 512
SGU_HEADS = 4
SGU_WIDTH = 512
CHUNK = 128
FNET_HEADS = 4
FNET_WIDTH = 512
MLA_HEADS = 8
QK_NOPE = 64
QK_ROPE = 32
V_DIM = 64
Q_LORA = 256
KV_LORA = 128
MLA_WIDTH = 512
ROPE_AXIS = QK_ROPE // 2
ROPE_BASE = 10000.0
ALPHA = (2 * DEPTH) ** 0.25
Q_SCALE = float((QK_NOPE + QK_ROPE) ** -0.5 * np.log2(np.e))

LANES = 128
HALO = 16
HEAD_PAD = LANES
OUT_SLAB = 256
VMEM_LIMIT = 56 * 1024 * 1024

F32 = jnp.float32
BF16 = jnp.bfloat16


def _silu(v):
    half = 0.5 * v
    return half * (1.0 + jnp.tanh(half))


def _dot(a, b):
    return jnp.dot(a, b, preferred_element_type=F32)


def _deep_norm_rows(x, y, gate_scaled, g, b):
    r = x + gate_scaled * y
    mu = jnp.mean(r, axis=-1, keepdims=True)
    rc = r - mu
    var = jnp.mean(rc * rc, axis=-1, keepdims=True)
    return rc * lax.rsqrt(var + EPS / (ALPHA * ALPHA)) * g + b


def _rms_rows(v, g):
    return v * lax.rsqrt(jnp.mean(v * v, axis=-1, keepdims=True) + EPS) * g


def _mod_kernel(c_ref, w_ref, b_ref, o_ref):
    s = _silu(c_ref[...]).astype(BF16)
    o_ref[...] = _dot(s, w_ref[...].astype(BF16)) + b_ref[...]


def _modulation(cond, w_mod, b_mod):
    rows, d = cond.shape
    n = w_mod.shape[1]
    tn = 768
    out = pl.pallas_call(
        _mod_kernel,
        grid=(n // tn,),
        in_specs=[pl.BlockSpec((rows, d), lambda j: (0, 0)),
                  pl.BlockSpec((d, tn), lambda j: (0, j)),
                  pl.BlockSpec((1, tn), lambda j: (0, j))],
        out_specs=pl.BlockSpec((rows, tn), lambda j: (0, j)),
        out_shape=jax.ShapeDtypeStruct((rows, n), F32),
        name="modulation",
    )(cond, w_mod, b_mod.reshape(1, n))
    return out.reshape(rows, 1, n)


def _split_mod(mod_ref):
    d = D_MODEL
    return mod_ref[0, :, 0:d], 1.0 + mod_ref[0, :, d:2 * d], mod_ref[0, :, 2 * d:3 * d]


def _ab_kernel(x_ref, xp_ref, xn_ref, mod_ref, win_ref, poolw_ref, pscale_ref, sguw_ref, sgub_ref,
               wout_ref, lng_ref, lnb_ref, o_ref, h_ref, a_ref, mix_ref, *, tl, seq_len):
    t = pl.program_id(1)
    nt = pl.num_programs(1)
    shift, scale1, gate = _split_mod(mod_ref)
    x = x_ref[0]
    h_ref[HALO:HALO + tl, :] = (x * scale1 + shift).astype(BF16)
    h_ref[0:HALO, :] = (xp_ref[0] * scale1 + shift).astype(BF16)
    h_ref[HALO + tl:2 * HALO + tl, :] = (xn_ref[0] * scale1 + shift).astype(BF16)

    a_ext = _dot(h_ref[...], win_ref[:, 0:POOL_WIDTH])
    a_ref[HALO:HALO + tl, :] = a_ext[HALO:HALO + tl]
    a_ref[0:HALO, :] = jnp.where(t > 0, a_ext[0:HALO], 0.0)
    a_ref[HALO + tl:2 * HALO + tl, :] = jnp.where(t < nt - 1, a_ext[HALO + tl:2 * HALO + tl], 0.0)

    base = 2 * POOL_WIDTH
    slabs = [slice(r0, r0 + OUT_SLAB) for r0 in range(0, tl, OUT_SLAB)]

    def wide_projections(rows):
        h = h_ref[HALO + rows.start:HALO + rows.stop, :]
        v = _dot(h, win_ref[:, base + SGU_WIDTH:base + 2 * SGU_WIDTH])
        u = _dot(h, win_ref[:, base:base + SGU_WIDTH])
        gate_a = _silu(_dot(h, win_ref[:, POOL_WIDTH:2 * POOL_WIDTH]))
        gate_b = _silu(_dot(h, win_ref[:, base + 2 * SGU_WIDTH:base + 3 * SGU_WIDTH]))
        return v, u, gate_a, gate_b

    def mix_and_project(rows, v, u, gate_a, gate_b):
        n = rows.stop - rows.start
        pos = t * tl + rows.start + lax.broadcasted_iota(jnp.int32, (SUBLANES, LANES), 0)
        for g, w in enumerate(POOL_WINDOWS):
            cols = slice(g * GROUP_DIM, (g + 1) * GROUP_DIM)
            first = HALO + rows.start - w // 2
            s = a_ref[first:first + n, cols]
            for j in range(1, w):
                s = s + a_ref[first + j:first + j + n, cols]
            edges = []
            for r0 in (0, n - SUBLANES):
                p = pos + r0
                count = jnp.minimum(p + (w - w // 2), seq_len) - jnp.maximum(p - w // 2, 0)
                edges.append(s[r0:r0 + SUBLANES] / count.astype(F32))
            pooled = jnp.concatenate([edges[0], s[SUBLANES:n - SUBLANES] * (1.0 / w), edges[1]], axis=0)
            diff = pooled - a_ref[HALO + rows.start:HALO + rows.stop, cols]
            ya = _dot(diff.astype(BF16), poolw_ref[g]) * pscale_ref[:, cols] * gate_a[:, cols]
            mix_ref[rows, cols] = ya.astype(BF16)

        for hd in range(SGU_HEADS):
            cols = slice(hd * GROUP_DIM, (hd + 1) * GROUP_DIM)
            vh = v[:, cols]
            mu = jnp.mean(vh, axis=-1, keepdims=True)
            vc = vh - mu
            var = jnp.mean(vc * vc, axis=-1, keepdims=True)
            vn = (vc * lax.rsqrt(var + EPS)).astype(BF16)
            f = jnp.concatenate(
                [_dot(sguw_ref[hd], vn[ci * CHUNK:(ci + 1) * CHUNK, :]) + sgub_ref[hd]
                 for ci in range(n // CHUNK)], axis=0)
            yb = u[:, cols] * f * gate_b[:, cols]
            mix_ref[rows, POOL_WIDTH + hd * GROUP_DIM:POOL_WIDTH + (hd + 1) * GROUP_DIM] = yb.astype(BF16)

        y = _dot(mix_ref[rows, :], wout_ref[...])
        o_ref[0, rows, :] = _deep_norm_rows(x_ref[0, rows, :], y, gate * (1.0 / ALPHA), lng_ref[...], lnb_ref[...])

    pending = None
    for rows in slabs:
        wide = wide_projections(rows)
        if pending is not None:
            mix_and_project(*pending)
        pending = (rows,) + wide
    mix_and_project(*pending)


def _ab_layer(x, mod, mod_row, w_in, pool_w, pool_scale, sgu_w, sgu_b, w_out, ln_g, ln_b, *, tl):
    bsz, seq, d = x.shape
    nt = seq // tl
    hb = tl // HALO
    last_hb = seq // HALO - 1
    const2 = lambda b, t: (0, 0)
    const3 = lambda b, t: (0, 0, 0)
    kern = functools.partial(_ab_kernel, tl=tl, seq_len=seq)
    return pl.pallas_call(
        kern,
        grid=(bsz, nt),
        in_specs=[
            pl.BlockSpec((1, tl, d), lambda b, t: (b, t, 0)),
            pl.BlockSpec((1, HALO, d), lambda b, t: (b, jnp.maximum(t * hb - 1, 0), 0)),
            pl.BlockSpec((1, HALO, d), lambda b, t: (b, jnp.minimum((t + 1) * hb, last_hb), 0)),
            pl.BlockSpec((1, 1, 3 * d), lambda b, t: (mod_row(b), 0, 0)),
            pl.BlockSpec(w_in.shape, const2),
            pl.BlockSpec(pool_w.shape, const3),
            pl.BlockSpec(pool_scale.shape, const2),
            pl.BlockSpec(sgu_w.shape, const3),
            pl.BlockSpec(sgu_b.shape, const3),
            pl.BlockSpec(w_out.shape, const2),
            pl.BlockSpec(ln_g.shape, const2),
            pl.BlockSpec(ln_b.shape, const2),
        ],
        out_specs=pl.BlockSpec((1, tl, d), lambda b, t: (b, t, 0)),
        out_shape=jax.ShapeDtypeStruct(x.shape, F32),
        scratch_shapes=[pltpu.VMEM((tl + 2 * HALO, d), BF16),
                        pltpu.VMEM((tl + 2 * HALO, POOL_WIDTH), F32),
                        pltpu.VMEM((tl, POOL_WIDTH + SGU_WIDTH), BF16)],
        compiler_params=pltpu.CompilerParams(dimension_semantics=("parallel", "arbitrary"),
                                             vmem_limit_bytes=VMEM_LIMIT),
        name="ab_layer",
    )(x, x, x, mod, w_in, pool_w, pool_scale, sgu_w, sgu_b, w_out, ln_g, ln_b)


def _dot_nt(a, b):
    return lax.dot_general(a, b, (((1,), (1,)), ((), ())), preferred_element_type=F32)


def _store_kv(ckv, kr_lanes, kvnorm_ref, wk_ref, wvt_ref, k_ref, vt_ref):
    ckv_n = _rms_rows(ckv, kvnorm_ref[...]).astype(BF16)
    hk = jnp.concatenate([ckv_n, kr_lanes.astype(BF16)], axis=-1)
    kk = _dot(hk, wk_ref[...]).astype(BF16)
    for hd in range(MLA_HEADS):
        k_ref[0, hd] = kk[:, hd * HEAD_PAD:(hd + 1) * HEAD_PAD]
    vt_ref[0] = _dot_nt(wvt_ref[...], ckv_n).astype(BF16)


def _cd_proj_ctx_kernel(x_ref, mod_ref, w_ref, kvnorm_ref, wk_ref, wvt_ref, k_ref, vt_ref):
    shift, scale1, _ = _split_mod(mod_ref)
    h = (x_ref[0] * scale1 + shift).astype(BF16)
    z = _dot(h, w_ref[...])
    _store_kv(z[:, 0:KV_LORA], z[:, KV_LORA:KV_LORA + LANES], kvnorm_ref, wk_ref, wvt_ref, k_ref, vt_ref)


def _cd_proj_kernel(x_ref, mod_ref, w_ref, cs_ref, qnorm_ref, wqt_ref, kvnorm_ref, wk_ref, wvt_ref,
                    cosqt_ref, sinqt_ref, cosk_ref, sink_ref,
                    xcs_ref, gf_ref, gd_ref, qt_ref, k_ref, vt_ref):
    shift, scale1, _ = _split_mod(mod_ref)
    h = (x_ref[0] * scale1 + shift).astype(BF16)
    fw = FNET_WIDTH
    o = 2 * fw + MLA_WIDTH
    small = _dot(h, w_ref[:, o:o + Q_LORA + KV_LORA + LANES])
    cq = small[:, 0:Q_LORA]
    ckv = small[:, Q_LORA:Q_LORA + KV_LORA]
    kr = small[:, Q_LORA + KV_LORA:Q_LORA + KV_LORA + LANES]
    f_in = _dot(h, w_ref[:, 0:fw]).astype(BF16)
    gf_ref[0] = _silu(_dot(h, w_ref[:, fw:2 * fw])).astype(BF16)
    gd_ref[0] = _silu(_dot(h, w_ref[:, 2 * fw:2 * fw + MLA_WIDTH])).astype(BF16)

    qn = _rms_rows(cq, qnorm_ref[...]).astype(BF16)
    q2t = _dot_nt(wqt_ref[...], qn)
    head_dim = QK_NOPE + QK_ROPE
    swap0 = MLA_HEADS * head_dim
    cosqt, sinqt = cosqt_ref[...], sinqt_ref[...]
    tl = q2t.shape[1]
    pad = jnp.zeros((HEAD_PAD - head_dim, tl), F32)
    for hd in range(MLA_HEADS):
        nope = q2t[hd * head_dim:hd * head_dim + QK_NOPE, :]
        rope = q2t[hd * head_dim + QK_NOPE:(hd + 1) * head_dim, :]
        rope_sw = q2t[swap0 + hd * QK_ROPE:swap0 + (hd + 1) * QK_ROPE, :]
        qt_ref[0, hd] = jnp.concatenate(
            [nope * Q_SCALE, rope * cosqt + rope_sw * sinqt, pad], axis=0).astype(BF16)

    for hd in range(FNET_HEADS):
        cols = slice(hd * GROUP_DIM, (hd + 1) * GROUP_DIM)
        xcs = _dot(f_in[:, cols], cs_ref[...]).astype(BF16)
        xcs_ref[0, 0, :, cols] = xcs[:, 0:GROUP_DIM]
        xcs_ref[0, 1, :, cols] = xcs[:, GROUP_DIM:2 * GROUP_DIM]

    kr_rot = kr * cosk_ref[...] + pltpu.roll(kr, LANES // 2, 1) * sink_ref[...]
    _store_kv(ckv, kr_rot, kvnorm_ref, wk_ref, wvt_ref, k_ref, vt_ref)


def _const_spec(a):
    nd = a.ndim
    return pl.BlockSpec(a.shape, lambda b, t: (0,) * nd)


def _cd_proj_ctx(ctx, mod, mod_row, w_kv, kv_norm, wk, wvt):
    bsz, seq, d = ctx.shape
    hp = HEAD_PAD
    return pl.pallas_call(
        _cd_proj_ctx_kernel,
        grid=(bsz, 1),
        in_specs=[pl.BlockSpec((1, seq, d), lambda b, t: (b, 0, 0)),
                  pl.BlockSpec((1, 1, 3 * d), lambda b, t: (mod_row(b), 0, 0)),
                  _const_spec(w_kv), _const_spec(kv_norm), _const_spec(wk), _const_spec(wvt)],
        out_specs=[pl.BlockSpec((1, MLA_HEADS, seq, hp), lambda b, t: (b, 0, 0, 0)),
                   pl.BlockSpec((1, MLA_WIDTH, seq), lambda b, t: (b, 0, 0))],
        out_shape=[jax.ShapeDtypeStruct((bsz, MLA_HEADS, seq, hp), BF16),
                   jax.ShapeDtypeStruct((bsz, MLA_WIDTH, seq), BF16)],
        compiler_params=pltpu.CompilerParams(dimension_semantics=("parallel", "arbitrary"),
                                             vmem_limit_bytes=VMEM_LIMIT),
        name="cd_proj_ctx",
    )(ctx, mod, w_kv, kv_norm, wk, wvt)


def _cd_proj(x, mod, w1, cs, q_norm, wqt, kv_norm, wk, wvt, cosqt, sinqt, cosk, sink, *, tl):
    bsz, seq, d = x.shape
    hp = HEAD_PAD
    fw = FNET_WIDTH
    tab = pl.BlockSpec((tl, LANES), lambda b, t: (t, 0))
    tab_t = pl.BlockSpec((QK_ROPE, tl), lambda b, t: (0, t))
    return pl.pallas_call(
        _cd_proj_kernel,
        grid=(bsz, seq // tl),
        in_specs=[pl.BlockSpec((1, tl, d), lambda b, t: (b, t, 0)),
                  pl.BlockSpec((1, 1, 3 * d), lambda b, t: (b, 0, 0)),
                  _const_spec(w1), _const_spec(cs), _const_spec(q_norm), _const_spec(wqt),
                  _const_spec(kv_norm), _const_spec(wk), _const_spec(wvt), tab_t, tab_t, tab, tab],
        out_specs=[pl.BlockSpec((1, 2, tl, fw), lambda b, t: (b, 0, t, 0)),
                   pl.BlockSpec((1, tl, fw), lambda b, t: (b, t, 0)),
                   pl.BlockSpec((1, tl, MLA_WIDTH), lambda b, t: (b, t, 0)),
                   pl.BlockSpec((1, MLA_HEADS, hp, tl), lambda b, t: (b, 0, 0, t)),
                   pl.BlockSpec((1, MLA_HEADS, tl, hp), lambda b, t: (b, 0, t, 0)),
                   pl.BlockSpec((1, MLA_WIDTH, tl), lambda b, t: (b, 0, t))],
        out_shape=[jax.ShapeDtypeStruct((bsz, 2, seq, fw), BF16),
                   jax.ShapeDtypeStruct((bsz, seq, fw), BF16),
                   jax.ShapeDtypeStruct((bsz, seq, MLA_WIDTH), BF16),
                   jax.ShapeDtypeStruct((bsz, MLA_HEADS, hp, seq), BF16),
                   jax.ShapeDtypeStruct((bsz, MLA_HEADS, seq, hp), BF16),
                   jax.ShapeDtypeStruct((bsz, MLA_WIDTH, seq), BF16)],
        compiler_params=pltpu.CompilerParams(dimension_semantics=("parallel", "arbitrary"),
                                             vmem_limit_bytes=VMEM_LIMIT),
        name="cd_proj",
    )(x, mod, w1, cs, q_norm, wqt, kv_norm, wk, wvt, cosqt, sinqt, cosk, sink)


DFT_FINE = 32
FOLD_BLOCK = 256


KEY_CHUNK = 256
SUBLANES = 8
BF16_SUBLANES = 16
AHEAD = 4


def _fold_rows(s, op):
    while s.shape[0] > SUBLANES:
        half = s.shape[0] // 2
        s = op(s[:half], s[half:])
    return s


def _fold_sequence(xcs_ref, mirror_ref, fold_ref, seq_len):
    half = seq_len // 2
    blk = FOLD_BLOCK
    n_blk = seq_len // blk
    for part, sign in ((0, 1.0), (1, -1.0)):
        base = part * seq_len
        for b in range(half // blk):
            low = xcs_ref[0, base + b * blk:base + (b + 1) * blk, :].astype(F32)
            if b == 0:
                upper = xcs_ref[0, base + (n_blk - 1) * blk:base + n_blk * blk, :]
                mirrored = _dot(mirror_ref[:, 0:blk], upper)
            else:
                upper = xcs_ref[0, base + (n_blk - 1 - b) * blk:base + (n_blk + 1 - b) * blk, :]
                mirrored = _dot(mirror_ref[...], upper)
            fold_ref[part * half + b * blk:part * half + (b + 1) * blk, :] = (low + sign * mirrored).astype(BF16)
    fold_ref[half:half + BF16_SUBLANES, :] = jnp.where(
        lax.broadcasted_iota(jnp.int32, (BF16_SUBLANES, FNET_WIDTH), 0) == 0,
        xcs_ref[0, half:half + BF16_SUBLANES, :], fold_ref[half:half + BF16_SUBLANES, :])


def _cd_main_kernel(qt_ref, qtn_ref, kc_ref, kl_ref, vtc_ref, vtl_ref, xcs_ref, mirror_ref, ac_ref, as_ref,
                    bc_ref, bs_ref, gf_ref, gd_ref, x_ref, mod_ref, fw_ref, wout_ref, lng_ref, lnb_ref,
                    o_ref, cls_ref, fold_ref, attnt_ref, m_ref, *s_refs, tq, seq_len):
    lc = kc_ref.shape[2]
    n_chunks = (lc + seq_len) // KEY_CHUNK
    ctx_chunks = lc // KEY_CHUNK
    half = seq_len // 2

    def chunk_of(ctx_ref, lat_ref, hd, c, lanes):
        ref, c = (ctx_ref, c) if c < ctx_chunks else (lat_ref, c - ctx_chunks)
        span = slice(c * KEY_CHUNK, (c + 1) * KEY_CHUNK)
        return ref[0, hd, :, span] if lanes else ref[0, hd, span, :]

    def score_chunk(s_ref, q_ref, hd, c, m8):
        s = _dot(chunk_of(kc_ref, kl_ref, hd, c, False), q_ref[0, hd])
        s_ref[c * KEY_CHUNK:(c + 1) * KEY_CHUNK, :] = s
        part = _fold_rows(s, jnp.maximum)
        return part if m8 is None else jnp.maximum(m8, part)

    ones_rows = jnp.ones((BF16_SUBLANES, KEY_CHUNK), BF16)

    def prob_chunk(s, hd, c, m, acc):
        p = jnp.exp2(s - m)
        vt_ext = jnp.concatenate([chunk_of(vtc_ref, vtl_ref, hd, c, True), ones_rows], axis=0)
        o_t = _dot(vt_ext, p.astype(BF16))
        return o_t if acc is None else acc + o_t

    def head_step(s_ref, hd, m8, q_ref, q_head):
        m = jnp.max(m8, axis=0, keepdims=True)
        m8, acc = None, None
        for c in range(n_chunks):
            s_old = s_ref[c * KEY_CHUNK:(c + 1) * KEY_CHUNK, :]
            m8 = score_chunk(s_ref, q_ref, q_head, c, m8)
            acc = prob_chunk(s_old, hd, c, m, acc)
        attnt_ref[hd] = acc[0:V_DIM] * (1.0 / acc[V_DIM:V_DIM + 1])
        return m8

    @pl.when(pl.program_id(1) == 0)
    def _():
        _fold_sequence(xcs_ref, mirror_ref, fold_ref, seq_len)
        m8s = [None] * AHEAD
        for c in range(n_chunks):
            for hd in range(AHEAD):
                m8s[hd] = score_chunk(s_refs[hd], qt_ref, hd, c, m8s[hd])
        for hd in range(AHEAD):
            m_ref[hd] = m8s[hd]

    bc, bs = bc_ref[...], bs_ref[...]
    for i in range(tq // DFT_FINE):
        ac, asn = ac_ref[i:i + 1, :], as_ref[i:i + 1, :]
        rows = slice(i * DFT_FINE, (i + 1) * DFT_FINE)
        cls_ref[rows, 0:half] = (ac * bc - asn * bs).astype(BF16)
        cls_ref[rows, half:2 * half] = (asn * bc + ac * bs).astype(BF16)
    spec = _dot(cls_ref[...], fold_ref[...])
    yc = _dot(spec.astype(BF16), fw_ref[...]) * gf_ref[0].astype(F32)

    m8s = [m_ref[hd] for hd in range(AHEAD)]
    for hd in range(MLA_HEADS):
        slot = hd % AHEAD
        ahead = hd + AHEAD
        q_src, q_head = (qt_ref, ahead) if ahead < MLA_HEADS else (qtn_ref, ahead - MLA_HEADS)
        m8s[slot] = head_step(s_refs[slot], hd, m8s[slot], q_src, q_head)
    for hd in range(AHEAD):
        m_ref[hd] = m8s[hd]

    yd = attnt_ref[...].reshape(MLA_WIDTH, tq).T * gd_ref[0].astype(F32)
    y = _dot(jnp.concatenate([yc.astype(BF16), yd.astype(BF16)], axis=-1), wout_ref[...])
    _, _, gate = _split_mod(mod_ref)
    o_ref[0] = _deep_norm_rows(x_ref[0], y, gate * (1.0 / ALPHA), lng_ref[...], lnb_ref[...])


def _cd_main(qt, kc, kl, vtc, vtl, xcs, mirror, a_cos, a_sin, b_cos, b_sin, gf, gd, x, mod, fnet_w, w_out,
             ln_g, ln_b, *, tq):
    bsz, seq, d = x.shape
    lc = kc.shape[2]
    hp = HEAD_PAD
    nh = MLA_HEADS
    per_b4 = lambda b, t: (b, 0, 0, 0)
    per_b3 = lambda b, t: (b, 0, 0)
    kern = functools.partial(_cd_main_kernel, tq=tq, seq_len=seq)
    coarse = tq // DFT_FINE
    last_t = seq // tq - 1
    return pl.pallas_call(
        kern,
        grid=(bsz, seq // tq),
        in_specs=[pl.BlockSpec((1, nh, hp, tq), lambda b, t: (b, 0, 0, t)),
                  pl.BlockSpec((1, nh, hp, tq), lambda b, t: (b, 0, 0, jnp.minimum(t + 1, last_t))),
                  pl.BlockSpec((1, nh, lc, hp), per_b4),
                  pl.BlockSpec((1, nh, seq, hp), per_b4),
                  pl.BlockSpec((1, nh, V_DIM, lc), per_b4),
                  pl.BlockSpec((1, nh, V_DIM, seq), per_b4),
                  pl.BlockSpec((1, 2 * seq, FNET_WIDTH), per_b3),
                  _const_spec(mirror),
                  pl.BlockSpec((coarse, seq // 2), lambda b, t: (t, 0)),
                  pl.BlockSpec((coarse, seq // 2), lambda b, t: (t, 0)),
                  _const_spec(b_cos), _const_spec(b_sin),
                  pl.BlockSpec((1, tq, FNET_WIDTH), lambda b, t: (b, t, 0)),
                  pl.BlockSpec((1, tq, MLA_WIDTH), lambda b, t: (b, t, 0)),
                  pl.BlockSpec((1, tq, d), lambda b, t: (b, t, 0)),
                  pl.BlockSpec((1, 1, 3 * d), lambda b, t: (b, 0, 0)),
                  _const_spec(fnet_w), _const_spec(w_out), _const_spec(ln_g), _const_spec(ln_b)],
        out_specs=pl.BlockSpec((1, tq, d), lambda b, t: (b, t, 0)),
        out_shape=jax.ShapeDtypeStruct(x.shape, F32),
        scratch_shapes=[pltpu.VMEM((tq, seq), BF16),
                        pltpu.VMEM((seq, FNET_WIDTH), BF16),
                        pltpu.VMEM((nh, V_DIM, tq), F32),
                        pltpu.VMEM((AHEAD, SUBLANES, tq), F32)]
                       + [pltpu.VMEM((lc + seq, tq), F32)] * AHEAD,
        compiler_params=pltpu.CompilerParams(dimension_semantics=("parallel", "arbitrary"),
                                             vmem_limit_bytes=VMEM_LIMIT),
        name="cd_main",
    )(qt, qt, kc, kl, vtc, vtl, xcs, mirror, a_cos, a_sin, b_cos, b_sin, gf, gd, x, mod, fnet_w, w_out, ln_g, ln_b)


def _rope_tables(seq):
    pos = np.arange(seq)
    row = (pos // GRID_W).astype(np.float64)
    col = (pos % GRID_W).astype(np.float64)
    half = ROPE_AXIS // 2
    inv = (ROPE_BASE ** (-np.arange(0, ROPE_AXIS, 2, dtype=np.float32) / ROPE_AXIS)).astype(np.float32)
    cos = np.ones((seq, LANES), np.float32)
    sin = np.zeros((seq, LANES), np.float32)
    for j in range(QK_ROPE):
        coord = row if j < ROPE_AXIS else col
        ang = (coord.astype(np.float32) * inv[j % half]).astype(np.float32)
        cos[:, QK_NOPE + j] = np.cos(ang)
        sign = -1.0 if (j % ROPE_AXIS) < half else 1.0
        sin[:, QK_NOPE + j] = sign * np.sin(ang)
    return cos, sin


def _rope_partner(j):
    half = ROPE_AXIS // 2
    return j + half if (j % ROPE_AXIS) < half else j - half


def _dft_tables(seq):
    l = np.arange(seq // 2, dtype=np.float64)
    k1 = np.arange(seq // DFT_FINE, dtype=np.float64)[:, None]
    k0 = np.arange(DFT_FINE, dtype=np.float64)[:, None]
    ang_a = 2.0 * np.pi * ((k1 * DFT_FINE * l) % seq) / seq
    ang_b = 2.0 * np.pi * ((k0 * l) % seq) / seq
    norm = 1.0 / np.sqrt(seq * GROUP_DIM)
    b_sin = np.sin(ang_b) * norm
    b_sin[:, 0] = np.where(np.arange(DFT_FINE) % 2 == 0, norm, -norm)
    return (np.cos(ang_a).astype(np.float32), np.sin(ang_a).astype(np.float32),
            (np.cos(ang_b) * norm).astype(np.float32), b_sin.astype(np.float32))


def _mirror_matrix():
    m = np.zeros((FOLD_BLOCK, 2 * FOLD_BLOCK), np.float32)
    i = np.arange(1, FOLD_BLOCK)
    m[i, FOLD_BLOCK - i] = 1.0
    m[0, FOLD_BLOCK] = 1.0
    return m


def _channel_dft():
    c = np.arange(GROUP_DIM, dtype=np.float64)
    ang = 2.0 * np.pi * ((c[:, None] * c[None, :]) % GROUP_DIM) / GROUP_DIM
    return np.concatenate([np.cos(ang), -np.sin(ang)], axis=1).astype(np.float32)


def kernel(x, c, ctx, c_ctx, ab_w_mod, ab_b_mod, ab_w_in, ab_pool_w, ab_pool_scale, ab_sgu_w, ab_sgu_b,
           ab_w_out, ab_ln_g, ab_ln_b, cd_w_mod, cd_b_mod, cd_w_in, cd_fnet_w, cd_q_norm, cd_kv_norm,
           cd_w_q_up, cd_w_kv_up, cd_w_out, cd_ln_g, cd_ln_b):
    bsz, seq, d = x.shape
    nh, hp = MLA_HEADS, HEAD_PAD

    rows = -(-(bsz + 1) // 8) * 8
    cond = jnp.zeros((rows, d), F32).at[:bsz].set(c).at[bsz].set(c_ctx)
    mod_ab = _modulation(cond, ab_w_mod[0], ab_b_mod[0])
    mod_cd = _modulation(cond, cd_w_mod[0], cd_b_mod[0])
    lat_row = lambda b: b
    ctx_row = lambda b: bsz

    ab_args = (ab_w_in[0].astype(BF16), ab_pool_w[0].astype(BF16), ab_pool_scale[0].reshape(1, -1),
               ab_sgu_w[0].astype(BF16),
               jnp.broadcast_to(ab_sgu_b[0][:, :, None], (SGU_HEADS, CHUNK, GROUP_DIM)),
               ab_w_out[0].astype(BF16), ab_ln_g[0].reshape(1, -1), ab_ln_b[0].reshape(1, -1))
    x1 = _ab_layer(x, mod_ab, lat_row, *ab_args, tl=1024)
    ctx1 = _ab_layer(ctx, mod_ab, ctx_row, *ab_args, tl=ctx.shape[1])

    w = cd_w_in[0]
    o_cq = 2 * FNET_WIDTH
    o_ckv = o_cq + Q_LORA
    o_kr = o_ckv + KV_LORA
    o_dg = o_kr + QK_ROPE
    partner = np.array([_rope_partner(j) for j in range(QK_ROPE)])
    zeros = lambda n: jnp.zeros((d, n), F32)
    kr_cols = w[:, o_kr:o_dg]
    kr_lanes = jnp.concatenate([zeros(QK_NOPE), kr_cols, zeros(LANES - QK_NOPE - QK_ROPE)], axis=1)
    kr_both = jnp.concatenate([kr_cols[:, partner], zeros(QK_NOPE - QK_ROPE), kr_cols,
                               zeros(LANES - QK_NOPE - QK_ROPE)], axis=1)
    w1 = jnp.concatenate([w[:, 0:o_cq], w[:, o_dg:], w[:, o_cq:o_kr], kr_both], axis=1).astype(BF16)
    w_kv_ctx = jnp.concatenate([w[:, o_ckv:o_kr], kr_lanes], axis=1).astype(BF16)

    wq = cd_w_q_up[0].reshape(Q_LORA, nh, QK_NOPE + QK_ROPE)
    wqt = jnp.concatenate([wq.reshape(Q_LORA, -1), wq[:, :, QK_NOPE + partner].reshape(Q_LORA, -1)],
                          axis=1).T.astype(BF16)

    wkv = cd_w_kv_up[0].reshape(KV_LORA, nh, QK_NOPE + V_DIM)
    wk_pad = jnp.concatenate([wkv[:, :, :QK_NOPE], jnp.zeros((KV_LORA, nh, hp - QK_NOPE), F32)], axis=-1)
    lane = np.arange(hp)
    sel = np.diag(((lane >= QK_NOPE) & (lane < QK_NOPE + QK_ROPE)).astype(np.float32))
    wk = jnp.concatenate([wk_pad.reshape(KV_LORA, nh * hp),
                          jnp.asarray(np.tile(sel, (1, nh)))], axis=0).astype(BF16)
    wvt = wkv[:, :, QK_NOPE:].reshape(KV_LORA, nh * V_DIM).T.astype(BF16)

    q_norm = cd_q_norm[0].reshape(1, -1)
    kv_norm = cd_kv_norm[0].reshape(1, -1)
    cos, sin = _rope_tables(seq)
    rope_lanes = slice(QK_NOPE, QK_NOPE + QK_ROPE)
    cosq_t = (cos[:, rope_lanes] * np.float32(Q_SCALE)).T.copy()
    sinq_t = (sin[:, rope_lanes] * np.float32(Q_SCALE)).T.copy()
    cs = jnp.asarray(_channel_dft()).astype(BF16)

    kc, vtc = _cd_proj_ctx(ctx1, mod_cd, ctx_row, w_kv_ctx, kv_norm, wk, wvt)
    xcs, gf, gd, qt, kl, vtl = _cd_proj(x1, mod_cd, w1, cs, q_norm, wqt, kv_norm, wk, wvt,
                                        jnp.asarray(cosq_t), jnp.asarray(sinq_t),
                                        jnp.asarray(cos), jnp.asarray(sin), tl=512)
    tq = 256
    a_cos, a_sin, b_cos, b_sin = (jnp.asarray(t) for t in _dft_tables(seq))
    mirror = jnp.asarray(_mirror_matrix()).astype(BF16)
    vtc = vtc.reshape(bsz, nh, V_DIM, -1)
    vtl = vtl.reshape(bsz, nh, V_DIM, -1)
    return _cd_main(qt, kc, kl, vtc, vtl, xcs.reshape(bsz, 2 * seq, FNET_WIDTH), mirror,
                    a_cos, a_sin, b_cos, b_sin,
                    gf, gd, x1, mod_cd, cd_fnet_w[0].astype(BF16), cd_w_out[0].astype(BF16),
                    cd_ln_g[0].reshape(1, -1), cd_ln_b[0].reshape(1, -1), tq=tq)
```

```python
import functools

import numpy as np
import jax
import jax.numpy as jnp
from jax import lax
from jax.experimental import pallas as pl
from jax.experimental.pallas import tpu as pltpu

D_MODEL = 1024
GRID_W = 64
EPS = 1e-6
DEPTH = 2
POOL_WINDOWS = (2, 4, 8, 16)
GROUP_DIM = 128
POOL_WIDTH = 512
SGU_HEADS = 4
SGU_WIDTH = 512
CHUNK = 128
FNET_HEADS = 4
FNET_WIDTH = 512
MLA_HEADS = 8
QK_NOPE = 64
QK_ROPE = 32
V_DIM = 64
Q_LORA = 256
KV_LORA = 128
MLA_WIDTH = 512
ROPE_AXIS = QK_ROPE // 2
ROPE_BASE = 10000.0
ALPHA = (2 * DEPTH) ** 0.25
Q_SCALE = float((QK_NOPE + QK_ROPE) ** -0.5 * np.log2(np.e))

LANES = 128
HALO = 16
HEAD_PAD = LANES
OUT_SLAB = 256
VMEM_LIMIT = 56 * 1024 * 1024

F32 = jnp.float32
BF16 = jnp.bfloat16


def _silu(v):
    half = 0.5 * v
    return half * (1.0 + jnp.tanh(half))


def _dot(a, b):
    return jnp.dot(a, b, preferred_element_type=F32)


def _deep_norm_rows(x, y, gate_scaled, g, b):
    r = x + gate_scaled * y
    mu = jnp.mean(r, axis=-1, keepdims=True)
    rc = r - mu
    var = jnp.mean(rc * rc, axis=-1, keepdims=True)
    return rc * lax.rsqrt(var + EPS / (ALPHA * ALPHA)) * g + b


def _rms_rows(v, g):
    return v * lax.rsqrt(jnp.mean(v * v, axis=-1, keepdims=True) + EPS) * g


def _mod_kernel(c_ref, w_ref, b_ref, o_ref):
    s = _silu(c_ref[...]).astype(BF16)
    o_ref[...] = _dot(s, w_ref[...].astype(BF16)) + b_ref[...]


def _modulation(cond, w_mod, b_mod):
    rows, d = cond.shape
    n = w_mod.shape[1]
    tn = 768
    out = pl.pallas_call(
        _mod_kernel,
        grid=(n // tn,),
        in_specs=[pl.BlockSpec((rows, d), lambda j: (0, 0)),
                  pl.BlockSpec((d, tn), lambda j: (0, j)),
                  pl.BlockSpec((1, tn), lambda j: (0, j))],
        out_specs=pl.BlockSpec((rows, tn), lambda j: (0, j)),
        out_shape=jax.ShapeDtypeStruct((rows, n), F32),
        name="modulation",
    )(cond, w_mod, b_mod.reshape(1, n))
    return out.reshape(rows, 1, n)


def _split_mod(mod_ref):
    d = D_MODEL
    return mod_ref[0, :, 0:d], 1.0 + mod_ref[0, :, d:2 * d], mod_ref[0, :, 2 * d:3 * d]


def _ab_kernel(x_ref, xp_ref, xn_ref, mod_ref, win_ref, poolw_ref, pscale_ref, sguw_ref, sgub_ref,
               wout_ref, lng_ref, lnb_ref, o_ref, h_ref, a_ref, mix_ref, *, tl, seq_len):
    t = pl.program_id(1)
    nt = pl.num_programs(1)
    shift, scale1, gate = _split_mod(mod_ref)
    x = x_ref[0]
    h_ref[HALO:HALO + tl, :] = (x * scale1 + shift).astype(BF16)
    h_ref[0:HALO, :] = (xp_ref[0] * scale1 + shift).astype(BF16)
    h_ref[HALO + tl:2 * HALO + tl, :] = (xn_ref[0] * scale1 + shift).astype(BF16)

    a_ext = _dot(h_ref[...], win_ref[:, 0:POOL_WIDTH])
    a_ref[HALO:HALO + tl, :] = a_ext[HALO:HALO + tl]
    a_ref[0:HALO, :] = jnp.where(t > 0, a_ext[0:HALO], 0.0)
    a_ref[HALO + tl:2 * HALO + tl, :] = jnp.where(t < nt - 1, a_ext[HALO + tl:2 * HALO + tl], 0.0)

    base = 2 * POOL_WIDTH
    slabs = [slice(r0, r0 + OUT_SLAB) for r0 in range(0, tl, OUT_SLAB)]

    def wide_projections(rows):
        h = h_ref[HALO + rows.start:HALO + rows.stop, :]
        v = _dot(h, win_ref[:, base + SGU_WIDTH:base + 2 * SGU_WIDTH])
        u = _dot(h, win_ref[:, base:base + SGU_WIDTH])
        gate_a = _silu(_dot(h, win_ref[:, POOL_WIDTH:2 * POOL_WIDTH]))
        gate_b = _silu(_dot(h, win_ref[:, base + 2 * SGU_WIDTH:base + 3 * SGU_WIDTH]))
        return v, u, gate_a, gate_b

    def mix_and_project(rows, v, u, gate_a, gate_b):
        n = rows.stop - rows.start
        pos = t * tl + rows.start + lax.broadcasted_iota(jnp.int32, (SUBLANES, LANES), 0)
        for g, w in enumerate(POOL_WINDOWS):
            cols = slice(g * GROUP_DIM, (g + 1) * GROUP_DIM)
            first = HALO + rows.start - w // 2
            s = a_ref[first:first + n, cols]
            for j in range(1, w):
                s = s + a_ref[first + j:first + j + n, cols]
            edges = []
            for r0 in (0, n - SUBLANES):
                p = pos + r0
                count = jnp.minimum(p + (w - w // 2), seq_len) - jnp.maximum(p - w // 2, 0)
                edges.append(s[r0:r0 + SUBLANES] / count.astype(F32))
            pooled = jnp.concatenate([edges[0], s[SUBLANES:n - SUBLANES] * (1.0 / w), edges[1]], axis=0)
            diff = pooled - a_ref[HALO + rows.start:HALO + rows.stop, cols]
            ya = _dot(diff.astype(BF16), poolw_ref[g]) * pscale_ref[:, cols] * gate_a[:, cols]
            mix_ref[rows, cols] = ya.astype(BF16)

        for hd in range(SGU_HEADS):
            cols = slice(hd * GROUP_DIM, (hd + 1) * GROUP_DIM)
            vh = v[:, cols]
            mu = jnp.mean(vh, axis=-1, keepdims=True)
            vc = vh - mu
            var = jnp.mean(vc * vc, axis=-1, keepdims=True)
            vn = (vc * lax.rsqrt(var + EPS)).astype(BF16)
            f = jnp.concatenate(
                [_dot(sguw_ref[hd], vn[ci * CHUNK:(ci + 1) * CHUNK, :]) + sgub_ref[hd]
                 for ci in range(n // CHUNK)], axis=0)
            yb = u[:, cols] * f * gate_b[:, cols]
            mix_ref[rows, POOL_WIDTH + hd * GROUP_DIM:POOL_WIDTH + (hd + 1) * GROUP_DIM] = yb.astype(BF16)

        y = _dot(mix_ref[rows, :], wout_ref[...])
        o_ref[0, rows, :] = _deep_norm_rows(x_ref[0, rows, :], y, gate * (1.0 / ALPHA), lng_ref[...], lnb_ref[...])

    pending = None
    for rows in slabs:
        wide = wide_projections(rows)
        if pending is not None:
            mix_and_project(*pending)
        pending = (rows,) + wide
    mix_and_project(*pending)


def _ab_layer(x, mod, mod_row, w_in, pool_w, pool_scale, sgu_w, sgu_b, w_out, ln_g, ln_b, *, tl):
    bsz, seq, d = x.shape
    nt = seq // tl
    hb = tl // HALO
    last_hb = seq // HALO - 1
    const2 = lambda b, t: (0, 0)
    const3 = lambda b, t: (0, 0, 0)
    kern = functools.partial(_ab_kernel, tl=tl, seq_len=seq)
    return pl.pallas_call(
        kern,
        grid=(bsz, nt),
        in_specs=[
            pl.BlockSpec((1, tl, d), lambda b, t: (b, t, 0)),
            pl.BlockSpec((1, HALO, d), lambda b, t: (b, jnp.maximum(t * hb - 1, 0), 0)),
            pl.BlockSpec((1, HALO, d), lambda b, t: (b, jnp.minimum((t + 1) * hb, last_hb), 0)),
            pl.BlockSpec((1, 1, 3 * d), lambda b, t: (mod_row(b), 0, 0)),
            pl.BlockSpec(w_in.shape, const2),
            pl.BlockSpec(pool_w.shape, const3),
            pl.BlockSpec(pool_scale.shape, const2),
            pl.BlockSpec(sgu_w.shape, const3),
            pl.BlockSpec(sgu_b.shape, const3),
            pl.BlockSpec(w_out.shape, const2),
            pl.BlockSpec(ln_g.shape, const2),
            pl.BlockSpec(ln_b.shape, const2),
        ],
        out_specs=pl.BlockSpec((1, tl, d), lambda b, t: (b, t, 0)),
        out_shape=jax.ShapeDtypeStruct(x.shape, F32),
        scratch_shapes=[pltpu.VMEM((tl + 2 * HALO, d), BF16),
                        pltpu.VMEM((tl + 2 * HALO, POOL_WIDTH), F32),
                        pltpu.VMEM((tl, POOL_WIDTH + SGU_WIDTH), BF16)],
        compiler_params=pltpu.CompilerParams(dimension_semantics=("parallel", "arbitrary"),
                                             vmem_limit_bytes=VMEM_LIMIT),
        name="ab_layer",
    )(x, x, x, mod, w_in, pool_w, pool_scale, sgu_w, sgu_b, w_out, ln_g, ln_b)


def _dot_nt(a, b):
    return lax.dot_general(a, b, (((1,), (1,)), ((), ())), preferred_element_type=F32)


def _store_kv(ckv, kr_lanes, kvnorm_ref, wk_ref, wvt_ref, k_ref, vt_ref):
    ckv_n = _rms_rows(ckv, kvnorm_ref[...]).astype(BF16)
    hk = jnp.concatenate([ckv_n, kr_lanes.astype(BF16)], axis=-1)
    kk = _dot(hk, wk_ref[...]).astype(BF16)
    for hd in range(MLA_HEADS):
        k_ref[0, hd] = kk[:, hd * HEAD_PAD:(hd + 1) * HEAD_PAD]
    vt_ref[0] = _dot_nt(wvt_ref[...], ckv_n).astype(BF16)


def _cd_proj_ctx_kernel(x_ref, mod_ref, w_ref, kvnorm_ref, wk_ref, wvt_ref, k_ref, vt_ref):
    shift, scale1, _ = _split_mod(mod_ref)
    h = (x_ref[0] * scale1 + shift).astype(BF16)
    z = _dot(h, w_ref[...])
    _store_kv(z[:, 0:KV_LORA], z[:, KV_LORA:KV_LORA + LANES], kvnorm_ref, wk_ref, wvt_ref, k_ref, vt_ref)


def _cd_proj_kernel(x_ref, mod_ref, w_ref, cs_ref, qnorm_ref, wqt_ref, kvnorm_ref, wk_ref, wvt_ref,
                    cosqt_ref, sinqt_ref, cosk_ref, sink_ref,
                    xcs_ref, gf_ref, gd_ref, qt_ref, k_ref, vt_ref):
    shift, scale1, _ = _split_mod(mod_ref)
    h = (x_ref[0] * scale1 + shift).astype(BF16)
    fw = FNET_WIDTH
    o = 2 * fw + MLA_WIDTH
    small = _dot(h, w_ref[:, o:o + Q_LORA + KV_LORA + LANES])
    cq = small[:, 0:Q_LORA]
    ckv = small[:, Q_LORA:Q_LORA + KV_LORA]
    kr = small[:, Q_LORA + KV_LORA:Q_LORA + KV_LORA + LANES]
    f_in = _dot(h, w_ref[:, 0:fw]).astype(BF16)
    gf_ref[0] = _silu(_dot(h, w_ref[:, fw:2 * fw])).astype(BF16)
    gd_ref[0] = _silu(_dot(h, w_ref[:, 2 * fw:2 * fw + MLA_WIDTH])).astype(BF16)

    qn = _rms_rows(cq, qnorm_ref[...]).astype(BF16)
    q2t = _dot_nt(wqt_ref[...], qn)
    head_dim = QK_NOPE + QK_ROPE
    swap0 = MLA_HEADS * head_dim
    cosqt, sinqt = cosqt_ref[...], sinqt_ref[...]
    tl = q2t.shape[1]
    pad = jnp.zeros((HEAD_PAD - head_dim, tl), F32)
    for hd in range(MLA_HEADS):
        nope = q2t[hd * head_dim:hd * head_dim + QK_NOPE, :]
        rope = q2t[hd * head_dim + QK_NOPE:(hd + 1) * head_dim, :]
        rope_sw = q2t[swap0 + hd * QK_ROPE:swap0 + (hd + 1) * QK_ROPE, :]
        qt_ref[0, hd] = jnp.concatenate(
            [nope * Q_SCALE, rope * cosqt + rope_sw * sinqt, pad], axis=0).astype(BF16)

    for hd in range(FNET_HEADS):
        cols = slice(hd * GROUP_DIM, (hd + 1) * GROUP_DIM)
        xcs = _dot(f_in[:, cols], cs_ref[...]).astype(BF16)
        xcs_ref[0, 0, :, cols] = xcs[:, 0:GROUP_DIM]
        xcs_ref[0, 1, :, cols] = xcs[:, GROUP_DIM:2 * GROUP_DIM]

    kr_rot = kr * cosk_ref[...] + pltpu.roll(kr, LANES // 2, 1) * sink_ref[...]
    _store_kv(ckv, kr_rot, kvnorm_ref, wk_ref, wvt_ref, k_ref, vt_ref)


def _const_spec(a):
    nd = a.ndim
    return pl.BlockSpec(a.shape, lambda b, t: (0,) * nd)


def _cd_proj_ctx(ctx, mod, mod_row, w_kv, kv_norm, wk, wvt):
    bsz, seq, d = ctx.shape
    hp = HEAD_PAD
    return pl.pallas_call(
        _cd_proj_ctx_kernel,
        grid=(bsz, 1),
        in_specs=[pl.BlockSpec((1, seq, d), lambda b, t: (b, 0, 0)),
                  pl.BlockSpec((1, 1, 3 * d), lambda b, t: (mod_row(b), 0, 0)),
                  _const_spec(w_kv), _const_spec(kv_norm), _const_spec(wk), _const_spec(wvt)],
        out_specs=[pl.BlockSpec((1, MLA_HEADS, seq, hp), lambda b, t: (b, 0, 0, 0)),
                   pl.BlockSpec((1, MLA_WIDTH, seq), lambda b, t: (b, 0, 0))],
        out_shape=[jax.ShapeDtypeStruct((bsz, MLA_HEADS, seq, hp), BF16),
                   jax.ShapeDtypeStruct((bsz, MLA_WIDTH, seq), BF16)],
        compiler_params=pltpu.CompilerParams(dimension_semantics=("parallel", "arbitrary"),
                                             vmem_limit_bytes=VMEM_LIMIT),
        name="cd_proj_ctx",
    )(ctx, mod, w_kv, kv_norm, wk, wvt)


def _cd_proj(x, mod, w1, cs, q_norm, wqt, kv_norm, wk, wvt, cosqt, sinqt, cosk, sink, *, tl):
    bsz, seq, d = x.shape
    hp = HEAD_PAD
    fw = FNET_WIDTH
    tab = pl.BlockSpec((tl, LANES), lambda b, t: (t, 0))
    tab_t = pl.BlockSpec((QK_ROPE, tl), lambda b, t: (0, t))
    return pl.pallas_call(
        _cd_proj_kernel,
        grid=(bsz, seq // tl),
        in_specs=[pl.BlockSpec((1, tl, d), lambda b, t: (b, t, 0)),
                  pl.BlockSpec((1, 1, 3 * d), lambda b, t: (b, 0, 0)),
                  _const_spec(w1), _const_spec(cs), _const_spec(q_norm), _const_spec(wqt),
                  _const_spec(kv_norm), _const_spec(wk), _const_spec(wvt), tab_t, tab_t, tab, tab],
        out_specs=[pl.BlockSpec((1, 2, tl, fw), lambda b, t: (b, 0, t, 0)),
                   pl.BlockSpec((1, tl, fw), lambda b, t: (b, t, 0)),
                   pl.BlockSpec((1, tl, MLA_WIDTH), lambda b, t: (b, t, 0)),
                   pl.BlockSpec((1, MLA_HEADS, hp, tl), lambda b, t: (b, 0, 0, t)),
                   pl.BlockSpec((1, MLA_HEADS, tl, hp), lambda b, t: (b, 0, t, 0)),
                   pl.BlockSpec((1, MLA_WIDTH, tl), lambda b, t: (b, 0, t))],
        out_shape=[jax.ShapeDtypeStruct((bsz, 2, seq, fw), BF16),
                   jax.ShapeDtypeStruct((bsz, seq, fw), BF16),
                   jax.ShapeDtypeStruct((bsz, seq, MLA_WIDTH), BF16),
                   jax.ShapeDtypeStruct((bsz, MLA_HEADS, hp, seq), BF16),
                   jax.ShapeDtypeStruct((bsz, MLA_HEADS, seq, hp), BF16),
                   jax.ShapeDtypeStruct((bsz, MLA_WIDTH, seq), BF16)],
        compiler_params=pltpu.CompilerParams(dimension_semantics=("parallel", "arbitrary"),
                                             vmem_limit_bytes=VMEM_LIMIT),
        name="cd_proj",
    )(x, mod, w1, cs, q_norm, wqt, kv_norm, wk, wvt, cosqt, sinqt, cosk, sink)


DFT_FINE = 32
FOLD_BLOCK = 256
DFT_BLOCK = 256
DFT_AFTER_HEADS = (1, 3)


KEY_CHUNK = 256
SUBLANES = 8
BF16_SUBLANES = 16
AHEAD = 4


def _fold_rows(s, op):
    while s.shape[0] > SUBLANES:
        half = s.shape[0] // 2
        s = op(s[:half], s[half:])
    return s


def _fold_sequence(xcs_ref, mirror_ref, fold_ref, seq_len):
    half = seq_len // 2
    blk = FOLD_BLOCK
    n_blk = seq_len // blk
    for part, sign in ((0, 1.0), (1, -1.0)):
        base = part * seq_len
        for b in range(half // blk):
            low = xcs_ref[0, base + b * blk:base + (b + 1) * blk, :].astype(F32)
            if b == 0:
                upper = xcs_ref[0, base + (n_blk - 1) * blk:base + n_blk * blk, :]
                mirrored = _dot(mirror_ref[:, 0:blk], upper)
            else:
                upper = xcs_ref[0, base + (n_blk - 1 - b) * blk:base + (n_blk + 1 - b) * blk, :]
                mirrored = _dot(mirror_ref[...], upper)
            fold_ref[part * half + b * blk:part * half + (b + 1) * blk, :] = (low + sign * mirrored).astype(BF16)
    fold_ref[half:half + BF16_SUBLANES, :] = jnp.where(
        lax.broadcasted_iota(jnp.int32, (BF16_SUBLANES, FNET_WIDTH), 0) == 0,
        xcs_ref[0, half:half + BF16_SUBLANES, :], fold_ref[half:half + BF16_SUBLANES, :])


def _cd_main_kernel(qt_ref, qtn_ref, kc_ref, kl_ref, vtc_ref, vtl_ref, xcs_ref, mirror_ref, ac_ref, as_ref,
                    bc_ref, bs_ref, gf_ref, gd_ref, x_ref, mod_ref, fw_ref, wout_ref, lng_ref, lnb_ref,
                    o_ref, cls_ref, fold_ref, attnt_ref, m_ref, *s_refs, tq, seq_len):
    lc = kc_ref.shape[2]
    n_chunks = (lc + seq_len) // KEY_CHUNK
    ctx_chunks = lc // KEY_CHUNK
    half = seq_len // 2

    def chunk_of(ctx_ref, lat_ref, hd, c, lanes):
        ref, c = (ctx_ref, c) if c < ctx_chunks else (lat_ref, c - ctx_chunks)
        span = slice(c * KEY_CHUNK, (c + 1) * KEY_CHUNK)
        return ref[0, hd, :, span] if lanes else ref[0, hd, span, :]

    def score_chunk(s_ref, q_ref, hd, c, m8):
        s = _dot(chunk_of(kc_ref, kl_ref, hd, c, False), q_ref[0, hd])
        s_ref[c * KEY_CHUNK:(c + 1) * KEY_CHUNK, :] = s
        part = _fold_rows(s, jnp.maximum)
        return part if m8 is None else jnp.maximum(m8, part)

    ones_rows = jnp.ones((BF16_SUBLANES, KEY_CHUNK), BF16)

    def prob_chunk(s, hd, c, m, acc):
        p = jnp.exp2(s - m)
        vt_ext = jnp.concatenate([chunk_of(vtc_ref, vtl_ref, hd, c, True), ones_rows], axis=0)
        o_t = _dot(vt_ext, p.astype(BF16))
        return o_t if acc is None else acc + o_t

    def head_step(s_ref, hd, m8, q_ref, q_head):
        m = jnp.max(m8, axis=0, keepdims=True)
        m8, acc = None, None
        for c in range(n_chunks):
            s_old = s_ref[c * KEY_CHUNK:(c + 1) * KEY_CHUNK, :]
            m8 = score_chunk(s_ref, q_ref, q_head, c, m8)
            acc = prob_chunk(s_old, hd, c, m, acc)
        attnt_ref[hd] = acc[0:V_DIM] * (1.0 / acc[V_DIM:V_DIM + 1])
        return m8

    @pl.when(pl.program_id(1) == 0)
    def _():
        _fold_sequence(xcs_ref, mirror_ref, fold_ref, seq_len)
        m8s = [None] * AHEAD
        for c in range(n_chunks):
            for hd in range(AHEAD):
                m8s[hd] = score_chunk(s_refs[hd], qt_ref, hd, c, m8s[hd])
        for hd in range(AHEAD):
            m_ref[hd] = m8s[hd]

    bc, bs = bc_ref[...], bs_ref[...]
    for i in range(tq // DFT_FINE):
        ac, asn = ac_ref[i:i + 1, :], as_ref[i:i + 1, :]
        rows = slice(i * DFT_FINE, (i + 1) * DFT_FINE)
        cls_ref[rows, 0:half] = (ac * bc - asn * bs).astype(BF16)
        cls_ref[rows, half:2 * half] = (asn * bc + ac * bs).astype(BF16)

    specs, yc = [], None
    m8s = [m_ref[hd] for hd in range(AHEAD)]
    for hd in range(MLA_HEADS):
        slot = hd % AHEAD
        ahead = hd + AHEAD
        q_src, q_head = (qt_ref, ahead) if ahead < MLA_HEADS else (qtn_ref, ahead - MLA_HEADS)
        m8s[slot] = head_step(s_refs[slot], hd, m8s[slot], q_src, q_head)
        if hd in DFT_AFTER_HEADS:
            j = DFT_AFTER_HEADS.index(hd)
            cols = slice(j * DFT_BLOCK, (j + 1) * DFT_BLOCK)
            specs.append(_dot(cls_ref[...], fold_ref[:, cols]).astype(BF16))
        if hd == DFT_AFTER_HEADS[-1] + 1:
            yc = _dot(jnp.concatenate(specs, axis=-1), fw_ref[...]) * gf_ref[0].astype(F32)
    for hd in range(AHEAD):
        m_ref[hd] = m8s[hd]

    yd = attnt_ref[...].reshape(MLA_WIDTH, tq).T * gd_ref[0].astype(F32)
    y = _dot(jnp.concatenate([yc.astype(BF16), yd.astype(BF16)], axis=-1), wout_ref[...])
    _, _, gate = _split_mod(mod_ref)
    o_ref[0] = _deep_norm_rows(x_ref[0], y, gate * (1.0 / ALPHA), lng_ref[...], lnb_ref[...])


def _cd_main(qt, kc, kl, vtc, vtl, xcs, mirror, a_cos, a_sin, b_cos, b_sin, gf, gd, x, mod, fnet_w, w_out,
             ln_g, ln_b, *, tq):
    bsz, seq, d = x.shape
    lc = kc.shape[2]
    hp = HEAD_PAD
    nh = MLA_HEADS
    per_b4 = lambda b, t: (b, 0, 0, 0)
    per_b3 = lambda b, t: (b, 0, 0)
    kern = functools.partial(_cd_main_kernel, tq=tq, seq_len=seq)
    coarse = tq // DFT_FINE
    last_t = seq // tq - 1
    return pl.pallas_call(
        kern,
        grid=(bsz, seq // tq),
        in_specs=[pl.BlockSpec((1, nh, hp, tq), lambda b, t: (b, 0, 0, t)),
                  pl.BlockSpec((1, nh, hp, tq), lambda b, t: (b, 0, 0, jnp.minimum(t + 1, last_t))),
                  pl.BlockSpec((1, nh, lc, hp), per_b4),
                  pl.BlockSpec((1, nh, seq, hp), per_b4),
                  pl.BlockSpec((1, nh, V_DIM, lc), per_b4),
                  pl.BlockSpec((1, nh, V_DIM, seq), per_b4),
                  pl.BlockSpec((1, 2 * seq, FNET_WIDTH), per_b3),
                  _const_spec(mirror),
                  pl.BlockSpec((coarse, seq // 2), lambda b, t: (t, 0)),
                  pl.BlockSpec((coarse, seq // 2), lambda b, t: (t, 0)),
                  _const_spec(b_cos), _const_spec(b_sin),
                  pl.BlockSpec((1, tq, FNET_WIDTH), lambda b, t: (b, t, 0)),
                  pl.BlockSpec((1, tq, MLA_WIDTH), lambda b, t: (b, t, 0)),
                  pl.BlockSpec((1, tq, d), lambda b, t: (b, t, 0)),
                  pl.BlockSpec((1, 1, 3 * d), lambda b, t: (b, 0, 0)),
                  _const_spec(fnet_w), _const_spec(w_out), _const_spec(ln_g), _const_spec(ln_b)],
        out_specs=pl.BlockSpec((1, tq, d), lambda b, t: (b, t, 0)),
        out_shape=jax.ShapeDtypeStruct(x.shape, F32),
        scratch_shapes=[pltpu.VMEM((tq, seq), BF16),
                        pltpu.VMEM((seq, FNET_WIDTH), BF16),
                        pltpu.VMEM((nh, V_DIM, tq), F32),
                        pltpu.VMEM((AHEAD, SUBLANES, tq), F32)]
                       + [pltpu.VMEM((lc + seq, tq), F32)] * AHEAD,
        compiler_params=pltpu.CompilerParams(dimension_semantics=("parallel", "arbitrary"),
                                             vmem_limit_bytes=VMEM_LIMIT),
        name="cd_main",
    )(qt, qt, kc, kl, vtc, vtl, xcs, mirror, a_cos, a_sin, b_cos, b_sin, gf, gd, x, mod, fnet_w, w_out, ln_g, ln_b)


def _rope_tables(seq):
    pos = np.arange(seq)
    row = (pos // GRID_W).astype(np.float64)
    col = (pos % GRID_W).astype(np.float64)
    half = ROPE_AXIS // 2
    inv = (ROPE_BASE ** (-np.arange(0, ROPE_AXIS, 2, dtype=np.float32) / ROPE_AXIS)).astype(np.float32)
    cos = np.ones((seq, LANES), np.float32)
    sin = np.zeros((seq, LANES), np.float32)
    for j in range(QK_ROPE):
        coord = row if j < ROPE_AXIS else col
        ang = (coord.astype(np.float32) * inv[j % half]).astype(np.float32)
        cos[:, QK_NOPE + j] = np.cos(ang)
        sign = -1.0 if (j % ROPE_AXIS) < half else 1.0
        sin[:, QK_NOPE + j] = sign * np.sin(ang)
    return cos, sin


def _rope_partner(j):
    half = ROPE_AXIS // 2
    return j + half if (j % ROPE_AXIS) < half else j - half


def _dft_tables(seq):
    l = np.arange(seq // 2, dtype=np.float64)
    k1 = np.arange(seq // DFT_FINE, dtype=np.float64)[:, None]
    k0 = np.arange(DFT_FINE, dtype=np.float64)[:, None]
    ang_a = 2.0 * np.pi * ((k1 * DFT_FINE * l) % seq) / seq
    ang_b = 2.0 * np.pi * ((k0 * l) % seq) / seq
    norm = 1.0 / np.sqrt(seq * GROUP_DIM)
    b_sin = np.sin(ang_b) * norm
    b_sin[:, 0] = np.where(np.arange(DFT_FINE) % 2 == 0, norm, -norm)
    return (np.cos(ang_a).astype(np.float32), np.sin(ang_a).astype(np.float32),
            (np.cos(ang_b) * norm).astype(np.float32), b_sin.astype(np.float32))


def _mirror_matrix():
    m = np.zeros((FOLD_BLOCK, 2 * FOLD_BLOCK), np.float32)
    i = np.arange(1, FOLD_BLOCK)
    m[i, FOLD_BLOCK - i] = 1.0
    m[0, FOLD_BLOCK] = 1.0
    return m


def _channel_dft():
    c = np.arange(GROUP_DIM, dtype=np.float64)
    ang = 2.0 * np.pi * ((c[:, None] * c[None, :]) % GROUP_DIM) / GROUP_DIM
    return np.concatenate([np.cos(ang), -np.sin(ang)], axis=1).astype(np.float32)


def kernel(x, c, ctx, c_ctx, ab_w_mod, ab_b_mod, ab_w_in, ab_pool_w, ab_pool_scale, ab_sgu_w, ab_sgu_b,
           ab_w_out, ab_ln_g, ab_ln_b, cd_w_mod, cd_b_mod, cd_w_in, cd_fnet_w, cd_q_norm, cd_kv_norm,
           cd_w_q_up, cd_w_kv_up, cd_w_out, cd_ln_g, cd_ln_b):
    bsz, seq, d = x.shape
    nh, hp = MLA_HEADS, HEAD_PAD

    rows = -(-(bsz + 1) // 8) * 8
    cond = jnp.zeros((rows, d), F32).at[:bsz].set(c).at[bsz].set(c_ctx)
    mod_ab = _modulation(cond, ab_w_mod[0], ab_b_mod[0])
    mod_cd = _modulation(cond, cd_w_mod[0], cd_b_mod[0])
    lat_row = lambda b: b
    ctx_row = lambda b: bsz

    ab_args = (ab_w_in[0].astype(BF16), ab_pool_w[0].astype(BF16), ab_pool_scale[0].reshape(1, -1),
               ab_sgu_w[0].astype(BF16),
               jnp.broadcast_to(ab_sgu_b[0][:, :, None], (SGU_HEADS, CHUNK, GROUP_DIM)),
               ab_w_out[0].astype(BF16), ab_ln_g[0].reshape(1, -1), ab_ln_b[0].reshape(1, -1))
    x1 = _ab_layer(x, mod_ab, lat_row, *ab_args, tl=1024)
    ctx1 = _ab_layer(ctx, mod_ab, ctx_row, *ab_args, tl=ctx.shape[1])

    w = cd_w_in[0]
    o_cq = 2 * FNET_WIDTH
    o_ckv = o_cq + Q_LORA
    o_kr = o_ckv + KV_LORA
    o_dg = o_kr + QK_ROPE
    partner = np.array([_rope_partner(j) for j in range(QK_ROPE)])
    zeros = lambda n: jnp.zeros((d, n), F32)
    kr_cols = w[:, o_kr:o_dg]
    kr_lanes = jnp.concatenate([zeros(QK_NOPE), kr_cols, zeros(LANES - QK_NOPE - QK_ROPE)], axis=1)
    kr_both = jnp.concatenate([kr_cols[:, partner], zeros(QK_NOPE - QK_ROPE), kr_cols,
                               zeros(LANES - QK_NOPE - QK_ROPE)], axis=1)
    w1 = jnp.concatenate([w[:, 0:o_cq], w[:, o_dg:], w[:, o_cq:o_kr], kr_both], axis=1).astype(BF16)
    w_kv_ctx = jnp.concatenate([w[:, o_ckv:o_kr], kr_lanes], axis=1).astype(BF16)

    wq = cd_w_q_up[0].reshape(Q_LORA, nh, QK_NOPE + QK_ROPE)
    wqt = jnp.concatenate([wq.reshape(Q_LORA, -1), wq[:, :, QK_NOPE + partner].reshape(Q_LORA, -1)],
                          axis=1).T.astype(BF16)

    wkv = cd_w_kv_up[0].reshape(KV_LORA, nh, QK_NOPE + V_DIM)
    wk_pad = jnp.concatenate([wkv[:, :, :QK_NOPE], jnp.zeros((KV_LORA, nh, hp - QK_NOPE), F32)], axis=-1)
    lane = np.arange(hp)
    sel = np.diag(((lane >= QK_NOPE) & (lane < QK_NOPE + QK_ROPE)).astype(np.float32))
    wk = jnp.concatenate([wk_pad.reshape(KV_LORA, nh * hp),
                          jnp.asarray(np.tile(sel, (1, nh)))], axis=0).astype(BF16)
    wvt = wkv[:, :, QK_NOPE:].reshape(KV_LORA, nh * V_DIM).T.astype(BF16)

    q_norm = cd_q_norm[0].reshape(1, -1)
    kv_norm = cd_kv_norm[0].reshape(1, -1)
    cos, sin = _rope_tables(seq)
    rope_lanes = slice(QK_NOPE, QK_NOPE + QK_ROPE)
    cosq_t = (cos[:, rope_lanes] * np.float32(Q_SCALE)).T.copy()
    sinq_t = (sin[:, rope_lanes] * np.float32(Q_SCALE)).T.copy()
    cs = jnp.asarray(_channel_dft()).astype(BF16)

    kc, vtc = _cd_proj_ctx(ctx1, mod_cd, ctx_row, w_kv_ctx, kv_norm, wk, wvt)
    xcs, gf, gd, qt, kl, vtl = _cd_proj(x1, mod_cd, w1, cs, q_norm, wqt, kv_norm, wk, wvt,
                                        jnp.asarray(cosq_t), jnp.asarray(sinq_t),
                                        jnp.asarray(cos), jnp.asarray(sin), tl=1024)
    tq = 256
    a_cos, a_sin, b_cos, b_sin = (jnp.asarray(t) for t in _dft_tables(seq))
    mirror = jnp.asarray(_mirror_matrix()).astype(BF16)
    vtc = vtc.reshape(bsz, nh, V_DIM, -1)
    vtl = vtl.reshape(bsz, nh, V_DIM, -1)
    return _cd_main(qt, kc, kl, vtc, vtl, xcs.reshape(bsz, 2 * seq, FNET_WIDTH), mirror,
                    a_cos, a_sin, b_cos, b_sin,
                    gf, gd, x1, mod_cd, cd_fnet_w[0].astype(BF16), cd_w_out[0].astype(BF16),
                    cd_ln_g[0].reshape(1, -1), cd_ln_b[0].reshape(1, -1), tq=tq)
```

```python
import functools

import numpy as np
import jax
import jax.numpy as jnp
from jax import lax
from jax.experimental import pallas as pl
from jax.experimental.pallas import tpu as pltpu

D_MODEL = 1024
GRID_W = 64
EPS = 1e-6
DEPTH = 2
POOL_WINDOWS = (2, 4, 8, 16)
GROUP_DIM = 128
POOL_WIDTH = 512
SGU_HEADS = 4
SGU_WIDTH = 512
CHUNK = 128
FNET_HEADS = 4
FNET_WIDTH = 512
MLA_HEADS = 8
QK_NOPE = 64
QK_ROPE = 32
V_DIM = 64
Q_LORA = 256
KV_LORA = 128
MLA_WIDTH = 512
ROPE_AXIS = QK_ROPE // 2
ROPE_BASE = 10000.0
ALPHA = (2 * DEPTH) ** 0.25
Q_SCALE = float((QK_NOPE + QK_ROPE) ** -0.5 * np.log2(np.e))

LANES = 128
HALO = 16
HEAD_PAD = LANES
OUT_SLAB = 256
VMEM_LIMIT = 56 * 1024 * 1024
ROW_TILE = 1024
QUERY_TILE = 256
MOD_COL_TILE = 768

F32 = jnp.float32
BF16 = jnp.bfloat16


def _silu(v):
    half = 0.5 * v
    return half * (1.0 + jnp.tanh(half))


def _dot(a, b):
    return jnp.dot(a, b, preferred_element_type=F32)


def _deep_norm_rows(x, y, gate_scaled, g, b):
    r = x + gate_scaled * y
    mu = jnp.mean(r, axis=-1, keepdims=True)
    rc = r - mu
    var = jnp.mean(rc * rc, axis=-1, keepdims=True)
    return rc * lax.rsqrt(var + EPS / (ALPHA * ALPHA)) * g + b


def _rms_rows(v, g):
    return v * lax.rsqrt(jnp.mean(v * v, axis=-1, keepdims=True) + EPS) * g


def _mod_kernel(c_ref, w_ref, b_ref, o_ref):
    s = _silu(c_ref[...]).astype(BF16)
    o_ref[...] = _dot(s, w_ref[...].astype(BF16)) + b_ref[...]


def _modulation(cond, w_mod, b_mod):
    rows, d = cond.shape
    n = w_mod.shape[1]
    tn = MOD_COL_TILE
    out = pl.pallas_call(
        _mod_kernel,
        grid=(n // tn,),
        in_specs=[pl.BlockSpec((rows, d), lambda j: (0, 0)),
                  pl.BlockSpec((d, tn), lambda j: (0, j)),
                  pl.BlockSpec((1, tn), lambda j: (0, j))],
        out_specs=pl.BlockSpec((rows, tn), lambda j: (0, j)),
        out_shape=jax.ShapeDtypeStruct((rows, n), F32),
        name="modulation",
    )(cond, w_mod, b_mod.reshape(1, n))
    return out.reshape(rows, 1, n)


def _split_mod(mod_ref):
    d = D_MODEL
    return mod_ref[0, :, 0:d], 1.0 + mod_ref[0, :, d:2 * d], mod_ref[0, :, 2 * d:3 * d]


def _ab_kernel(x_ref, xp_ref, xn_ref, mod_ref, win_ref, poolw_ref, pscale_ref, sguw_ref, sgub_ref,
               wout_ref, lng_ref, lnb_ref, o_ref, h_ref, a_ref, mix_ref, *, tl, seq_len):
    t = pl.program_id(1)
    nt = pl.num_programs(1)
    shift, scale1, gate = _split_mod(mod_ref)
    x = x_ref[0]
    h_ref[HALO:HALO + tl, :] = (x * scale1 + shift).astype(BF16)
    h_ref[0:HALO, :] = (xp_ref[0] * scale1 + shift).astype(BF16)
    h_ref[HALO + tl:2 * HALO + tl, :] = (xn_ref[0] * scale1 + shift).astype(BF16)

    a_ext = _dot(h_ref[...], win_ref[:, 0:POOL_WIDTH])
    a_ref[HALO:HALO + tl, :] = a_ext[HALO:HALO + tl]
    a_ref[0:HALO, :] = jnp.where(t > 0, a_ext[0:HALO], 0.0)
    a_ref[HALO + tl:2 * HALO + tl, :] = jnp.where(t < nt - 1, a_ext[HALO + tl:2 * HALO + tl], 0.0)

    base = 2 * POOL_WIDTH
    slabs = [slice(r0, r0 + OUT_SLAB) for r0 in range(0, tl, OUT_SLAB)]

    def wide_projections(rows):
        h = h_ref[HALO + rows.start:HALO + rows.stop, :]
        v = _dot(h, win_ref[:, base + SGU_WIDTH:base + 2 * SGU_WIDTH])
        u = _dot(h, win_ref[:, base:base + SGU_WIDTH])
        gate_a = _silu(_dot(h, win_ref[:, POOL_WIDTH:2 * POOL_WIDTH]))
        gate_b = _silu(_dot(h, win_ref[:, base + 2 * SGU_WIDTH:base + 3 * SGU_WIDTH]))
        return v, u, gate_a, gate_b

    def mix_and_project(rows, v, u, gate_a, gate_b):
        n = rows.stop - rows.start
        pos = t * tl + rows.start + lax.broadcasted_iota(jnp.int32, (SUBLANES, LANES), 0)
        for g, w in enumerate(POOL_WINDOWS):
            cols = slice(g * GROUP_DIM, (g + 1) * GROUP_DIM)
            first = HALO + rows.start - w // 2
            s = a_ref[first:first + n, cols]
            for j in range(1, w):
                s = s + a_ref[first + j:first + j + n, cols]
            edges = []
            for r0 in (0, n - SUBLANES):
                p = pos + r0
                count = jnp.minimum(p + (w - w // 2), seq_len) - jnp.maximum(p - w // 2, 0)
                edges.append(s[r0:r0 + SUBLANES] / count.astype(F32))
            pooled = jnp.concatenate([edges[0], s[SUBLANES:n - SUBLANES] * (1.0 / w), edges[1]], axis=0)
            diff = pooled - a_ref[HALO + rows.start:HALO + rows.stop, cols]
            ya = _dot(diff.astype(BF16), poolw_ref[g]) * pscale_ref[:, cols] * gate_a[:, cols]
            mix_ref[rows, cols] = ya.astype(BF16)

        for hd in range(SGU_HEADS):
            cols = slice(hd * GROUP_DIM, (hd + 1) * GROUP_DIM)
            vh = v[:, cols]
            mu = jnp.mean(vh, axis=-1, keepdims=True)
            vc = vh - mu
            var = jnp.mean(vc * vc, axis=-1, keepdims=True)
            vn = (vc * lax.rsqrt(var + EPS)).astype(BF16)
            f = jnp.concatenate(
                [_dot(sguw_ref[hd], vn[ci * CHUNK:(ci + 1) * CHUNK, :]) + sgub_ref[hd]
                 for ci in range(n // CHUNK)], axis=0)
            yb = u[:, cols] * f * gate_b[:, cols]
            mix_ref[rows, POOL_WIDTH + hd * GROUP_DIM:POOL_WIDTH + (hd + 1) * GROUP_DIM] = yb.astype(BF16)

        y = _dot(mix_ref[rows, :], wout_ref[...])
        o_ref[0, rows, :] = _deep_norm_rows(x_ref[0, rows, :], y, gate * (1.0 / ALPHA), lng_ref[...], lnb_ref[...])

    pending = None
    for rows in slabs:
        wide = wide_projections(rows)
        if pending is not None:
            mix_and_project(*pending)
        pending = (rows,) + wide
    mix_and_project(*pending)


def _ab_layer(x, mod, mod_row, w_in, pool_w, pool_scale, sgu_w, sgu_b, w_out, ln_g, ln_b, *, tl):
    bsz, seq, d = x.shape
    nt = seq // tl
    hb = tl // HALO
    last_hb = seq // HALO - 1
    const2 = lambda b, t: (0, 0)
    const3 = lambda b, t: (0, 0, 0)
    kern = functools.partial(_ab_kernel, tl=tl, seq_len=seq)
    return pl.pallas_call(
        kern,
        grid=(bsz, nt),
        in_specs=[
            pl.BlockSpec((1, tl, d), lambda b, t: (b, t, 0)),
            pl.BlockSpec((1, HALO, d), lambda b, t: (b, jnp.maximum(t * hb - 1, 0), 0)),
            pl.BlockSpec((1, HALO, d), lambda b, t: (b, jnp.minimum((t + 1) * hb, last_hb), 0)),
            pl.BlockSpec((1, 1, 3 * d), lambda b, t: (mod_row(b), 0, 0)),
            pl.BlockSpec(w_in.shape, const2),
            pl.BlockSpec(pool_w.shape, const3),
            pl.BlockSpec(pool_scale.shape, const2),
            pl.BlockSpec(sgu_w.shape, const3),
            pl.BlockSpec(sgu_b.shape, const3),
            pl.BlockSpec(w_out.shape, const2),
            pl.BlockSpec(ln_g.shape, const2),
            pl.BlockSpec(ln_b.shape, const2),
        ],
        out_specs=pl.BlockSpec((1, tl, d), lambda b, t: (b, t, 0)),
        out_shape=jax.ShapeDtypeStruct(x.shape, F32),
        scratch_shapes=[pltpu.VMEM((tl + 2 * HALO, d), BF16),
                        pltpu.VMEM((tl + 2 * HALO, POOL_WIDTH), F32),
                        pltpu.VMEM((tl, POOL_WIDTH + SGU_WIDTH), BF16)],
        compiler_params=pltpu.CompilerParams(dimension_semantics=("parallel", "arbitrary"),
                                             vmem_limit_bytes=VMEM_LIMIT),
        name="ab_layer",
    )(x, x, x, mod, w_in, pool_w, pool_scale, sgu_w, sgu_b, w_out, ln_g, ln_b)


def _dot_nt(a, b):
    return lax.dot_general(a, b, (((1,), (1,)), ((), ())), preferred_element_type=F32)


def _store_kv(ckv, kr_lanes, kvnorm_ref, wk_ref, wvt_ref, k_ref, vt_ref, e=0):
    ckv_n = _rms_rows(ckv, kvnorm_ref[...]).astype(BF16)
    hk = jnp.concatenate([ckv_n, kr_lanes.astype(BF16)], axis=-1)
    kk = _dot(hk, wk_ref[...]).astype(BF16)
    for hd in range(MLA_HEADS):
        k_ref[e, hd] = kk[:, hd * HEAD_PAD:(hd + 1) * HEAD_PAD]
    vt_ref[e] = _dot_nt(wvt_ref[...], ckv_n).astype(BF16)


def _cd_proj_ctx_kernel(x_ref, mod_ref, w_ref, kvnorm_ref, wk_ref, wvt_ref, k_ref, vt_ref):
    shift, scale1, _ = _split_mod(mod_ref)
    for e in range(x_ref.shape[0]):
        h = (x_ref[e] * scale1 + shift).astype(BF16)
        z = _dot(h, w_ref[...])
        _store_kv(z[:, 0:KV_LORA], z[:, KV_LORA:KV_LORA + LANES], kvnorm_ref, wk_ref, wvt_ref, k_ref, vt_ref, e)


def _cd_proj_kernel(x_ref, mod_ref, w_ref, cs_ref, qnorm_ref, wqt_ref, kvnorm_ref, wk_ref, wvt_ref,
                    cosqt_ref, sinqt_ref, cosk_ref, sink_ref,
                    xcs_ref, gf_ref, gd_ref, qt_ref, k_ref, vt_ref):
    shift, scale1, _ = _split_mod(mod_ref)
    h = (x_ref[0] * scale1 + shift).astype(BF16)
    fw = FNET_WIDTH
    o = 2 * fw + MLA_WIDTH
    small = _dot(h, w_ref[:, o:o + Q_LORA + KV_LORA + LANES])
    cq = small[:, 0:Q_LORA]
    ckv = small[:, Q_LORA:Q_LORA + KV_LORA]
    kr = small[:, Q_LORA + KV_LORA:Q_LORA + KV_LORA + LANES]
    f_in = _dot(h, w_ref[:, 0:fw]).astype(BF16)
    gf_ref[0] = _silu(_dot(h, w_ref[:, fw:2 * fw])).astype(BF16)
    gd_ref[0] = _silu(_dot(h, w_ref[:, 2 * fw:2 * fw + MLA_WIDTH])).astype(BF16)

    qn = _rms_rows(cq, qnorm_ref[...]).astype(BF16)
    q2t = _dot_nt(wqt_ref[...], qn)
    head_dim = QK_NOPE + QK_ROPE
    swap0 = MLA_HEADS * head_dim
    cosqt, sinqt = cosqt_ref[...], sinqt_ref[...]
    tl = q2t.shape[1]
    pad = jnp.zeros((HEAD_PAD - head_dim, tl), F32)
    for hd in range(MLA_HEADS):
        nope = q2t[hd * head_dim:hd * head_dim + QK_NOPE, :]
        rope = q2t[hd * head_dim + QK_NOPE:(hd + 1) * head_dim, :]
        rope_sw = q2t[swap0 + hd * QK_ROPE:swap0 + (hd + 1) * QK_ROPE, :]
        qt_ref[0, hd] = jnp.concatenate(
            [nope * Q_SCALE, rope * cosqt + rope_sw * sinqt, pad], axis=0).astype(BF16)

    for hd in range(FNET_HEADS):
        cols = slice(hd * GROUP_DIM, (hd + 1) * GROUP_DIM)
        xcs = _dot(f_in[:, cols], cs_ref[...]).astype(BF16)
        xcs_ref[0, 0, :, cols] = xcs[:, 0:GROUP_DIM]
        xcs_ref[0, 1, :, cols] = xcs[:, GROUP_DIM:2 * GROUP_DIM]

    kr_rot = kr * cosk_ref[...] + pltpu.roll(kr, LANES // 2, 1) * sink_ref[...]
    _store_kv(ckv, kr_rot, kvnorm_ref, wk_ref, wvt_ref, k_ref, vt_ref)


def _const_spec(a):
    nd = a.ndim
    return pl.BlockSpec(a.shape, lambda b, t: (0,) * nd)


def _cd_proj_ctx(ctx, mod, mod_row, w_kv, kv_norm, wk, wvt):
    bsz, seq, d = ctx.shape
    hp = HEAD_PAD
    grp = max(g for g in (4, 2, 1) if bsz % g == 0)
    return pl.pallas_call(
        _cd_proj_ctx_kernel,
        grid=(bsz // grp, 1),
        in_specs=[pl.BlockSpec((grp, seq, d), lambda b, t: (b, 0, 0)),
                  pl.BlockSpec((1, 1, 3 * d), lambda b, t: (mod_row(b), 0, 0)),
                  _const_spec(w_kv), _const_spec(kv_norm), _const_spec(wk), _const_spec(wvt)],
        out_specs=[pl.BlockSpec((grp, MLA_HEADS, seq, hp), lambda b, t: (b, 0, 0, 0)),
                   pl.BlockSpec((grp, MLA_WIDTH, seq), lambda b, t: (b, 0, 0))],
        out_shape=[jax.ShapeDtypeStruct((bsz, MLA_HEADS, seq, hp), BF16),
                   jax.ShapeDtypeStruct((bsz, MLA_WIDTH, seq), BF16)],
        compiler_params=pltpu.CompilerParams(dimension_semantics=("parallel", "arbitrary"),
                                             vmem_limit_bytes=VMEM_LIMIT),
        name="cd_proj_ctx",
    )(ctx, mod, w_kv, kv_norm, wk, wvt)


def _cd_proj(x, mod, w1, cs, q_norm, wqt, kv_norm, wk, wvt, cosqt, sinqt, cosk, sink, *, tl):
    bsz, seq, d = x.shape
    hp = HEAD_PAD
    fw = FNET_WIDTH
    tab = pl.BlockSpec((tl, LANES), lambda b, t: (t, 0))
    tab_t = pl.BlockSpec((QK_ROPE, tl), lambda b, t: (0, t))
    return pl.pallas_call(
        _cd_proj_kernel,
        grid=(bsz, seq // tl),
        in_specs=[pl.BlockSpec((1, tl, d), lambda b, t: (b, t, 0)),
                  pl.BlockSpec((1, 1, 3 * d), lambda b, t: (b, 0, 0)),
                  _const_spec(w1), _const_spec(cs), _const_spec(q_norm), _const_spec(wqt),
                  _const_spec(kv_norm), _const_spec(wk), _const_spec(wvt), tab_t, tab_t, tab, tab],
        out_specs=[pl.BlockSpec((1, 2, tl, fw), lambda b, t: (b, 0, t, 0)),
                   pl.BlockSpec((1, tl, fw), lambda b, t: (b, t, 0)),
                   pl.BlockSpec((1, tl, MLA_WIDTH), lambda b, t: (b, t, 0)),
                   pl.BlockSpec((1, MLA_HEADS, hp, tl), lambda b, t: (b, 0, 0, t)),
                   pl.BlockSpec((1, MLA_HEADS, tl, hp), lambda b, t: (b, 0, t, 0)),
                   pl.BlockSpec((1, MLA_WIDTH, tl), lambda b, t: (b, 0, t))],
        out_shape=[jax.ShapeDtypeStruct((bsz, 2, seq, fw), BF16),
                   jax.ShapeDtypeStruct((bsz, seq, fw), BF16),
                   jax.ShapeDtypeStruct((bsz, seq, MLA_WIDTH), BF16),
                   jax.ShapeDtypeStruct((bsz, MLA_HEADS, hp, seq), BF16),
                   jax.ShapeDtypeStruct((bsz, MLA_HEADS, seq, hp), BF16),
                   jax.ShapeDtypeStruct((bsz, MLA_WIDTH, seq), BF16)],
        compiler_params=pltpu.CompilerParams(dimension_semantics=("parallel", "arbitrary"),
                                             vmem_limit_bytes=VMEM_LIMIT),
        name="cd_proj",
    )(x, mod, w1, cs, q_norm, wqt, kv_norm, wk, wvt, cosqt, sinqt, cosk, sink)


DFT_FINE = 32
FOLD_BLOCK = 256
DFT_BLOCK = 256
DFT_AFTER_HEADS = (1, 3)


KEY_CHUNK = 256
SUBLANES = 8
BF16_SUBLANES = 16
AHEAD = 4


def _fold_rows(s, op):
    while s.shape[0] > SUBLANES:
        half = s.shape[0] // 2
        s = op(s[:half], s[half:])
    return s


def _fold_sequence(xcs_ref, mirror_ref, fold_ref, seq_len):
    half = seq_len // 2
    blk = FOLD_BLOCK
    n_blk = seq_len // blk
    for part, sign in ((0, 1.0), (1, -1.0)):
        base = part * seq_len
        for b in range(half // blk):
            low = xcs_ref[0, base + b * blk:base + (b + 1) * blk, :].astype(F32)
            if b == 0:
                upper = xcs_ref[0, base + (n_blk - 1) * blk:base + n_blk * blk, :]
                mirrored = _dot(mirror_ref[:, 0:blk], upper)
            else:
                upper = xcs_ref[0, base + (n_blk - 1 - b) * blk:base + (n_blk + 1 - b) * blk, :]
                mirrored = _dot(mirror_ref[...], upper)
            fold_ref[part * half + b * blk:part * half + (b + 1) * blk, :] = (low + sign * mirrored).astype(BF16)
    fold_ref[half:half + BF16_SUBLANES, :] = jnp.where(
        lax.broadcasted_iota(jnp.int32, (BF16_SUBLANES, FNET_WIDTH), 0) == 0,
        xcs_ref[0, half:half + BF16_SUBLANES, :], fold_ref[half:half + BF16_SUBLANES, :])


def _cd_main_kernel(qt_ref, qtn_ref, kc_ref, kl_ref, vtc_ref, vtl_ref, xcs_ref, mirror_ref, ac_ref, as_ref,
                    bc_ref, bs_ref, gf_ref, gd_ref, x_ref, mod_ref, fw_ref, wout_ref, lng_ref, lnb_ref,
                    o_ref, cls_ref, fold_ref, attnt_ref, m_ref, *s_refs, tq, seq_len):
    lc = kc_ref.shape[2]
    n_chunks = (lc + seq_len) // KEY_CHUNK
    ctx_chunks = lc // KEY_CHUNK
    half = seq_len // 2

    def chunk_of(ctx_ref, lat_ref, hd, c, lanes):
        ref, c = (ctx_ref, c) if c < ctx_chunks else (lat_ref, c - ctx_chunks)
        span = slice(c * KEY_CHUNK, (c + 1) * KEY_CHUNK)
        return ref[0, hd, :, span] if lanes else ref[0, hd, span, :]

    def score_chunk(s_ref, q_ref, hd, c, m8):
        s = _dot(chunk_of(kc_ref, kl_ref, hd, c, False), q_ref[0, hd])
        s_ref[c * KEY_CHUNK:(c + 1) * KEY_CHUNK, :] = s
        part = _fold_rows(s, jnp.maximum)
        return part if m8 is None else jnp.maximum(m8, part)

    ones_rows = jnp.ones((BF16_SUBLANES, KEY_CHUNK), BF16)

    def prob_chunk(s, hd, c, m, acc):
        p = jnp.exp2(s - m)
        vt_ext = jnp.concatenate([chunk_of(vtc_ref, vtl_ref, hd, c, True), ones_rows], axis=0)
        o_t = _dot(vt_ext, p.astype(BF16))
        return o_t if acc is None else acc + o_t

    def head_step(s_ref, hd, m8, q_ref, q_head):
        m = jnp.max(m8, axis=0, keepdims=True)
        m8, acc = None, None
        for c in range(n_chunks):
            s_old = s_ref[c * KEY_CHUNK:(c + 1) * KEY_CHUNK, :]
            m8 = score_chunk(s_ref, q_ref, q_head, c, m8)
            acc = prob_chunk(s_old, hd, c, m, acc)
        attnt_ref[hd] = acc[0:V_DIM] * (1.0 / acc[V_DIM:V_DIM + 1])
        return m8

    @pl.when(pl.program_id(1) == 0)
    def _():
        _fold_sequence(xcs_ref, mirror_ref, fold_ref, seq_len)
        m8s = [None] * AHEAD
        for c in range(n_chunks):
            for hd in range(AHEAD):
                m8s[hd] = score_chunk(s_refs[hd], qt_ref, hd, c, m8s[hd])
        for hd in range(AHEAD):
            m_ref[hd] = m8s[hd]

    bc, bs = bc_ref[...], bs_ref[...]
    for i in range(tq // DFT_FINE):
        ac, asn = ac_ref[i:i + 1, :], as_ref[i:i + 1, :]
        rows = slice(i * DFT_FINE, (i + 1) * DFT_FINE)
        cls_ref[rows, 0:half] = (ac * bc - asn * bs).astype(BF16)
        cls_ref[rows, half:2 * half] = (asn * bc + ac * bs).astype(BF16)

    specs, yc = [], None
    m8s = [m_ref[hd] for hd in range(AHEAD)]
    for hd in range(MLA_HEADS):
        slot = hd % AHEAD
        ahead = hd + AHEAD
        q_src, q_head = (qt_ref, ahead) if ahead < MLA_HEADS else (qtn_ref, ahead - MLA_HEADS)
        m8s[slot] = head_step(s_refs[slot], hd, m8s[slot], q_src, q_head)
        if hd in DFT_AFTER_HEADS:
            j = DFT_AFTER_HEADS.index(hd)
            cols = slice(j * DFT_BLOCK, (j + 1) * DFT_BLOCK)
            specs.append(_dot(cls_ref[...], fold_ref[:, cols]).astype(BF16))
        if hd == DFT_AFTER_HEADS[-1] + 1:
            yc = _dot(jnp.concatenate(specs, axis=-1), fw_ref[...]) * gf_ref[0].astype(F32)
    for hd in range(AHEAD):
        m_ref[hd] = m8s[hd]

    yd = attnt_ref[...].reshape(MLA_WIDTH, tq).T * gd_ref[0].astype(F32)
    y = _dot(jnp.concatenate([yc.astype(BF16), yd.astype(BF16)], axis=-1), wout_ref[...])
    _, _, gate = _split_mod(mod_ref)
    o_ref[0] = _deep_norm_rows(x_ref[0], y, gate * (1.0 / ALPHA), lng_ref[...], lnb_ref[...])


def _cd_main(qt, kc, kl, vtc, vtl, xcs, mirror, a_cos, a_sin, b_cos, b_sin, gf, gd, x, mod, fnet_w, w_out,
             ln_g, ln_b, *, tq):
    bsz, seq, d = x.shape
    lc = kc.shape[2]
    hp = HEAD_PAD
    nh = MLA_HEADS
    per_b4 = lambda b, t: (b, 0, 0, 0)
    per_b3 = lambda b, t: (b, 0, 0)
    kern = functools.partial(_cd_main_kernel, tq=tq, seq_len=seq)
    coarse = tq // DFT_FINE
    last_t = seq // tq - 1
    return pl.pallas_call(
        kern,
        grid=(bsz, seq // tq),
        in_specs=[pl.BlockSpec((1, nh, hp, tq), lambda b, t: (b, 0, 0, t)),
                  pl.BlockSpec((1, nh, hp, tq), lambda b, t: (b, 0, 0, jnp.minimum(t + 1, last_t))),
                  pl.BlockSpec((1, nh, lc, hp), per_b4),
                  pl.BlockSpec((1, nh, seq, hp), per_b4),
                  pl.BlockSpec((1, nh, V_DIM, lc), per_b4),
                  pl.BlockSpec((1, nh, V_DIM, seq), per_b4),
                  pl.BlockSpec((1, 2 * seq, FNET_WIDTH), per_b3),
                  _const_spec(mirror),
                  pl.BlockSpec((coarse, seq // 2), lambda b, t: (t, 0)),
                  pl.BlockSpec((coarse, seq // 2), lambda b, t: (t, 0)),
                  _const_spec(b_cos), _const_spec(b_sin),
                  pl.BlockSpec((1, tq, FNET_WIDTH), lambda b, t: (b, t, 0)),
                  pl.BlockSpec((1, tq, MLA_WIDTH), lambda b, t: (b, t, 0)),
                  pl.BlockSpec((1, tq, d), lambda b, t: (b, t, 0)),
                  pl.BlockSpec((1, 1, 3 * d), lambda b, t: (b, 0, 0)),
                  _const_spec(fnet_w), _const_spec(w_out), _const_spec(ln_g), _const_spec(ln_b)],
        out_specs=pl.BlockSpec((1, tq, d), lambda b, t: (b, t, 0)),
        out_shape=jax.ShapeDtypeStruct(x.shape, F32),
        scratch_shapes=[pltpu.VMEM((tq, seq), BF16),
                        pltpu.VMEM((seq, FNET_WIDTH), BF16),
                        pltpu.VMEM((nh, V_DIM, tq), F32),
                        pltpu.VMEM((AHEAD, SUBLANES, tq), F32)]
                       + [pltpu.VMEM((lc + seq, tq), F32)] * AHEAD,
        compiler_params=pltpu.CompilerParams(dimension_semantics=("parallel", "arbitrary"),
                                             vmem_limit_bytes=VMEM_LIMIT),
        name="cd_main",
    )(qt, qt, kc, kl, vtc, vtl, xcs, mirror, a_cos, a_sin, b_cos, b_sin, gf, gd, x, mod, fnet_w, w_out, ln_g, ln_b)


def _rope_tables(seq):
    pos = np.arange(seq)
    row = (pos // GRID_W).astype(np.float64)
    col = (pos % GRID_W).astype(np.float64)
    half = ROPE_AXIS // 2
    inv = (ROPE_BASE ** (-np.arange(0, ROPE_AXIS, 2, dtype=np.float32) / ROPE_AXIS)).astype(np.float32)
    cos = np.ones((seq, LANES), np.float32)
    sin = np.zeros((seq, LANES), np.float32)
    for j in range(QK_ROPE):
        coord = row if j < ROPE_AXIS else col
        ang = (coord.astype(np.float32) * inv[j % half]).astype(np.float32)
        cos[:, QK_NOPE + j] = np.cos(ang)
        sign = -1.0 if (j % ROPE_AXIS) < half else 1.0
        sin[:, QK_NOPE + j] = sign * np.sin(ang)
    return cos, sin


def _rope_partner(j):
    half = ROPE_AXIS // 2
    return j + half if (j % ROPE_AXIS) < half else j - half


def _dft_tables(seq):
    l = np.arange(seq // 2, dtype=np.float64)
    k1 = np.arange(seq // DFT_FINE, dtype=np.float64)[:, None]
    k0 = np.arange(DFT_FINE, dtype=np.float64)[:, None]
    ang_a = 2.0 * np.pi * ((k1 * DFT_FINE * l) % seq) / seq
    ang_b = 2.0 * np.pi * ((k0 * l) % seq) / seq
    norm = 1.0 / np.sqrt(seq * GROUP_DIM)
    b_sin = np.sin(ang_b) * norm
    b_sin[:, 0] = np.where(np.arange(DFT_FINE) % 2 == 0, norm, -norm)
    return (np.cos(ang_a).astype(np.float32), np.sin(ang_a).astype(np.float32),
            (np.cos(ang_b) * norm).astype(np.float32), b_sin.astype(np.float32))


def _mirror_matrix():
    m = np.zeros((FOLD_BLOCK, 2 * FOLD_BLOCK), np.float32)
    i = np.arange(1, FOLD_BLOCK)
    m[i, FOLD_BLOCK - i] = 1.0
    m[0, FOLD_BLOCK] = 1.0
    return m


def _channel_dft():
    c = np.arange(GROUP_DIM, dtype=np.float64)
    ang = 2.0 * np.pi * ((c[:, None] * c[None, :]) % GROUP_DIM) / GROUP_DIM
    return np.concatenate([np.cos(ang), -np.sin(ang)], axis=1).astype(np.float32)


def kernel(x, c, ctx, c_ctx, ab_w_mod, ab_b_mod, ab_w_in, ab_pool_w, ab_pool_scale, ab_sgu_w, ab_sgu_b,
           ab_w_out, ab_ln_g, ab_ln_b, cd_w_mod, cd_b_mod, cd_w_in, cd_fnet_w, cd_q_norm, cd_kv_norm,
           cd_w_q_up, cd_w_kv_up, cd_w_out, cd_ln_g, cd_ln_b):
    bsz, seq, d = x.shape
    nh, hp = MLA_HEADS, HEAD_PAD
    ctx_len = ctx.shape[1]
    assert d == D_MODEL and seq % ROW_TILE == 0 and seq % (2 * FOLD_BLOCK) == 0 and seq % GRID_W == 0
    assert seq % QUERY_TILE == 0 and QUERY_TILE % DFT_FINE == 0 and FNET_WIDTH == len(DFT_AFTER_HEADS) * DFT_BLOCK
    assert ctx_len % KEY_CHUNK == 0 and ctx_len % OUT_SLAB == 0 and seq % KEY_CHUNK == 0
    assert max(POOL_WINDOWS) // 2 <= SUBLANES <= HALO and MLA_HEADS % AHEAD == 0

    rows = -(-(bsz + 1) // 8) * 8
    cond = jnp.zeros((rows, d), F32).at[:bsz].set(c).at[bsz].set(c_ctx)
    mod_ab = _modulation(cond, ab_w_mod[0], ab_b_mod[0])
    mod_cd = _modulation(cond, cd_w_mod[0], cd_b_mod[0])
    lat_row = lambda b: b
    ctx_row = lambda b: bsz

    ab_args = (ab_w_in[0].astype(BF16), ab_pool_w[0].astype(BF16), ab_pool_scale[0].reshape(1, -1),
               ab_sgu_w[0].astype(BF16),
               jnp.broadcast_to(ab_sgu_b[0][:, :, None], (SGU_HEADS, CHUNK, GROUP_DIM)),
               ab_w_out[0].astype(BF16), ab_ln_g[0].reshape(1, -1), ab_ln_b[0].reshape(1, -1))
    x1 = _ab_layer(x, mod_ab, lat_row, *ab_args, tl=ROW_TILE)
    ctx1 = _ab_layer(ctx, mod_ab, ctx_row, *ab_args, tl=ctx.shape[1])

    w = cd_w_in[0]
    o_cq = 2 * FNET_WIDTH
    o_ckv = o_cq + Q_LORA
    o_kr = o_ckv + KV_LORA
    o_dg = o_kr + QK_ROPE
    partner = np.array([_rope_partner(j) for j in range(QK_ROPE)])
    zeros = lambda n: jnp.zeros((d, n), F32)
    kr_cols = w[:, o_kr:o_dg]
    kr_lanes = jnp.concatenate([zeros(QK_NOPE), kr_cols, zeros(LANES - QK_NOPE - QK_ROPE)], axis=1)
    kr_both = jnp.concatenate([kr_cols[:, partner], zeros(QK_NOPE - QK_ROPE), kr_cols,
                               zeros(LANES - QK_NOPE - QK_ROPE)], axis=1)
    w1 = jnp.concatenate([w[:, 0:o_cq], w[:, o_dg:], w[:, o_cq:o_kr], kr_both], axis=1).astype(BF16)
    w_kv_ctx = jnp.concatenate([w[:, o_ckv:o_kr], kr_lanes], axis=1).astype(BF16)

    wq = cd_w_q_up[0].reshape(Q_LORA, nh, QK_NOPE + QK_ROPE)
    wqt = jnp.concatenate([wq.reshape(Q_LORA, -1), wq[:, :, QK_NOPE + partner].reshape(Q_LORA, -1)],
                          axis=1).T.astype(BF16)

    wkv = cd_w_kv_up[0].reshape(KV_LORA, nh, QK_NOPE + V_DIM)
    wk_pad = jnp.concatenate([wkv[:, :, :QK_NOPE], jnp.zeros((KV_LORA, nh, hp - QK_NOPE), F32)], axis=-1)
    lane = np.arange(hp)
    sel = np.diag(((lane >= QK_NOPE) & (lane < QK_NOPE + QK_ROPE)).astype(np.float32))
    wk = jnp.concatenate([wk_pad.reshape(KV_LORA, nh * hp),
                          jnp.asarray(np.tile(sel, (1, nh)))], axis=0).astype(BF16)
    wvt = wkv[:, :, QK_NOPE:].reshape(KV_LORA, nh * V_DIM).T.astype(BF16)

    q_norm = cd_q_norm[0].reshape(1, -1)
    kv_norm = cd_kv_norm[0].reshape(1, -1)
    cos, sin = _rope_tables(seq)
    rope_lanes = slice(QK_NOPE, QK_NOPE + QK_ROPE)
    cosq_t = (cos[:, rope_lanes] * np.float32(Q_SCALE)).T.copy()
    sinq_t = (sin[:, rope_lanes] * np.float32(Q_SCALE)).T.copy()
    cs = jnp.asarray(_channel_dft()).astype(BF16)

    kc, vtc = _cd_proj_ctx(ctx1, mod_cd, ctx_row, w_kv_ctx, kv_norm, wk, wvt)
    xcs, gf, gd, qt, kl, vtl = _cd_proj(x1, mod_cd, w1, cs, q_norm, wqt, kv_norm, wk, wvt,
                                        jnp.asarray(cosq_t), jnp.asarray(sinq_t),
                                        jnp.asarray(cos), jnp.asarray(sin), tl=ROW_TILE)
    tq = QUERY_TILE
    a_cos, a_sin, b_cos, b_sin = (jnp.asarray(t) for t in _dft_tables(seq))
    mirror = jnp.asarray(_mirror_matrix()).astype(BF16)
    vtc = vtc.reshape(bsz, nh, V_DIM, -1)
    vtl = vtl.reshape(bsz, nh, V_DIM, -1)
    return _cd_main(qt, kc, kl, vtc, vtl, xcs.reshape(bsz, 2 * seq, FNET_WIDTH), mirror,
                    a_cos, a_sin, b_cos, b_sin,
                    gf, gd, x1, mod_cd, cd_fnet_w[0].astype(BF16), cd_w_out[0].astype(BF16),
                    cd_ln_g[0].reshape(1, -1), cd_ln_b[0].reshape(1, -1), tq=tq)
```

```python
import functools

import numpy as np
import jax
import jax.numpy as jnp
from jax import lax
from jax.experimental import pallas as pl
from jax.experimental.pallas import tpu as pltpu

D_MODEL = 1024
GRID_W = 64
EPS = 1e-6
DEPTH = 2
POOL_WINDOWS = (2, 4, 8, 16)
GROUP_DIM = 128
POOL_WIDTH = 512
SGU_HEADS = 4
SGU_WIDTH = 512
CHUNK = 128
FNET_HEADS = 4
FNET_WIDTH = 512
MLA_HEADS = 8
QK_NOPE = 64
QK_ROPE = 32
V_DIM = 64
Q_LORA = 256
KV_LORA = 128
MLA_WIDTH = 512
ROPE_AXIS = QK_ROPE // 2
ROPE_BASE = 10000.0
ALPHA = (2 * DEPTH) ** 0.25
Q_SCALE = float((QK_NOPE + QK_ROPE) ** -0.5 * np.log2(np.e))

LANES = 128
HALO = 16
HEAD_PAD = LANES
OUT_SLAB = 256
VMEM_LIMIT = 56 * 1024 * 1024
ROW_TILE = 1024
QUERY_TILE = 256
MOD_COL_TILE = 768

F32 = jnp.float32
BF16 = jnp.bfloat16


def _silu(v):
    half = 0.5 * v
    return half * (1.0 + jnp.tanh(half))


def _dot(a, b):
    return jnp.dot(a, b, preferred_element_type=F32)


def _deep_norm_rows(x, y, gate_scaled, g, b):
    r = x + gate_scaled * y
    mu = jnp.mean(r, axis=-1, keepdims=True)
    rc = r - mu
    var = jnp.mean(rc * rc, axis=-1, keepdims=True)
    return rc * lax.rsqrt(var + EPS / (ALPHA * ALPHA)) * g + b


def _rms_rows(v, g):
    return v * lax.rsqrt(jnp.mean(v * v, axis=-1, keepdims=True) + EPS) * g


def _mod_kernel(c_ref, w_ref, b_ref, o_ref):
    s = _silu(c_ref[...]).astype(BF16)
    o_ref[...] = _dot(s, w_ref[...].astype(BF16)) + b_ref[...]


def _modulation(cond, w_mod, b_mod):
    rows, d = cond.shape
    n = w_mod.shape[1]
    tn = MOD_COL_TILE
    out = pl.pallas_call(
        _mod_kernel,
        grid=(n // tn,),
        in_specs=[pl.BlockSpec((rows, d), lambda j: (0, 0)),
                  pl.BlockSpec((d, tn), lambda j: (0, j)),
                  pl.BlockSpec((1, tn), lambda j: (0, j))],
        out_specs=pl.BlockSpec((rows, tn), lambda j: (0, j)),
        out_shape=jax.ShapeDtypeStruct((rows, n), F32),
        name="modulation",
    )(cond, w_mod, b_mod.reshape(1, n))
    return out.reshape(rows, 1, n)


def _split_mod(mod_ref):
    d = D_MODEL
    return mod_ref[0, :, 0:d], 1.0 + mod_ref[0, :, d:2 * d], mod_ref[0, :, 2 * d:3 * d]


def _ab_kernel(x_ref, xp_ref, xn_ref, mod_ref, win_ref, poolw_ref, pscale_ref, sguw_ref, sgub_ref,
               wout_ref, lng_ref, lnb_ref, o_ref, h_ref, a_ref, mix_ref, *, tl, seq_len):
    t = pl.program_id(1)
    nt = pl.num_programs(1)
    shift, scale1, gate = _split_mod(mod_ref)
    x = x_ref[0]
    h_ref[HALO:HALO + tl, :] = (x * scale1 + shift).astype(BF16)
    h_ref[0:HALO, :] = (xp_ref[0] * scale1 + shift).astype(BF16)
    h_ref[HALO + tl:2 * HALO + tl, :] = (xn_ref[0] * scale1 + shift).astype(BF16)

    a_ext = _dot(h_ref[...], win_ref[:, 0:POOL_WIDTH])
    a_ref[HALO:HALO + tl, :] = a_ext[HALO:HALO + tl]
    a_ref[0:HALO, :] = jnp.where(t > 0, a_ext[0:HALO], 0.0)
    a_ref[HALO + tl:2 * HALO + tl, :] = jnp.where(t < nt - 1, a_ext[HALO + tl:2 * HALO + tl], 0.0)

    base = 2 * POOL_WIDTH
    slabs = [slice(r0, r0 + OUT_SLAB) for r0 in range(0, tl, OUT_SLAB)]

    def wide_projections(rows):
        h = h_ref[HALO + rows.start:HALO + rows.stop, :]
        v = _dot(h, win_ref[:, base + SGU_WIDTH:base + 2 * SGU_WIDTH])
        u = _dot(h, win_ref[:, base:base + SGU_WIDTH])
        gate_a = _silu(_dot(h, win_ref[:, POOL_WIDTH:2 * POOL_WIDTH]))
        gate_b = _silu(_dot(h, win_ref[:, base + 2 * SGU_WIDTH:base + 3 * SGU_WIDTH]))
        return v, u, gate_a, gate_b

    def mix_and_project(rows, v, u, gate_a, gate_b):
        n = rows.stop - rows.start
        pos = t * tl + rows.start + lax.broadcasted_iota(jnp.int32, (SUBLANES, LANES), 0)
        for g, w in enumerate(POOL_WINDOWS):
            cols = slice(g * GROUP_DIM, (g + 1) * GROUP_DIM)
            first = HALO + rows.start - w // 2
            s = a_ref[first:first + n, cols]
            for j in range(1, w):
                s = s + a_ref[first + j:first + j + n, cols]
            edges = []
            for r0 in (0, n - SUBLANES):
                p = pos + r0
                count = jnp.minimum(p + (w - w // 2), seq_len) - jnp.maximum(p - w // 2, 0)
                edges.append(s[r0:r0 + SUBLANES] / count.astype(F32))
            pooled = jnp.concatenate([edges[0], s[SUBLANES:n - SUBLANES] * (1.0 / w), edges[1]], axis=0)
            diff = pooled - a_ref[HALO + rows.start:HALO + rows.stop, cols]
            ya = _dot(diff.astype(BF16), poolw_ref[g]) * pscale_ref[:, cols] * gate_a[:, cols]
            mix_ref[rows, cols] = ya.astype(BF16)

        for hd in range(SGU_HEADS):
            cols = slice(hd * GROUP_DIM, (hd + 1) * GROUP_DIM)
            vh = v[:, cols]
            mu = jnp.mean(vh, axis=-1, keepdims=True)
            vc = vh - mu
            var = jnp.mean(vc * vc, axis=-1, keepdims=True)
            vn = (vc * lax.rsqrt(var + EPS)).astype(BF16)
            f = jnp.concatenate(
                [_dot(sguw_ref[hd], vn[ci * CHUNK:(ci + 1) * CHUNK, :]) + sgub_ref[hd]
                 for ci in range(n // CHUNK)], axis=0)
            yb = u[:, cols] * f * gate_b[:, cols]
            mix_ref[rows, POOL_WIDTH + hd * GROUP_DIM:POOL_WIDTH + (hd + 1) * GROUP_DIM] = yb.astype(BF16)

        y = _dot(mix_ref[rows, :], wout_ref[...])
        o_ref[0, rows, :] = _deep_norm_rows(x_ref[0, rows, :], y, gate * (1.0 / ALPHA), lng_ref[...], lnb_ref[...])

    pending = None
    for rows in slabs:
        wide = wide_projections(rows)
        if pending is not None:
            mix_and_project(*pending)
        pending = (rows,) + wide
    mix_and_project(*pending)


def _ab_layer(x, mod, mod_row, w_in, pool_w, pool_scale, sgu_w, sgu_b, w_out, ln_g, ln_b, *, tl):
    bsz, seq, d = x.shape
    nt = seq // tl
    hb = tl // HALO
    last_hb = seq // HALO - 1
    const2 = lambda b, t: (0, 0)
    const3 = lambda b, t: (0, 0, 0)
    kern = functools.partial(_ab_kernel, tl=tl, seq_len=seq)
    return pl.pallas_call(
        kern,
        grid=(bsz, nt),
        in_specs=[
            pl.BlockSpec((1, tl, d), lambda b, t: (b, t, 0)),
            pl.BlockSpec((1, HALO, d), lambda b, t: (b, jnp.maximum(t * hb - 1, 0), 0)),
            pl.BlockSpec((1, HALO, d), lambda b, t: (b, jnp.minimum((t + 1) * hb, last_hb), 0)),
            pl.BlockSpec((1, 1, 3 * d), lambda b, t: (mod_row(b), 0, 0)),
            pl.BlockSpec(w_in.shape, const2),
            pl.BlockSpec(pool_w.shape, const3),
            pl.BlockSpec(pool_scale.shape, const2),
            pl.BlockSpec(sgu_w.shape, const3),
            pl.BlockSpec(sgu_b.shape, const3),
            pl.BlockSpec(w_out.shape, const2),
            pl.BlockSpec(ln_g.shape, const2),
            pl.BlockSpec(ln_b.shape, const2),
        ],
        out_specs=pl.BlockSpec((1, tl, d), lambda b, t: (b, t, 0)),
        out_shape=jax.ShapeDtypeStruct(x.shape, F32),
        scratch_shapes=[pltpu.VMEM((tl + 2 * HALO, d), BF16),
                        pltpu.VMEM((tl + 2 * HALO, POOL_WIDTH), F32),
                        pltpu.VMEM((tl, POOL_WIDTH + SGU_WIDTH), BF16)],
        compiler_params=pltpu.CompilerParams(dimension_semantics=("parallel", "arbitrary"),
                                             vmem_limit_bytes=VMEM_LIMIT),
        name="ab_layer",
    )(x, x, x, mod, w_in, pool_w, pool_scale, sgu_w, sgu_b, w_out, ln_g, ln_b)


def _dot_nt(a, b):
    return lax.dot_general(a, b, (((1,), (1,)), ((), ())), preferred_element_type=F32)


def _store_kv(ckv, kr_lanes, kvnorm_ref, wk_ref, wvt_ref, k_ref, vt_ref, e=0):
    ckv_n = _rms_rows(ckv, kvnorm_ref[...]).astype(BF16)
    hk = jnp.concatenate([ckv_n, kr_lanes.astype(BF16)], axis=-1)
    kk = _dot(hk, wk_ref[...]).astype(BF16)
    for hd in range(MLA_HEADS):
        k_ref[e, hd] = kk[:, hd * HEAD_PAD:(hd + 1) * HEAD_PAD]
    vt_ref[e] = _dot_nt(wvt_ref[...], ckv_n).astype(BF16)


def _cd_proj_ctx_kernel(x_ref, mod_ref, w_ref, kvnorm_ref, wk_ref, wvt_ref, k_ref, vt_ref):
    shift, scale1, _ = _split_mod(mod_ref)
    for e in range(x_ref.shape[0]):
        h = (x_ref[e] * scale1 + shift).astype(BF16)
        z = _dot(h, w_ref[...])
        _store_kv(z[:, 0:KV_LORA], z[:, KV_LORA:KV_LORA + LANES], kvnorm_ref, wk_ref, wvt_ref, k_ref, vt_ref, e)


def _cd_proj_kernel(x_ref, mod_ref, w_ref, cs_ref, qnorm_ref, wqt_ref, kvnorm_ref, wk_ref, wvt_ref,
                    cosqt_ref, sinqt_ref, cosk_ref, sink_ref,
                    xcs_ref, gf_ref, gd_ref, qt_ref, k_ref, vt_ref):
    shift, scale1, _ = _split_mod(mod_ref)
    h = (x_ref[0] * scale1 + shift).astype(BF16)
    fw = FNET_WIDTH
    o = 2 * fw + MLA_WIDTH
    small = _dot(h, w_ref[:, o:o + Q_LORA + KV_LORA + LANES])
    cq = small[:, 0:Q_LORA]
    ckv = small[:, Q_LORA:Q_LORA + KV_LORA]
    kr = small[:, Q_LORA + KV_LORA:Q_LORA + KV_LORA + LANES]
    f_in = _dot(h, w_ref[:, 0:fw]).astype(BF16)
    gf_ref[0] = _silu(_dot(h, w_ref[:, fw:2 * fw])).astype(BF16)
    gd_ref[0] = _silu(_dot(h, w_ref[:, 2 * fw:2 * fw + MLA_WIDTH])).astype(BF16)

    qn = _rms_rows(cq, qnorm_ref[...]).astype(BF16)
    q2t = _dot_nt(wqt_ref[...], qn)
    head_dim = QK_NOPE + QK_ROPE
    swap0 = MLA_HEADS * head_dim
    cosqt, sinqt = cosqt_ref[...], sinqt_ref[...]
    tl = q2t.shape[1]
    pad = jnp.zeros((HEAD_PAD - head_dim, tl), F32)
    for hd in range(MLA_HEADS):
        nope = q2t[hd * head_dim:hd * head_dim + QK_NOPE, :]
        rope = q2t[hd * head_dim + QK_NOPE:(hd + 1) * head_dim, :]
        rope_sw = q2t[swap0 + hd * QK_ROPE:swap0 + (hd + 1) * QK_ROPE, :]
        qt_ref[0, hd] = jnp.concatenate(
            [nope * Q_SCALE, rope * cosqt + rope_sw * sinqt, pad], axis=0).astype(BF16)

    for hd in range(FNET_HEADS):
        cols = slice(hd * GROUP_DIM, (hd + 1) * GROUP_DIM)
        xcs = _dot(f_in[:, cols], cs_ref[...]).astype(BF16)
        xcs_ref[0, 0, :, cols] = xcs[:, 0:GROUP_DIM]
        xcs_ref[0, 1, :, cols] = xcs[:, GROUP_DIM:2 * GROUP_DIM]

    kr_rot = kr * cosk_ref[...] + pltpu.roll(kr, LANES // 2, 1) * sink_ref[...]
    _store_kv(ckv, kr_rot, kvnorm_ref, wk_ref, wvt_ref, k_ref, vt_ref)


def _const_spec(a):
    nd = a.ndim
    return pl.BlockSpec(a.shape, lambda b, t: (0,) * nd)


def _cd_proj_ctx(ctx, mod, mod_row, w_kv, kv_norm, wk, wvt):
    bsz, seq, d = ctx.shape
    hp = HEAD_PAD
    grp = max(g for g in (4, 2, 1) if bsz % g == 0)
    return pl.pallas_call(
        _cd_proj_ctx_kernel,
        grid=(bsz // grp, 1),
        in_specs=[pl.BlockSpec((grp, seq, d), lambda b, t: (b, 0, 0)),
                  pl.BlockSpec((1, 1, 3 * d), lambda b, t: (mod_row(b), 0, 0)),
                  _const_spec(w_kv), _const_spec(kv_norm), _const_spec(wk), _const_spec(wvt)],
        out_specs=[pl.BlockSpec((grp, MLA_HEADS, seq, hp), lambda b, t: (b, 0, 0, 0)),
                   pl.BlockSpec((grp, MLA_WIDTH, seq), lambda b, t: (b, 0, 0))],
        out_shape=[jax.ShapeDtypeStruct((bsz, MLA_HEADS, seq, hp), BF16),
                   jax.ShapeDtypeStruct((bsz, MLA_WIDTH, seq), BF16)],
        compiler_params=pltpu.CompilerParams(dimension_semantics=("parallel", "arbitrary"),
                                             vmem_limit_bytes=VMEM_LIMIT),
        name="cd_proj_ctx",
    )(ctx, mod, w_kv, kv_norm, wk, wvt)


def _cd_proj(x, mod, w1, cs, q_norm, wqt, kv_norm, wk, wvt, cosqt, sinqt, cosk, sink, *, tl):
    bsz, seq, d = x.shape
    hp = HEAD_PAD
    fw = FNET_WIDTH
    tab = pl.BlockSpec((tl, LANES), lambda b, t: (t, 0))
    tab_t = pl.BlockSpec((QK_ROPE, tl), lambda b, t: (0, t))
    return pl.pallas_call(
        _cd_proj_kernel,
        grid=(bsz, seq // tl),
        in_specs=[pl.BlockSpec((1, tl, d), lambda b, t: (b, t, 0)),
                  pl.BlockSpec((1, 1, 3 * d), lambda b, t: (b, 0, 0)),
                  _const_spec(w1), _const_spec(cs), _const_spec(q_norm), _const_spec(wqt),
                  _const_spec(kv_norm), _const_spec(wk), _const_spec(wvt), tab_t, tab_t, tab, tab],
        out_specs=[pl.BlockSpec((1, 2, tl, fw), lambda b, t: (b, 0, t, 0)),
                   pl.BlockSpec((1, tl, fw), lambda b, t: (b, t, 0)),
                   pl.BlockSpec((1, tl, MLA_WIDTH), lambda b, t: (b, t, 0)),
                   pl.BlockSpec((1, MLA_HEADS, hp, tl), lambda b, t: (b, 0, 0, t)),
                   pl.BlockSpec((1, MLA_HEADS, tl, hp), lambda b, t: (b, 0, t, 0)),
                   pl.BlockSpec((1, MLA_WIDTH, tl), lambda b, t: (b, 0, t))],
        out_shape=[jax.ShapeDtypeStruct((bsz, 2, seq, fw), BF16),
                   jax.ShapeDtypeStruct((bsz, seq, fw), BF16),
                   jax.ShapeDtypeStruct((bsz, seq, MLA_WIDTH), BF16),
                   jax.ShapeDtypeStruct((bsz, MLA_HEADS, hp, seq), BF16),
                   jax.ShapeDtypeStruct((bsz, MLA_HEADS, seq, hp), BF16),
                   jax.ShapeDtypeStruct((bsz, MLA_WIDTH, seq), BF16)],
        compiler_params=pltpu.CompilerParams(dimension_semantics=("parallel", "arbitrary"),
                                             vmem_limit_bytes=VMEM_LIMIT),
        name="cd_proj",
    )(x, mod, w1, cs, q_norm, wqt, kv_norm, wk, wvt, cosqt, sinqt, cosk, sink)


DFT_FINE = 32
FOLD_BLOCK = 256
DFT_BLOCK = 256
DFT_AFTER_HEADS = (1, 3)


KEY_CHUNK = 256
SUBLANES = 8
BF16_SUBLANES = 16
AHEAD = 4
TILES_PER_STEP = 2


def _fold_rows(s, op):
    while s.shape[0] > SUBLANES:
        half = s.shape[0] // 2
        s = op(s[:half], s[half:])
    return s


def _fold_sequence(xcs_ref, mirror_ref, fold_ref, seq_len):
    half = seq_len // 2
    blk = FOLD_BLOCK
    n_blk = seq_len // blk
    for part, sign in ((0, 1.0), (1, -1.0)):
        base = part * seq_len
        for b in range(half // blk):
            low = xcs_ref[0, base + b * blk:base + (b + 1) * blk, :].astype(F32)
            if b == 0:
                upper = xcs_ref[0, base + (n_blk - 1) * blk:base + n_blk * blk, :]
                mirrored = _dot(mirror_ref[:, 0:blk], upper)
            else:
                upper = xcs_ref[0, base + (n_blk - 1 - b) * blk:base + (n_blk + 1 - b) * blk, :]
                mirrored = _dot(mirror_ref[...], upper)
            fold_ref[part * half + b * blk:part * half + (b + 1) * blk, :] = (low + sign * mirrored).astype(BF16)
    fold_ref[half:half + BF16_SUBLANES, :] = jnp.where(
        lax.broadcasted_iota(jnp.int32, (BF16_SUBLANES, FNET_WIDTH), 0) == 0,
        xcs_ref[0, half:half + BF16_SUBLANES, :], fold_ref[half:half + BF16_SUBLANES, :])


def _cd_main_kernel(qt_ref, qtn_ref, kc_ref, kl_ref, vtc_ref, vtl_ref, xcs_ref, mirror_ref, ac_ref, as_ref,
                    bc_ref, bs_ref, gf_ref, gd_ref, x_ref, mod_ref, fw_ref, wout_ref, lng_ref, lnb_ref,
                    o_ref, cls_ref, fold_ref, attnt_ref, m_ref, *s_refs, tq, seq_len):
    lc = kc_ref.shape[2]
    n_chunks = (lc + seq_len) // KEY_CHUNK
    ctx_chunks = lc // KEY_CHUNK
    half = seq_len // 2
    tiles = qt_ref.shape[3] // tq
    n_items = tiles * MLA_HEADS

    def query_of(item):
        tile, hd = divmod(item, MLA_HEADS)
        if tile < tiles:
            return qt_ref[0, hd, :, tile * tq:(tile + 1) * tq]
        return qtn_ref[0, hd]

    def chunk_of(ctx_ref, lat_ref, hd, c, lanes):
        ref, c = (ctx_ref, c) if c < ctx_chunks else (lat_ref, c - ctx_chunks)
        span = slice(c * KEY_CHUNK, (c + 1) * KEY_CHUNK)
        return ref[0, hd, :, span] if lanes else ref[0, hd, span, :]

    def score_chunk(s_ref, item, c, m8):
        hd = item % MLA_HEADS
        s = _dot(chunk_of(kc_ref, kl_ref, hd, c, False), query_of(item))
        s_ref[c * KEY_CHUNK:(c + 1) * KEY_CHUNK, :] = s
        part = _fold_rows(s, jnp.maximum)
        return part if m8 is None else jnp.maximum(m8, part)

    ones_rows = jnp.ones((BF16_SUBLANES, KEY_CHUNK), BF16)

    def prob_chunk(s, hd, c, m, acc):
        p = jnp.exp2(s - m)
        vt_ext = jnp.concatenate([chunk_of(vtc_ref, vtl_ref, hd, c, True), ones_rows], axis=0)
        o_t = _dot(vt_ext, p.astype(BF16))
        return o_t if acc is None else acc + o_t

    def item_step(item, m8):
        s_ref, hd = s_refs[item % AHEAD], item % MLA_HEADS
        m = jnp.max(m8, axis=0, keepdims=True)
        m8, acc = None, None
        for c in range(n_chunks):
            s_old = s_ref[c * KEY_CHUNK:(c + 1) * KEY_CHUNK, :]
            m8 = score_chunk(s_ref, item + AHEAD, c, m8)
            acc = prob_chunk(s_old, hd, c, m, acc)
        attnt_ref[hd] = acc[0:V_DIM] * (1.0 / acc[V_DIM:V_DIM + 1])
        return m8

    @pl.when(pl.program_id(1) == 0)
    def _():
        _fold_sequence(xcs_ref, mirror_ref, fold_ref, seq_len)
        m8s = [None] * AHEAD
        for c in range(n_chunks):
            for item in range(AHEAD):
                m8s[item] = score_chunk(s_refs[item], item, c, m8s[item])
        for item in range(AHEAD):
            m_ref[item] = m8s[item]

    _, _, gate = _split_mod(mod_ref)
    bc, bs = bc_ref[...], bs_ref[...]
    coarse = tq // DFT_FINE
    m8s = [m_ref[slot] for slot in range(AHEAD)]
    for tile in range(tiles):
        span = slice(tile * tq, (tile + 1) * tq)
        for i in range(coarse):
            ac = ac_ref[tile * coarse + i:tile * coarse + i + 1, :]
            asn = as_ref[tile * coarse + i:tile * coarse + i + 1, :]
            rows = slice(i * DFT_FINE, (i + 1) * DFT_FINE)
            cls_ref[tile, rows, 0:half] = (ac * bc - asn * bs).astype(BF16)
            cls_ref[tile, rows, half:2 * half] = (asn * bc + ac * bs).astype(BF16)
        specs, yc = [], None
        for hd in range(MLA_HEADS):
            item = tile * MLA_HEADS + hd
            m8s[item % AHEAD] = item_step(item, m8s[item % AHEAD])
            if hd in DFT_AFTER_HEADS:
                j = DFT_AFTER_HEADS.index(hd)
                cols = slice(j * DFT_BLOCK, (j + 1) * DFT_BLOCK)
                specs.append(_dot(cls_ref[tile], fold_ref[:, cols]).astype(BF16))
            if hd == DFT_AFTER_HEADS[-1] + 1:
                yc = _dot(jnp.concatenate(specs, axis=-1), fw_ref[...]) * gf_ref[0, span, :].astype(F32)
        yd = attnt_ref[...].reshape(MLA_WIDTH, tq).T * gd_ref[0, span, :].astype(F32)
        y = _dot(jnp.concatenate([yc.astype(BF16), yd.astype(BF16)], axis=-1), wout_ref[...])
        o_ref[0, span, :] = _deep_norm_rows(x_ref[0, span, :], y, gate * (1.0 / ALPHA), lng_ref[...], lnb_ref[...])
    assert n_items % AHEAD == 0
    for slot in range(AHEAD):
        m_ref[slot] = m8s[slot]


def _cd_main(qt, kc, kl, vtc, vtl, xcs, mirror, a_cos, a_sin, b_cos, b_sin, gf, gd, x, mod, fnet_w, w_out,
             ln_g, ln_b, *, tq):
    bsz, seq, d = x.shape
    lc = kc.shape[2]
    hp = HEAD_PAD
    nh = MLA_HEADS
    per_b4 = lambda b, t: (b, 0, 0, 0)
    per_b3 = lambda b, t: (b, 0, 0)
    kern = functools.partial(_cd_main_kernel, tq=tq, seq_len=seq)
    tiles = TILES_PER_STEP
    rows = tiles * tq
    coarse = rows // DFT_FINE
    last_tile = seq // tq - 1
    return pl.pallas_call(
        kern,
        grid=(bsz, seq // rows),
        in_specs=[pl.BlockSpec((1, nh, hp, rows), lambda b, t: (b, 0, 0, t)),
                  pl.BlockSpec((1, nh, hp, tq), lambda b, t: (b, 0, 0, jnp.minimum(tiles * (t + 1), last_tile))),
                  pl.BlockSpec((1, nh, lc, hp), per_b4),
                  pl.BlockSpec((1, nh, seq, hp), per_b4),
                  pl.BlockSpec((1, nh, V_DIM, lc), per_b4),
                  pl.BlockSpec((1, nh, V_DIM, seq), per_b4),
                  pl.BlockSpec((1, 2 * seq, FNET_WIDTH), per_b3),
                  _const_spec(mirror),
                  pl.BlockSpec((coarse, seq // 2), lambda b, t: (t, 0)),
                  pl.BlockSpec((coarse, seq // 2), lambda b, t: (t, 0)),
                  _const_spec(b_cos), _const_spec(b_sin),
                  pl.BlockSpec((1, rows, FNET_WIDTH), lambda b, t: (b, t, 0)),
                  pl.BlockSpec((1, rows, MLA_WIDTH), lambda b, t: (b, t, 0)),
                  pl.BlockSpec((1, rows, d), lambda b, t: (b, t, 0)),
                  pl.BlockSpec((1, 1, 3 * d), lambda b, t: (b, 0, 0)),
                  _const_spec(fnet_w), _const_spec(w_out), _const_spec(ln_g), _const_spec(ln_b)],
        out_specs=pl.BlockSpec((1, rows, d), lambda b, t: (b, t, 0)),
        out_shape=jax.ShapeDtypeStruct(x.shape, F32),
        scratch_shapes=[pltpu.VMEM((tiles, tq, seq), BF16),
                        pltpu.VMEM((seq, FNET_WIDTH), BF16),
                        pltpu.VMEM((nh, V_DIM, tq), F32),
                        pltpu.VMEM((AHEAD, SUBLANES, tq), F32)]
                       + [pltpu.VMEM((lc + seq, tq), F32)] * AHEAD,
        compiler_params=pltpu.CompilerParams(dimension_semantics=("parallel", "arbitrary"),
                                             vmem_limit_bytes=VMEM_LIMIT),
        name="cd_main",
    )(qt, qt, kc, kl, vtc, vtl, xcs, mirror, a_cos, a_sin, b_cos, b_sin, gf, gd, x, mod, fnet_w, w_out, ln_g, ln_b)


def _rope_tables(seq):
    pos = np.arange(seq)
    row = (pos // GRID_W).astype(np.float64)
    col = (pos % GRID_W).astype(np.float64)
    half = ROPE_AXIS // 2
    inv = (ROPE_BASE ** (-np.arange(0, ROPE_AXIS, 2, dtype=np.float32) / ROPE_AXIS)).astype(np.float32)
    cos = np.ones((seq, LANES), np.float32)
    sin = np.zeros((seq, LANES), np.float32)
    for j in range(QK_ROPE):
        coord = row if j < ROPE_AXIS else col
        ang = (coord.astype(np.float32) * inv[j % half]).astype(np.float32)
        cos[:, QK_NOPE + j] = np.cos(ang)
        sign = -1.0 if (j % ROPE_AXIS) < half else 1.0
        sin[:, QK_NOPE + j] = sign * np.sin(ang)
    return cos, sin


def _rope_partner(j):
    half = ROPE_AXIS // 2
    return j + half if (j % ROPE_AXIS) < half else j - half


def _dft_tables(seq):
    l = np.arange(seq // 2, dtype=np.float64)
    k1 = np.arange(seq // DFT_FINE, dtype=np.float64)[:, None]
    k0 = np.arange(DFT_FINE, dtype=np.float64)[:, None]
    ang_a = 2.0 * np.pi * ((k1 * DFT_FINE * l) % seq) / seq
    ang_b = 2.0 * np.pi * ((k0 * l) % seq) / seq
    norm = 1.0 / np.sqrt(seq * GROUP_DIM)
    b_sin = np.sin(ang_b) * norm
    b_sin[:, 0] = np.where(np.arange(DFT_FINE) % 2 == 0, norm, -norm)
    return (np.cos(ang_a).astype(np.float32), np.sin(ang_a).astype(np.float32),
            (np.cos(ang_b) * norm).astype(np.float32), b_sin.astype(np.float32))


def _mirror_matrix():
    m = np.zeros((FOLD_BLOCK, 2 * FOLD_BLOCK), np.float32)
    i = np.arange(1, FOLD_BLOCK)
    m[i, FOLD_BLOCK - i] = 1.0
    m[0, FOLD_BLOCK] = 1.0
    return m


def _channel_dft():
    c = np.arange(GROUP_DIM, dtype=np.float64)
    ang = 2.0 * np.pi * ((c[:, None] * c[None, :]) % GROUP_DIM) / GROUP_DIM
    return np.concatenate([np.cos(ang), -np.sin(ang)], axis=1).astype(np.float32)


def kernel(x, c, ctx, c_ctx, ab_w_mod, ab_b_mod, ab_w_in, ab_pool_w, ab_pool_scale, ab_sgu_w, ab_sgu_b,
           ab_w_out, ab_ln_g, ab_ln_b, cd_w_mod, cd_b_mod, cd_w_in, cd_fnet_w, cd_q_norm, cd_kv_norm,
           cd_w_q_up, cd_w_kv_up, cd_w_out, cd_ln_g, cd_ln_b):
    bsz, seq, d = x.shape
    nh, hp = MLA_HEADS, HEAD_PAD
    ctx_len = ctx.shape[1]
    assert d == D_MODEL and seq % ROW_TILE == 0 and seq % (2 * FOLD_BLOCK) == 0 and seq % GRID_W == 0
    assert seq % (TILES_PER_STEP * QUERY_TILE) == 0 and QUERY_TILE % DFT_FINE == 0
    assert FNET_WIDTH == len(DFT_AFTER_HEADS) * DFT_BLOCK and (TILES_PER_STEP * MLA_HEADS) % AHEAD == 0
    assert ctx_len % KEY_CHUNK == 0 and ctx_len % OUT_SLAB == 0 and seq % KEY_CHUNK == 0
    assert max(POOL_WINDOWS) // 2 <= SUBLANES <= HALO and MLA_HEADS % AHEAD == 0

    rows = -(-(bsz + 1) // 8) * 8
    cond = jnp.zeros((rows, d), F32).at[:bsz].set(c).at[bsz].set(c_ctx)
    mod_ab = _modulation(cond, ab_w_mod[0], ab_b_mod[0])
    mod_cd = _modulation(cond, cd_w_mod[0], cd_b_mod[0])
    lat_row = lambda b: b
    ctx_row = lambda b: bsz

    ab_args = (ab_w_in[0].astype(BF16), ab_pool_w[0].astype(BF16), ab_pool_scale[0].reshape(1, -1),
               ab_sgu_w[0].astype(BF16),
               jnp.broadcast_to(ab_sgu_b[0][:, :, None], (SGU_HEADS, CHUNK, GROUP_DIM)),
               ab_w_out[0].astype(BF16), ab_ln_g[0].reshape(1, -1), ab_ln_b[0].reshape(1, -1))
    x1 = _ab_layer(x, mod_ab, lat_row, *ab_args, tl=ROW_TILE)
    ctx1 = _ab_layer(ctx, mod_ab, ctx_row, *ab_args, tl=ctx.shape[1])

    w = cd_w_in[0]
    o_cq = 2 * FNET_WIDTH
    o_ckv = o_cq + Q_LORA
    o_kr = o_ckv + KV_LORA
    o_dg = o_kr + QK_ROPE
    partner = np.array([_rope_partner(j) for j in range(QK_ROPE)])
    zeros = lambda n: jnp.zeros((d, n), F32)
    kr_cols = w[:, o_kr:o_dg]
    kr_lanes = jnp.concatenate([zeros(QK_NOPE), kr_cols, zeros(LANES - QK_NOPE - QK_ROPE)], axis=1)
    kr_both = jnp.concatenate([kr_cols[:, partner], zeros(QK_NOPE - QK_ROPE), kr_cols,
                               zeros(LANES - QK_NOPE - QK_ROPE)], axis=1)
    w1 = jnp.concatenate([w[:, 0:o_cq], w[:, o_dg:], w[:, o_cq:o_kr], kr_both], axis=1).astype(BF16)
    w_kv_ctx = jnp.concatenate([w[:, o_ckv:o_kr], kr_lanes], axis=1).astype(BF16)

    wq = cd_w_q_up[0].reshape(Q_LORA, nh, QK_NOPE + QK_ROPE)
    wqt = jnp.concatenate([wq.reshape(Q_LORA, -1), wq[:, :, QK_NOPE + partner].reshape(Q_LORA, -1)],
                          axis=1).T.astype(BF16)

    wkv = cd_w_kv_up[0].reshape(KV_LORA, nh, QK_NOPE + V_DIM)
    wk_pad = jnp.concatenate([wkv[:, :, :QK_NOPE], jnp.zeros((KV_LORA, nh, hp - QK_NOPE), F32)], axis=-1)
    lane = np.arange(hp)
    sel = np.diag(((lane >= QK_NOPE) & (lane < QK_NOPE + QK_ROPE)).astype(np.float32))
    wk = jnp.concatenate([wk_pad.reshape(KV_LORA, nh * hp),
                          jnp.asarray(np.tile(sel, (1, nh)))], axis=0).astype(BF16)
    wvt = wkv[:, :, QK_NOPE:].reshape(KV_LORA, nh * V_DIM).T.astype(BF16)

    q_norm = cd_q_norm[0].reshape(1, -1)
    kv_norm = cd_kv_norm[0].reshape(1, -1)
    cos, sin = _rope_tables(seq)
    rope_lanes = slice(QK_NOPE, QK_NOPE + QK_ROPE)
    cosq_t = (cos[:, rope_lanes] * np.float32(Q_SCALE)).T.copy()
    sinq_t = (sin[:, rope_lanes] * np.float32(Q_SCALE)).T.copy()
    cs = jnp.asarray(_channel_dft()).astype(BF16)

    kc, vtc = _cd_proj_ctx(ctx1, mod_cd, ctx_row, w_kv_ctx, kv_norm, wk, wvt)
    xcs, gf, gd, qt, kl, vtl = _cd_proj(x1, mod_cd, w1, cs, q_norm, wqt, kv_norm, wk, wvt,
                                        jnp.asarray(cosq_t), jnp.asarray(sinq_t),
                                        jnp.asarray(cos), jnp.asarray(sin), tl=ROW_TILE)
    tq = QUERY_TILE
    a_cos, a_sin, b_cos, b_sin = (jnp.asarray(t) for t in _dft_tables(seq))
    mirror = jnp.asarray(_mirror_matrix()).astype(BF16)
    vtc = vtc.reshape(bsz, nh, V_DIM, -1)
    vtl = vtl.reshape(bsz, nh, V_DIM, -1)
    return _cd_main(qt, kc, kl, vtc, vtl, xcs.reshape(bsz, 2 * seq, FNET_WIDTH), mirror,
                    a_cos, a_sin, b_cos, b_sin,
                    gf, gd, x1, mod_cd, cd_fnet_w[0].astype(BF16), cd_w_out[0].astype(BF16),
                    cd_ln_g[0].reshape(1, -1), cd_ln_b[0].reshape(1, -1), tq=tq)
```

```python
import functools

import numpy as np
import jax
import jax.numpy as jnp
from jax import lax
from jax.experimental import pallas as pl
from jax.experimental.pallas import tpu as pltpu

D_MODEL = 1024
GRID_W = 64
EPS = 1e-6
DEPTH = 2
POOL_WINDOWS = (2, 4, 8, 16)
GROUP_DIM = 128
POOL_WIDTH = 512
SGU_HEADS = 4
SGU_WIDTH = 512
CHUNK = 128
FNET_HEADS = 4
FNET_WIDTH = 512
MLA_HEADS = 8
QK_NOPE = 64
QK_ROPE = 32
V_DIM = 64
Q_LORA = 256
KV_LORA = 128
MLA_WIDTH = 512
ROPE_AXIS = QK_ROPE // 2
ROPE_BASE = 10000.0
ALPHA = (2 * DEPTH) ** 0.25
Q_SCALE = float((QK_NOPE + QK_ROPE) ** -0.5 * np.log2(np.e))

LANES = 128
SUBLANES = 8
BF16_SUBLANES = 16
MXU_TILE = 256
V7X_VMEM_BYTES = 64 * 1024 * 1024
VMEM_LIMIT = V7X_VMEM_BYTES * 7 // 8

HALO = BF16_SUBLANES
HEAD_PAD = LANES
OUT_SLAB = MXU_TILE
ROW_TILE = 1024
QUERY_TILE = 256
MOD_COL_TILE = 768

F32 = jnp.float32
BF16 = jnp.bfloat16


def _silu(v):
    half = 0.5 * v
    return half * (1.0 + jnp.tanh(half))


def _dot(a, b):
    return jnp.dot(a, b, preferred_element_type=F32)


def _deep_norm_rows(x, y, gate_scaled, g, b):
    r = x + gate_scaled * y
    mu = jnp.mean(r, axis=-1, keepdims=True)
    rc = r - mu
    var = jnp.mean(rc * rc, axis=-1, keepdims=True)
    return rc * lax.rsqrt(var + EPS / (ALPHA * ALPHA)) * g + b


def _rms_rows(v, g):
    return v * lax.rsqrt(jnp.mean(v * v, axis=-1, keepdims=True) + EPS) * g


def _mod_kernel(c_ref, w_ref, b_ref, o_ref):
    s = _silu(c_ref[...]).astype(BF16)
    o_ref[...] = _dot(s, w_ref[...].astype(BF16)) + b_ref[...]


def _modulation(cond, w_mod, b_mod):
    rows, d = cond.shape
    n = w_mod.shape[1]
    tn = MOD_COL_TILE
    out = pl.pallas_call(
        _mod_kernel,
        grid=(n // tn,),
        in_specs=[pl.BlockSpec((rows, d), lambda j: (0, 0)),
                  pl.BlockSpec((d, tn), lambda j: (0, j)),
                  pl.BlockSpec((1, tn), lambda j: (0, j))],
        out_specs=pl.BlockSpec((rows, tn), lambda j: (0, j)),
        out_shape=jax.ShapeDtypeStruct((rows, n), F32),
        name="modulation",
    )(cond, w_mod, b_mod.reshape(1, n))
    return out.reshape(rows, 1, n)


def _split_mod(mod_ref):
    d = D_MODEL
    return mod_ref[0, :, 0:d], 1.0 + mod_ref[0, :, d:2 * d], mod_ref[0, :, 2 * d:3 * d]


def _ab_kernel(x_ref, xp_ref, xn_ref, mod_ref, win_ref, poolw_ref, pscale_ref, sguw_ref, sgub_ref,
               wout_ref, lng_ref, lnb_ref, o_ref, h_ref, a_ref, mix_ref, *, tl, seq_len):
    t = pl.program_id(1)
    nt = pl.num_programs(1)
    shift, scale1, gate = _split_mod(mod_ref)
    x = x_ref[0]
    h_ref[HALO:HALO + tl, :] = (x * scale1 + shift).astype(BF16)
    h_ref[0:HALO, :] = (xp_ref[0] * scale1 + shift).astype(BF16)
    h_ref[HALO + tl:2 * HALO + tl, :] = (xn_ref[0] * scale1 + shift).astype(BF16)

    a_ext = _dot(h_ref[...], win_ref[:, 0:POOL_WIDTH])
    a_ref[HALO:HALO + tl, :] = a_ext[HALO:HALO + tl]
    a_ref[0:HALO, :] = jnp.where(t > 0, a_ext[0:HALO], 0.0)
    a_ref[HALO + tl:2 * HALO + tl, :] = jnp.where(t < nt - 1, a_ext[HALO + tl:2 * HALO + tl], 0.0)

    base = 2 * POOL_WIDTH
    slabs = [slice(r0, r0 + OUT_SLAB) for r0 in range(0, tl, OUT_SLAB)]

    def wide_projections(rows):
        h = h_ref[HALO + rows.start:HALO + rows.stop, :]
        v = _dot(h, win_ref[:, base + SGU_WIDTH:base + 2 * SGU_WIDTH])
        u = _dot(h, win_ref[:, base:base + SGU_WIDTH])
        gate_a = _silu(_dot(h, win_ref[:, POOL_WIDTH:2 * POOL_WIDTH]))
        gate_b = _silu(_dot(h, win_ref[:, base + 2 * SGU_WIDTH:base + 3 * SGU_WIDTH]))
        return v, u, gate_a, gate_b

    def mix_and_project(rows, v, u, gate_a, gate_b):
        n = rows.stop - rows.start
        pos = t * tl + rows.start + lax.broadcasted_iota(jnp.int32, (SUBLANES, LANES), 0)
        for g, w in enumerate(POOL_WINDOWS):
            cols = slice(g * GROUP_DIM, (g + 1) * GROUP_DIM)
            first = HALO + rows.start - w // 2
            s = a_ref[first:first + n, cols]
            for j in range(1, w):
                s = s + a_ref[first + j:first + j + n, cols]
            edges = []
            for r0 in (0, n - SUBLANES):
                p = pos + r0
                count = jnp.minimum(p + (w - w // 2), seq_len) - jnp.maximum(p - w // 2, 0)
                edges.append(s[r0:r0 + SUBLANES] / count.astype(F32))
            pooled = jnp.concatenate([edges[0], s[SUBLANES:n - SUBLANES] * (1.0 / w), edges[1]], axis=0)
            diff = pooled - a_ref[HALO + rows.start:HALO + rows.stop, cols]
            ya = _dot(diff.astype(BF16), poolw_ref[g]) * pscale_ref[:, cols] * gate_a[:, cols]
            mix_ref[rows, cols] = ya.astype(BF16)

        for hd in range(SGU_HEADS):
            cols = slice(hd * GROUP_DIM, (hd + 1) * GROUP_DIM)
            vh = v[:, cols]
            mu = jnp.mean(vh, axis=-1, keepdims=True)
            vc = vh - mu
            var = jnp.mean(vc * vc, axis=-1, keepdims=True)
            vn = (vc * lax.rsqrt(var + EPS)).astype(BF16)
            f = jnp.concatenate(
                [_dot(sguw_ref[hd], vn[ci * CHUNK:(ci + 1) * CHUNK, :]) + sgub_ref[hd]
                 for ci in range(n // CHUNK)], axis=0)
            yb = u[:, cols] * f * gate_b[:, cols]
            mix_ref[rows, POOL_WIDTH + hd * GROUP_DIM:POOL_WIDTH + (hd + 1) * GROUP_DIM] = yb.astype(BF16)

        y = _dot(mix_ref[rows, :], wout_ref[...])
        o_ref[0, rows, :] = _deep_norm_rows(x_ref[0, rows, :], y, gate * (1.0 / ALPHA), lng_ref[...], lnb_ref[...])

    pending = None
    for rows in slabs:
        wide = wide_projections(rows)
        if pending is not None:
            mix_and_project(*pending)
        pending = (rows,) + wide
    mix_and_project(*pending)


def _ab_layer(x, mod, mod_row, w_in, pool_w, pool_scale, sgu_w, sgu_b, w_out, ln_g, ln_b, *, tl):
    bsz, seq, d = x.shape
    nt = seq // tl
    hb = tl // HALO
    last_hb = seq // HALO - 1
    const2 = lambda b, t: (0, 0)
    const3 = lambda b, t: (0, 0, 0)
    kern = functools.partial(_ab_kernel, tl=tl, seq_len=seq)
    return pl.pallas_call(
        kern,
        grid=(bsz, nt),
        in_specs=[
            pl.BlockSpec((1, tl, d), lambda b, t: (b, t, 0)),
            pl.BlockSpec((1, HALO, d), lambda b, t: (b, jnp.maximum(t * hb - 1, 0), 0)),
            pl.BlockSpec((1, HALO, d), lambda b, t: (b, jnp.minimum((t + 1) * hb, last_hb), 0)),
            pl.BlockSpec((1, 1, 3 * d), lambda b, t: (mod_row(b), 0, 0)),
            pl.BlockSpec(w_in.shape, const2),
            pl.BlockSpec(pool_w.shape, const3),
            pl.BlockSpec(pool_scale.shape, const2),
            pl.BlockSpec(sgu_w.shape, const3),
            pl.BlockSpec(sgu_b.shape, const3),
            pl.BlockSpec(w_out.shape, const2),
            pl.BlockSpec(ln_g.shape, const2),
            pl.BlockSpec(ln_b.shape, const2),
        ],
        out_specs=pl.BlockSpec((1, tl, d), lambda b, t: (b, t, 0)),
        out_shape=jax.ShapeDtypeStruct(x.shape, F32),
        scratch_shapes=[pltpu.VMEM((tl + 2 * HALO, d), BF16),
                        pltpu.VMEM((tl + 2 * HALO, POOL_WIDTH), F32),
                        pltpu.VMEM((tl, POOL_WIDTH + SGU_WIDTH), BF16)],
        compiler_params=pltpu.CompilerParams(dimension_semantics=("parallel", "arbitrary"),
                                             vmem_limit_bytes=VMEM_LIMIT),
        name="ab_layer",
    )(x, x, x, mod, w_in, pool_w, pool_scale, sgu_w, sgu_b, w_out, ln_g, ln_b)


def _dot_nt(a, b):
    return lax.dot_general(a, b, (((1,), (1,)), ((), ())), preferred_element_type=F32)


def _store_kv(ckv, kr_lanes, kvnorm_ref, wk_ref, wvt_ref, k_ref, vt_ref, e=0):
    ckv_n = _rms_rows(ckv, kvnorm_ref[...]).astype(BF16)
    hk = jnp.concatenate([ckv_n, kr_lanes.astype(BF16)], axis=-1)
    kk = _dot(hk, wk_ref[...]).astype(BF16)
    for hd in range(MLA_HEADS):
        k_ref[e, hd] = kk[:, hd * HEAD_PAD:(hd + 1) * HEAD_PAD]
    vt_ref[e] = _dot_nt(wvt_ref[...], ckv_n).astype(BF16)


def _cd_proj_ctx_kernel(x_ref, mod_ref, w_ref, kvnorm_ref, wk_ref, wvt_ref, k_ref, vt_ref):
    shift, scale1, _ = _split_mod(mod_ref)
    for e in range(x_ref.shape[0]):
        h = (x_ref[e] * scale1 + shift).astype(BF16)
        z = _dot(h, w_ref[...])
        _store_kv(z[:, 0:KV_LORA], z[:, KV_LORA:KV_LORA + LANES], kvnorm_ref, wk_ref, wvt_ref, k_ref, vt_ref, e)


def _cd_proj_kernel(x_ref, mod_ref, w_ref, cs_ref, qnorm_ref, wqt_ref, kvnorm_ref, wk_ref, wvt_ref,
                    cosqt_ref, sinqt_ref, cosk_ref, sink_ref,
                    xcs_ref, gf_ref, gd_ref, qt_ref, k_ref, vt_ref):
    shift, scale1, _ = _split_mod(mod_ref)
    h = (x_ref[0] * scale1 + shift).astype(BF16)
    fw = FNET_WIDTH
    o = 2 * fw + MLA_WIDTH
    small = _dot(h, w_ref[:, o:o + Q_LORA + KV_LORA + LANES])
    cq = small[:, 0:Q_LORA]
    ckv = small[:, Q_LORA:Q_LORA + KV_LORA]
    kr = small[:, Q_LORA + KV_LORA:Q_LORA + KV_LORA + LANES]
    f_in = _dot(h, w_ref[:, 0:fw]).astype(BF16)
    gf_ref[0] = _silu(_dot(h, w_ref[:, fw:2 * fw])).astype(BF16)
    gd_ref[0] = _silu(_dot(h, w_ref[:, 2 * fw:2 * fw + MLA_WIDTH])).astype(BF16)

    qn = _rms_rows(cq, qnorm_ref[...]).astype(BF16)
    q2t = _dot_nt(wqt_ref[...], qn)
    head_dim = QK_NOPE + QK_ROPE
    swap0 = MLA_HEADS * head_dim
    cosqt, sinqt = cosqt_ref[...], sinqt_ref[...]
    tl = q2t.shape[1]
    pad = jnp.zeros((HEAD_PAD - head_dim, tl), F32)
    for hd in range(MLA_HEADS):
        nope = q2t[hd * head_dim:hd * head_dim + QK_NOPE, :]
        rope = q2t[hd * head_dim + QK_NOPE:(hd + 1) * head_dim, :]
        rope_sw = q2t[swap0 + hd * QK_ROPE:swap0 + (hd + 1) * QK_ROPE, :]
        qt_ref[0, hd] = jnp.concatenate(
            [nope * Q_SCALE, rope * cosqt + rope_sw * sinqt, pad], axis=0).astype(BF16)

    for hd in range(FNET_HEADS):
        cols = slice(hd * GROUP_DIM, (hd + 1) * GROUP_DIM)
        xcs = _dot(f_in[:, cols], cs_ref[...]).astype(BF16)
        xcs_ref[0, 0, :, cols] = xcs[:, 0:GROUP_DIM]
        xcs_ref[0, 1, :, cols] = xcs[:, GROUP_DIM:2 * GROUP_DIM]

    kr_rot = kr * cosk_ref[...] + pltpu.roll(kr, LANES // 2, 1) * sink_ref[...]
    _store_kv(ckv, kr_rot, kvnorm_ref, wk_ref, wvt_ref, k_ref, vt_ref)


def _const_spec(a):
    nd = a.ndim
    return pl.BlockSpec(a.shape, lambda b, t: (0,) * nd)


def _cd_proj_ctx(ctx, mod, mod_row, w_kv, kv_norm, wk, wvt):
    bsz, seq, d = ctx.shape
    hp = HEAD_PAD
    grp = max(g for g in (4, 2, 1) if bsz % g == 0)
    return pl.pallas_call(
        _cd_proj_ctx_kernel,
        grid=(bsz // grp, 1),
        in_specs=[pl.BlockSpec((grp, seq, d), lambda b, t: (b, 0, 0)),
                  pl.BlockSpec((1, 1, 3 * d), lambda b, t: (mod_row(b), 0, 0)),
                  _const_spec(w_kv), _const_spec(kv_norm), _const_spec(wk), _const_spec(wvt)],
        out_specs=[pl.BlockSpec((grp, MLA_HEADS, seq, hp), lambda b, t: (b, 0, 0, 0)),
                   pl.BlockSpec((grp, MLA_WIDTH, seq), lambda b, t: (b, 0, 0))],
        out_shape=[jax.ShapeDtypeStruct((bsz, MLA_HEADS, seq, hp), BF16),
                   jax.ShapeDtypeStruct((bsz, MLA_WIDTH, seq), BF16)],
        compiler_params=pltpu.CompilerParams(dimension_semantics=("parallel", "arbitrary"),
                                             vmem_limit_bytes=VMEM_LIMIT),
        name="cd_proj_ctx",
    )(ctx, mod, w_kv, kv_norm, wk, wvt)


def _cd_proj(x, mod, w1, cs, q_norm, wqt, kv_norm, wk, wvt, cosqt, sinqt, cosk, sink, *, tl):
    bsz, seq, d = x.shape
    hp = HEAD_PAD
    fw = FNET_WIDTH
    tab = pl.BlockSpec((tl, LANES), lambda b, t: (t, 0))
    tab_t = pl.BlockSpec((QK_ROPE, tl), lambda b, t: (0, t))
    return pl.pallas_call(
        _cd_proj_kernel,
        grid=(bsz, seq // tl),
        in_specs=[pl.BlockSpec((1, tl, d), lambda b, t: (b, t, 0)),
                  pl.BlockSpec((1, 1, 3 * d), lambda b, t: (b, 0, 0)),
                  _const_spec(w1), _const_spec(cs), _const_spec(q_norm), _const_spec(wqt),
                  _const_spec(kv_norm), _const_spec(wk), _const_spec(wvt), tab_t, tab_t, tab, tab],
        out_specs=[pl.BlockSpec((1, 2, tl, fw), lambda b, t: (b, 0, t, 0)),
                   pl.BlockSpec((1, tl, fw), lambda b, t: (b, t, 0)),
                   pl.BlockSpec((1, tl, MLA_WIDTH), lambda b, t: (b, t, 0)),
                   pl.BlockSpec((1, MLA_HEADS, hp, tl), lambda b, t: (b, 0, 0, t)),
                   pl.BlockSpec((1, MLA_HEADS, tl, hp), lambda b, t: (b, 0, t, 0)),
                   pl.BlockSpec((1, MLA_WIDTH, tl), lambda b, t: (b, 0, t))],
        out_shape=[jax.ShapeDtypeStruct((bsz, 2, seq, fw), BF16),
                   jax.ShapeDtypeStruct((bsz, seq, fw), BF16),
                   jax.ShapeDtypeStruct((bsz, seq, MLA_WIDTH), BF16),
                   jax.ShapeDtypeStruct((bsz, MLA_HEADS, hp, seq), BF16),
                   jax.ShapeDtypeStruct((bsz, MLA_HEADS, seq, hp), BF16),
                   jax.ShapeDtypeStruct((bsz, MLA_WIDTH, seq), BF16)],
        compiler_params=pltpu.CompilerParams(dimension_semantics=("parallel", "arbitrary"),
                                             vmem_limit_bytes=VMEM_LIMIT),
        name="cd_proj",
    )(x, mod, w1, cs, q_norm, wqt, kv_norm, wk, wvt, cosqt, sinqt, cosk, sink)


DFT_FINE = 32
FOLD_BLOCK = MXU_TILE
DFT_BLOCK = MXU_TILE
DFT_AFTER_HEADS = (1, 3)
KEY_CHUNK = MXU_TILE
AHEAD = 4
TILES_PER_STEP = 2


def _fold_rows(s, op):
    while s.shape[0] > SUBLANES:
        half = s.shape[0] // 2
        s = op(s[:half], s[half:])
    return s


def _fold_sequence(xcs_ref, mirror_ref, fold_ref, seq_len):
    half = seq_len // 2
    blk = FOLD_BLOCK
    n_blk = seq_len // blk
    first_row = lax.broadcasted_iota(jnp.int32, (BF16_SUBLANES, FNET_WIDTH), 0) == 0
    for part, sign in ((0, 1.0), (1, -1.0)):
        base = part * seq_len
        for b in range(half // blk):
            low = xcs_ref[0, base + b * blk:base + (b + 1) * blk, :].astype(F32)
            upper = xcs_ref[0, base + (n_blk - 1 - b) * blk:base + (n_blk - b) * blk, :]
            folded = low + sign * _dot(mirror_ref[...], upper)
            if b > 0 or part == 1:
                src = base + (n_blk - b) * blk if b > 0 else half
                partner = xcs_ref[0, src:src + BF16_SUBLANES, :].astype(F32)
                head = low[0:BF16_SUBLANES] + sign * partner if b > 0 else partner
                folded = jnp.concatenate(
                    [jnp.where(first_row, head, folded[0:BF16_SUBLANES]), folded[BF16_SUBLANES:]], axis=0)
            fold_ref[part * half + b * blk:part * half + (b + 1) * blk, :] = folded.astype(BF16)


def _cd_main_kernel(qt_ref, qtn_ref, kc_ref, kl_ref, vtc_ref, vtl_ref, xcs_ref, mirror_ref, ac_ref, as_ref,
                    bc_ref, bs_ref, gf_ref, gd_ref, x_ref, mod_ref, fw_ref, wout_ref, lng_ref, lnb_ref,
                    o_ref, cls_ref, fold_ref, attnt_ref, m_ref, *s_refs, tq, seq_len):
    lc = kc_ref.shape[2]
    n_chunks = (lc + seq_len) // KEY_CHUNK
    ctx_chunks = lc // KEY_CHUNK
    half = seq_len // 2
    tiles = qt_ref.shape[3] // tq
    n_items = tiles * MLA_HEADS

    def query_of(item):
        tile, hd = divmod(item, MLA_HEADS)
        if tile < tiles:
            return qt_ref[0, hd, :, tile * tq:(tile + 1) * tq]
        return qtn_ref[0, hd]

    def chunk_of(ctx_ref, lat_ref, hd, c, lanes):
        ref, c = (ctx_ref, c) if c < ctx_chunks else (lat_ref, c - ctx_chunks)
        span = slice(c * KEY_CHUNK, (c + 1) * KEY_CHUNK)
        return ref[0, hd, :, span] if lanes else ref[0, hd, span, :]

    def score_chunk(s_ref, item, c, m8):
        hd = item % MLA_HEADS
        s = _dot(chunk_of(kc_ref, kl_ref, hd, c, False), query_of(item))
        s_ref[c * KEY_CHUNK:(c + 1) * KEY_CHUNK, :] = s
        part = _fold_rows(s, jnp.maximum)
        return part if m8 is None else jnp.maximum(m8, part)

    ones_rows = jnp.ones((BF16_SUBLANES, KEY_CHUNK), BF16)

    def prob_chunk(s, hd, c, m, acc):
        p = jnp.exp2(s - m)
        vt_ext = jnp.concatenate([chunk_of(vtc_ref, vtl_ref, hd, c, True), ones_rows], axis=0)
        o_t = _dot(vt_ext, p.astype(BF16))
        return o_t if acc is None else acc + o_t

    def item_step(item, m8):
        s_ref, hd = s_refs[item % AHEAD], item % MLA_HEADS
        m = jnp.max(m8, axis=0, keepdims=True)
        m8, acc = None, None
        for c in range(n_chunks):
            s_old = s_ref[c * KEY_CHUNK:(c + 1) * KEY_CHUNK, :]
            m8 = score_chunk(s_ref, item + AHEAD, c, m8)
            acc = prob_chunk(s_old, hd, c, m, acc)
        attnt_ref[hd] = acc[0:V_DIM] * (1.0 / acc[V_DIM:V_DIM + 1])
        return m8

    @pl.when(pl.program_id(1) == 0)
    def _():
        _fold_sequence(xcs_ref, mirror_ref, fold_ref, seq_len)
        m8s = [None] * AHEAD
        for c in range(n_chunks):
            for item in range(AHEAD):
                m8s[item] = score_chunk(s_refs[item], item, c, m8s[item])
        for item in range(AHEAD):
            m_ref[item] = m8s[item]

    _, _, gate = _split_mod(mod_ref)
    bc, bs = bc_ref[...], bs_ref[...]
    coarse = tq // DFT_FINE
    m8s = [m_ref[slot] for slot in range(AHEAD)]
    for tile in range(tiles):
        span = slice(tile * tq, (tile + 1) * tq)
        for i in range(coarse):
            ac = ac_ref[tile * coarse + i:tile * coarse + i + 1, :]
            asn = as_ref[tile * coarse + i:tile * coarse + i + 1, :]
            rows = slice(i * DFT_FINE, (i + 1) * DFT_FINE)
            cls_ref[tile, rows, 0:half] = (ac * bc - asn * bs).astype(BF16)
            cls_ref[tile, rows, half:2 * half] = (asn * bc + ac * bs).astype(BF16)
        specs, yc = [], None
        for hd in range(MLA_HEADS):
            item = tile * MLA_HEADS + hd
            m8s[item % AHEAD] = item_step(item, m8s[item % AHEAD])
            if hd in DFT_AFTER_HEADS:
                j = DFT_AFTER_HEADS.index(hd)
                cols = slice(j * DFT_BLOCK, (j + 1) * DFT_BLOCK)
                specs.append(_dot(cls_ref[tile], fold_ref[:, cols]).astype(BF16))
            if hd == DFT_AFTER_HEADS[-1] + 1:
                yc = _dot(jnp.concatenate(specs, axis=-1), fw_ref[...]) * gf_ref[0, span, :].astype(F32)
        yd = attnt_ref[...].reshape(MLA_WIDTH, tq).T * gd_ref[0, span, :].astype(F32)
        y = _dot(jnp.concatenate([yc.astype(BF16), yd.astype(BF16)], axis=-1), wout_ref[...])
        o_ref[0, span, :] = _deep_norm_rows(x_ref[0, span, :], y, gate * (1.0 / ALPHA), lng_ref[...], lnb_ref[...])
    assert n_items % AHEAD == 0
    for slot in range(AHEAD):
        m_ref[slot] = m8s[slot]


def _cd_main(qt, kc, kl, vtc, vtl, xcs, mirror, a_cos, a_sin, b_cos, b_sin, gf, gd, x, mod, fnet_w, w_out,
             ln_g, ln_b, *, tq):
    bsz, seq, d = x.shape
    lc = kc.shape[2]
    hp = HEAD_PAD
    nh = MLA_HEADS
    per_b4 = lambda b, t: (b, 0, 0, 0)
    per_b3 = lambda b, t: (b, 0, 0)
    kern = functools.partial(_cd_main_kernel, tq=tq, seq_len=seq)
    tiles = TILES_PER_STEP
    rows = tiles * tq
    coarse = rows // DFT_FINE
    last_tile = seq // tq - 1
    return pl.pallas_call(
        kern,
        grid=(bsz, seq // rows),
        in_specs=[pl.BlockSpec((1, nh, hp, rows), lambda b, t: (b, 0, 0, t)),
                  pl.BlockSpec((1, nh, hp, tq), lambda b, t: (b, 0, 0, jnp.minimum(tiles * (t + 1), last_tile))),
                  pl.BlockSpec((1, nh, lc, hp), per_b4),
                  pl.BlockSpec((1, nh, seq, hp), per_b4),
                  pl.BlockSpec((1, nh, V_DIM, lc), per_b4),
                  pl.BlockSpec((1, nh, V_DIM, seq), per_b4),
                  pl.BlockSpec((1, 2 * seq, FNET_WIDTH), per_b3),
                  _const_spec(mirror),
                  pl.BlockSpec((coarse, seq // 2), lambda b, t: (t, 0)),
                  pl.BlockSpec((coarse, seq // 2), lambda b, t: (t, 0)),
                  _const_spec(b_cos), _const_spec(b_sin),
                  pl.BlockSpec((1, rows, FNET_WIDTH), lambda b, t: (b, t, 0)),
                  pl.BlockSpec((1, rows, MLA_WIDTH), lambda b, t: (b, t, 0)),
                  pl.BlockSpec((1, rows, d), lambda b, t: (b, t, 0)),
                  pl.BlockSpec((1, 1, 3 * d), lambda b, t: (b, 0, 0)),
                  _const_spec(fnet_w), _const_spec(w_out), _const_spec(ln_g), _const_spec(ln_b)],
        out_specs=pl.BlockSpec((1, rows, d), lambda b, t: (b, t, 0)),
        out_shape=jax.ShapeDtypeStruct(x.shape, F32),
        scratch_shapes=[pltpu.VMEM((tiles, tq, seq), BF16),
                        pltpu.VMEM((seq, FNET_WIDTH), BF16),
                        pltpu.VMEM((nh, V_DIM, tq), F32),
                        pltpu.VMEM((AHEAD, SUBLANES, tq), F32)]
                       + [pltpu.VMEM((lc + seq, tq), F32)] * AHEAD,
        compiler_params=pltpu.CompilerParams(dimension_semantics=("parallel", "arbitrary"),
                                             vmem_limit_bytes=VMEM_LIMIT),
        name="cd_main",
    )(qt, qt, kc, kl, vtc, vtl, xcs, mirror, a_cos, a_sin, b_cos, b_sin, gf, gd, x, mod, fnet_w, w_out, ln_g, ln_b)


def _rope_tables(seq):
    pos = np.arange(seq)
    row = (pos // GRID_W).astype(np.float64)
    col = (pos % GRID_W).astype(np.float64)
    half = ROPE_AXIS // 2
    inv = (ROPE_BASE ** (-np.arange(0, ROPE_AXIS, 2, dtype=np.float32) / ROPE_AXIS)).astype(np.float32)
    cos = np.ones((seq, LANES), np.float32)
    sin = np.zeros((seq, LANES), np.float32)
    for j in range(QK_ROPE):
        coord = row if j < ROPE_AXIS else col
        ang = (coord.astype(np.float32) * inv[j % half]).astype(np.float32)
        cos[:, QK_NOPE + j] = np.cos(ang)
        sign = -1.0 if (j % ROPE_AXIS) < half else 1.0
        sin[:, QK_NOPE + j] = sign * np.sin(ang)
    return cos, sin


def _rope_partner(j):
    half = ROPE_AXIS // 2
    return j + half if (j % ROPE_AXIS) < half else j - half


def _dft_tables(seq):
    l = np.arange(seq // 2, dtype=np.float64)
    k1 = np.arange(seq // DFT_FINE, dtype=np.float64)[:, None]
    k0 = np.arange(DFT_FINE, dtype=np.float64)[:, None]
    ang_a = 2.0 * np.pi * ((k1 * DFT_FINE * l) % seq) / seq
    ang_b = 2.0 * np.pi * ((k0 * l) % seq) / seq
    norm = 1.0 / np.sqrt(seq * GROUP_DIM)
    b_sin = np.sin(ang_b) * norm
    b_sin[:, 0] = np.where(np.arange(DFT_FINE) % 2 == 0, norm, -norm)
    return (np.cos(ang_a).astype(np.float32), np.sin(ang_a).astype(np.float32),
            (np.cos(ang_b) * norm).astype(np.float32), b_sin.astype(np.float32))


def _mirror_matrix():
    m = np.zeros((FOLD_BLOCK, FOLD_BLOCK), np.float32)
    i = np.arange(1, FOLD_BLOCK)
    m[i, FOLD_BLOCK - i] = 1.0
    return m


def _channel_dft():
    c = np.arange(GROUP_DIM, dtype=np.float64)
    ang = 2.0 * np.pi * ((c[:, None] * c[None, :]) % GROUP_DIM) / GROUP_DIM
    return np.concatenate([np.cos(ang), -np.sin(ang)], axis=1).astype(np.float32)


def kernel(x, c, ctx, c_ctx, ab_w_mod, ab_b_mod, ab_w_in, ab_pool_w, ab_pool_scale, ab_sgu_w, ab_sgu_b,
           ab_w_out, ab_ln_g, ab_ln_b, cd_w_mod, cd_b_mod, cd_w_in, cd_fnet_w, cd_q_norm, cd_kv_norm,
           cd_w_q_up, cd_w_kv_up, cd_w_out, cd_ln_g, cd_ln_b):
    bsz, seq, d = x.shape
    nh, hp = MLA_HEADS, HEAD_PAD
    ctx_len = ctx.shape[1]
    assert d == D_MODEL and seq % ROW_TILE == 0 and seq % (2 * FOLD_BLOCK) == 0 and seq % GRID_W == 0
    assert seq % (TILES_PER_STEP * QUERY_TILE) == 0 and QUERY_TILE % DFT_FINE == 0
    assert FNET_WIDTH == len(DFT_AFTER_HEADS) * DFT_BLOCK and (TILES_PER_STEP * MLA_HEADS) % AHEAD == 0
    assert ctx_len % KEY_CHUNK == 0 and ctx_len % OUT_SLAB == 0 and seq % KEY_CHUNK == 0
    assert max(POOL_WINDOWS) // 2 <= SUBLANES <= HALO and MLA_HEADS % AHEAD == 0

    rows = -(-(bsz + 1) // 8) * 8
    cond = jnp.zeros((rows, d), F32).at[:bsz].set(c).at[bsz].set(c_ctx)
    mod_ab = _modulation(cond, ab_w_mod[0], ab_b_mod[0])
    mod_cd = _modulation(cond, cd_w_mod[0], cd_b_mod[0])
    lat_row = lambda b: b
    ctx_row = lambda b: bsz

    ab_args = (ab_w_in[0].astype(BF16), ab_pool_w[0].astype(BF16), ab_pool_scale[0].reshape(1, -1),
               ab_sgu_w[0].astype(BF16),
               jnp.broadcast_to(ab_sgu_b[0][:, :, None], (SGU_HEADS, CHUNK, GROUP_DIM)),
               ab_w_out[0].astype(BF16), ab_ln_g[0].reshape(1, -1), ab_ln_b[0].reshape(1, -1))
    x1 = _ab_layer(x, mod_ab, lat_row, *ab_args, tl=ROW_TILE)
    ctx1 = _ab_layer(ctx, mod_ab, ctx_row, *ab_args, tl=ctx.shape[1])

    w = cd_w_in[0]
    o_cq = 2 * FNET_WIDTH
    o_ckv = o_cq + Q_LORA
    o_kr = o_ckv + KV_LORA
    o_dg = o_kr + QK_ROPE
    partner = np.array([_rope_partner(j) for j in range(QK_ROPE)])
    zeros = lambda n: jnp.zeros((d, n), F32)
    kr_cols = w[:, o_kr:o_dg]
    kr_lanes = jnp.concatenate([zeros(QK_NOPE), kr_cols, zeros(LANES - QK_NOPE - QK_ROPE)], axis=1)
    kr_both = jnp.concatenate([kr_cols[:, partner], zeros(QK_NOPE - QK_ROPE), kr_cols,
                               zeros(LANES - QK_NOPE - QK_ROPE)], axis=1)
    w1 = jnp.concatenate([w[:, 0:o_cq], w[:, o_dg:], w[:, o_cq:o_kr], kr_both], axis=1).astype(BF16)
    w_kv_ctx = jnp.concatenate([w[:, o_ckv:o_kr], kr_lanes], axis=1).astype(BF16)

    wq = cd_w_q_up[0].reshape(Q_LORA, nh, QK_NOPE + QK_ROPE)
    wqt = jnp.concatenate([wq.reshape(Q_LORA, -1), wq[:, :, QK_NOPE + partner].reshape(Q_LORA, -1)],
                          axis=1).T.astype(BF16)

    wkv = cd_w_kv_up[0].reshape(KV_LORA, nh, QK_NOPE + V_DIM)
    wk_pad = jnp.concatenate([wkv[:, :, :QK_NOPE], jnp.zeros((KV_LORA, nh, hp - QK_NOPE), F32)], axis=-1)
    lane = np.arange(hp)
    sel = np.diag(((lane >= QK_NOPE) & (lane < QK_NOPE + QK_ROPE)).astype(np.float32))
    wk = jnp.concatenate([wk_pad.reshape(KV_LORA, nh * hp),
                          jnp.asarray(np.tile(sel, (1, nh)))], axis=0).astype(BF16)
    wvt = wkv[:, :, QK_NOPE:].reshape(KV_LORA, nh * V_DIM).T.astype(BF16)

    q_norm = cd_q_norm[0].reshape(1, -1)
    kv_norm = cd_kv_norm[0].reshape(1, -1)
    cos, sin = _rope_tables(seq)
    rope_lanes = slice(QK_NOPE, QK_NOPE + QK_ROPE)
    cosq_t = (cos[:, rope_lanes] * np.float32(Q_SCALE)).T.copy()
    sinq_t = (sin[:, rope_lanes] * np.float32(Q_SCALE)).T.copy()
    cs = jnp.asarray(_channel_dft()).astype(BF16)

    kc, vtc = _cd_proj_ctx(ctx1, mod_cd, ctx_row, w_kv_ctx, kv_norm, wk, wvt)
    xcs, gf, gd, qt, kl, vtl = _cd_proj(x1, mod_cd, w1, cs, q_norm, wqt, kv_norm, wk, wvt,
                                        jnp.asarray(cosq_t), jnp.asarray(sinq_t),
                                        jnp.asarray(cos), jnp.asarray(sin), tl=ROW_TILE)
    tq = QUERY_TILE
    a_cos, a_sin, b_cos, b_sin = (jnp.asarray(t) for t in _dft_tables(seq))
    mirror = jnp.asarray(_mirror_matrix()).astype(BF16)
    vtc = vtc.reshape(bsz, nh, V_DIM, -1)
    vtl = vtl.reshape(bsz, nh, V_DIM, -1)
    return _cd_main(qt, kc, kl, vtc, vtl, xcs.reshape(bsz, 2 * seq, FNET_WIDTH), mirror,
                    a_cos, a_sin, b_cos, b_sin,
                    gf, gd, x1, mod_cd, cd_fnet_w[0].astype(BF16), cd_w_out[0].astype(BF16),
                    cd_ln_g[0].reshape(1, -1), cd_ln_b[0].reshape(1, -1), tq=tq)
```

```python
import functools

import numpy as np
import jax
import jax.numpy as jnp
from jax import lax
from jax.experimental import pallas as pl
from jax.experimental.pallas import tpu as pltpu

D_MODEL = 1024
GRID_W = 64
EPS = 1e-6
DEPTH = 2
POOL_WINDOWS = (2, 4, 8, 16)
GROUP_DIM = 128
POOL_WIDTH = 512
SGU_HEADS = 4
SGU_WIDTH = 512
CHUNK = 128
FNET_HEADS = 4
FNET_WIDTH = 512
MLA_HEADS = 8
QK_NOPE = 64
QK_ROPE = 32
V_DIM = 64
Q_LORA = 256
KV_LORA = 128
MLA_WIDTH = 512
ROPE_AXIS = QK_ROPE // 2
ROPE_BASE = 10000.0
ALPHA = (2 * DEPTH) ** 0.25
Q_SCALE = float((QK_NOPE + QK_ROPE) ** -0.5 * np.log2(np.e))

LANES = 128
SUBLANES = 8
BF16_SUBLANES = 16
MXU_TILE = 256
V7X_VMEM_BYTES = 64 * 1024 * 1024
VMEM_LIMIT = V7X_VMEM_BYTES * 7 // 8

HALO = BF16_SUBLANES
HEAD_PAD = LANES
OUT_SLAB = 2 * MXU_TILE
ROW_TILE = 1024
QUERY_TILE = 256
MOD_COL_TILE = 768

F32 = jnp.float32
BF16 = jnp.bfloat16


def _silu(v):
    half = 0.5 * v
    return half * (1.0 + jnp.tanh(half))


def _dot(a, b):
    return jnp.dot(a, b, preferred_element_type=F32)


def _deep_norm_rows(x, y, gate_scaled, g, b):
    r = x + gate_scaled * y
    mu = jnp.mean(r, axis=-1, keepdims=True)
    rc = r - mu
    var = jnp.mean(rc * rc, axis=-1, keepdims=True)
    return rc * lax.rsqrt(var + EPS / (ALPHA * ALPHA)) * g + b


def _rms_rows(v, g):
    return v * lax.rsqrt(jnp.mean(v * v, axis=-1, keepdims=True) + EPS) * g


def _mod_kernel(c_ref, w_ref, b_ref, o_ref):
    s = _silu(c_ref[...]).astype(BF16)
    o_ref[...] = _dot(s, w_ref[...].astype(BF16)) + b_ref[...]


def _modulation(cond, w_mod, b_mod):
    rows, d = cond.shape
    n = w_mod.shape[1]
    tn = MOD_COL_TILE
    out = pl.pallas_call(
        _mod_kernel,
        grid=(n // tn,),
        in_specs=[pl.BlockSpec((rows, d), lambda j: (0, 0)),
                  pl.BlockSpec((d, tn), lambda j: (0, j)),
                  pl.BlockSpec((1, tn), lambda j: (0, j))],
        out_specs=pl.BlockSpec((rows, tn), lambda j: (0, j)),
        out_shape=jax.ShapeDtypeStruct((rows, n), F32),
        name="modulation",
    )(cond, w_mod, b_mod.reshape(1, n))
    return out.reshape(rows, 1, n)


def _split_mod(mod_ref):
    d = D_MODEL
    return mod_ref[0, :, 0:d], 1.0 + mod_ref[0, :, d:2 * d], mod_ref[0, :, 2 * d:3 * d]


def _ab_kernel(x_ref, xp_ref, xn_ref, mod_ref, win_ref, poolw_ref, pscale_ref, sguw_ref, sgub_ref,
               wout_ref, lng_ref, lnb_ref, o_ref, h_ref, a_ref, mix_ref, *, tl, seq_len):
    t = pl.program_id(1)
    nt = pl.num_programs(1)
    shift, scale1, gate = _split_mod(mod_ref)
    x = x_ref[0]
    h_ref[HALO:HALO + tl, :] = (x * scale1 + shift).astype(BF16)
    h_ref[0:HALO, :] = (xp_ref[0] * scale1 + shift).astype(BF16)
    h_ref[HALO + tl:2 * HALO + tl, :] = (xn_ref[0] * scale1 + shift).astype(BF16)

    a_ext = _dot(h_ref[...], win_ref[:, 0:POOL_WIDTH])
    a_ref[HALO:HALO + tl, :] = a_ext[HALO:HALO + tl]
    a_ref[0:HALO, :] = jnp.where(t > 0, a_ext[0:HALO], 0.0)
    a_ref[HALO + tl:2 * HALO + tl, :] = jnp.where(t < nt - 1, a_ext[HALO + tl:2 * HALO + tl], 0.0)

    base = 2 * POOL_WIDTH
    slab = min(OUT_SLAB, tl)
    slabs = [slice(r0, r0 + slab) for r0 in range(0, tl, slab)]

    def wide_projections(rows):
        h = h_ref[HALO + rows.start:HALO + rows.stop, :]
        v = _dot(h, win_ref[:, base + SGU_WIDTH:base + 2 * SGU_WIDTH])
        u = _dot(h, win_ref[:, base:base + SGU_WIDTH])
        gate_a = _silu(_dot(h, win_ref[:, POOL_WIDTH:2 * POOL_WIDTH]))
        gate_b = _silu(_dot(h, win_ref[:, base + 2 * SGU_WIDTH:base + 3 * SGU_WIDTH]))
        return v, u, gate_a, gate_b

    def mix_and_project(rows, v, u, gate_a, gate_b):
        n = rows.stop - rows.start
        pos = t * tl + rows.start + lax.broadcasted_iota(jnp.int32, (SUBLANES, LANES), 0)
        for g, w in enumerate(POOL_WINDOWS):
            cols = slice(g * GROUP_DIM, (g + 1) * GROUP_DIM)
            first = HALO + rows.start - w // 2
            s = a_ref[first:first + n, cols]
            for j in range(1, w):
                s = s + a_ref[first + j:first + j + n, cols]
            edges = []
            for r0 in (0, n - SUBLANES):
                p = pos + r0
                count = jnp.minimum(p + (w - w // 2), seq_len) - jnp.maximum(p - w // 2, 0)
                edges.append(s[r0:r0 + SUBLANES] / count.astype(F32))
            pooled = jnp.concatenate([edges[0], s[SUBLANES:n - SUBLANES] * (1.0 / w), edges[1]], axis=0)
            diff = pooled - a_ref[HALO + rows.start:HALO + rows.stop, cols]
            ya = _dot(diff.astype(BF16), poolw_ref[g]) * pscale_ref[:, cols] * gate_a[:, cols]
            mix_ref[rows, cols] = ya.astype(BF16)

        for hd in range(SGU_HEADS):
            cols = slice(hd * GROUP_DIM, (hd + 1) * GROUP_DIM)
            vh = v[:, cols]
            mu = jnp.mean(vh, axis=-1, keepdims=True)
            vc = vh - mu
            var = jnp.mean(vc * vc, axis=-1, keepdims=True)
            vn = (vc * lax.rsqrt(var + EPS)).astype(BF16)
            f = jnp.concatenate(
                [_dot(sguw_ref[hd], vn[ci * CHUNK:(ci + 1) * CHUNK, :]) + sgub_ref[hd]
                 for ci in range(n // CHUNK)], axis=0)
            yb = u[:, cols] * f * gate_b[:, cols]
            mix_ref[rows, POOL_WIDTH + hd * GROUP_DIM:POOL_WIDTH + (hd + 1) * GROUP_DIM] = yb.astype(BF16)

        y = _dot(mix_ref[rows, :], wout_ref[...])
        o_ref[0, rows, :] = _deep_norm_rows(x_ref[0, rows, :], y, gate * (1.0 / ALPHA), lng_ref[...], lnb_ref[...])

    pending = None
    for rows in slabs:
        wide = wide_projections(rows)
        if pending is not None:
            mix_and_project(*pending)
        pending = (rows,) + wide
    mix_and_project(*pending)


def _ab_layer(x, mod, mod_row, w_in, pool_w, pool_scale, sgu_w, sgu_b, w_out, ln_g, ln_b, *, tl):
    bsz, seq, d = x.shape
    nt = seq // tl
    hb = tl // HALO
    last_hb = seq // HALO - 1
    const2 = lambda b, t: (0, 0)
    const3 = lambda b, t: (0, 0, 0)
    kern = functools.partial(_ab_kernel, tl=tl, seq_len=seq)
    return pl.pallas_call(
        kern,
        grid=(bsz, nt),
        in_specs=[
            pl.BlockSpec((1, tl, d), lambda b, t: (b, t, 0)),
            pl.BlockSpec((1, HALO, d), lambda b, t: (b, jnp.maximum(t * hb - 1, 0), 0)),
            pl.BlockSpec((1, HALO, d), lambda b, t: (b, jnp.minimum((t + 1) * hb, last_hb), 0)),
            pl.BlockSpec((1, 1, 3 * d), lambda b, t: (mod_row(b), 0, 0)),
            pl.BlockSpec(w_in.shape, const2),
            pl.BlockSpec(pool_w.shape, const3),
            pl.BlockSpec(pool_scale.shape, const2),
            pl.BlockSpec(sgu_w.shape, const3),
            pl.BlockSpec(sgu_b.shape, const3),
            pl.BlockSpec(w_out.shape, const2),
            pl.BlockSpec(ln_g.shape, const2),
            pl.BlockSpec(ln_b.shape, const2),
        ],
        out_specs=pl.BlockSpec((1, tl, d), lambda b, t: (b, t, 0)),
        out_shape=jax.ShapeDtypeStruct(x.shape, F32),
        scratch_shapes=[pltpu.VMEM((tl + 2 * HALO, d), BF16),
                        pltpu.VMEM((tl + 2 * HALO, POOL_WIDTH), F32),
                        pltpu.VMEM((tl, POOL_WIDTH + SGU_WIDTH), BF16)],
        compiler_params=pltpu.CompilerParams(dimension_semantics=("parallel", "arbitrary"),
                                             vmem_limit_bytes=VMEM_LIMIT),
        name="ab_layer",
    )(x, x, x, mod, w_in, pool_w, pool_scale, sgu_w, sgu_b, w_out, ln_g, ln_b)


def _dot_nt(a, b):
    return lax.dot_general(a, b, (((1,), (1,)), ((), ())), preferred_element_type=F32)


def _store_kv(ckv, kr_lanes, kvnorm_ref, wk_ref, wvt_ref, k_ref, vt_ref, e=0):
    ckv_n = _rms_rows(ckv, kvnorm_ref[...]).astype(BF16)
    hk = jnp.concatenate([ckv_n, kr_lanes.astype(BF16)], axis=-1)
    kk = _dot(hk, wk_ref[...]).astype(BF16)
    for hd in range(MLA_HEADS):
        k_ref[e, hd] = kk[:, hd * HEAD_PAD:(hd + 1) * HEAD_PAD]
    vt_ref[e] = _dot_nt(wvt_ref[...], ckv_n).astype(BF16)


def _cd_proj_ctx_kernel(x_ref, mod_ref, w_ref, kvnorm_ref, wk_ref, wvt_ref, k_ref, vt_ref):
    shift, scale1, _ = _split_mod(mod_ref)
    for e in range(x_ref.shape[0]):
        h = (x_ref[e] * scale1 + shift).astype(BF16)
        z = _dot(h, w_ref[...])
        _store_kv(z[:, 0:KV_LORA], z[:, KV_LORA:KV_LORA + LANES], kvnorm_ref, wk_ref, wvt_ref, k_ref, vt_ref, e)


def _cd_proj_kernel(x_ref, mod_ref, w_ref, cs_ref, qnorm_ref, wqt_ref, kvnorm_ref, wk_ref, wvt_ref,
                    cosqt_ref, sinqt_ref, cosk_ref, sink_ref,
                    xcs_ref, gf_ref, gd_ref, qt_ref, k_ref, vt_ref):
    shift, scale1, _ = _split_mod(mod_ref)
    h = (x_ref[0] * scale1 + shift).astype(BF16)
    fw = FNET_WIDTH
    o = 2 * fw + MLA_WIDTH
    small = _dot(h, w_ref[:, o:o + Q_LORA + KV_LORA + LANES])
    cq = small[:, 0:Q_LORA]
    ckv = small[:, Q_LORA:Q_LORA + KV_LORA]
    kr = small[:, Q_LORA + KV_LORA:Q_LORA + KV_LORA + LANES]
    f_in = _dot(h, w_ref[:, 0:fw]).astype(BF16)
    gf_ref[0] = _silu(_dot(h, w_ref[:, fw:2 * fw])).astype(BF16)
    gd_ref[0] = _silu(_dot(h, w_ref[:, 2 * fw:2 * fw + MLA_WIDTH])).astype(BF16)

    qn = _rms_rows(cq, qnorm_ref[...]).astype(BF16)
    q2t = _dot_nt(wqt_ref[...], qn)
    head_dim = QK_NOPE + QK_ROPE
    swap0 = MLA_HEADS * head_dim
    cosqt, sinqt = cosqt_ref[...], sinqt_ref[...]
    tl = q2t.shape[1]
    pad = jnp.zeros((HEAD_PAD - head_dim, tl), F32)
    for hd in range(MLA_HEADS):
        nope = q2t[hd * head_dim:hd * head_dim + QK_NOPE, :]
        rope = q2t[hd * head_dim + QK_NOPE:(hd + 1) * head_dim, :]
        rope_sw = q2t[swap0 + hd * QK_ROPE:swap0 + (hd + 1) * QK_ROPE, :]
        qt_ref[0, hd] = jnp.concatenate(
            [nope * Q_SCALE, rope * cosqt + rope_sw * sinqt, pad], axis=0).astype(BF16)

    for hd in range(FNET_HEADS):
        cols = slice(hd * GROUP_DIM, (hd + 1) * GROUP_DIM)
        xcs = _dot(f_in[:, cols], cs_ref[...]).astype(BF16)
        xcs_ref[0, 0, :, cols] = xcs[:, 0:GROUP_DIM]
        xcs_ref[0, 1, :, cols] = xcs[:, GROUP_DIM:2 * GROUP_DIM]

    kr_rot = kr * cosk_ref[...] + pltpu.roll(kr, LANES // 2, 1) * sink_ref[...]
    _store_kv(ckv, kr_rot, kvnorm_ref, wk_ref, wvt_ref, k_ref, vt_ref)


def _const_spec(a):
    nd = a.ndim
    return pl.BlockSpec(a.shape, lambda b, t: (0,) * nd)


def _cd_proj_ctx(ctx, mod, mod_row, w_kv, kv_norm, wk, wvt):
    bsz, seq, d = ctx.shape
    hp = HEAD_PAD
    grp = max(g for g in (4, 2, 1) if bsz % g == 0)
    return pl.pallas_call(
        _cd_proj_ctx_kernel,
        grid=(bsz // grp, 1),
        in_specs=[pl.BlockSpec((grp, seq, d), lambda b, t: (b, 0, 0)),
                  pl.BlockSpec((1, 1, 3 * d), lambda b, t: (mod_row(b), 0, 0)),
                  _const_spec(w_kv), _const_spec(kv_norm), _const_spec(wk), _const_spec(wvt)],
        out_specs=[pl.BlockSpec((grp, MLA_HEADS, seq, hp), lambda b, t: (b, 0, 0, 0)),
                   pl.BlockSpec((grp, MLA_WIDTH, seq), lambda b, t: (b, 0, 0))],
        out_shape=[jax.ShapeDtypeStruct((bsz, MLA_HEADS, seq, hp), BF16),
                   jax.ShapeDtypeStruct((bsz, MLA_WIDTH, seq), BF16)],
        compiler_params=pltpu.CompilerParams(dimension_semantics=("parallel", "arbitrary"),
                                             vmem_limit_bytes=VMEM_LIMIT),
        name="cd_proj_ctx",
    )(ctx, mod, w_kv, kv_norm, wk, wvt)


def _cd_proj(x, mod, w1, cs, q_norm, wqt, kv_norm, wk, wvt, cosqt, sinqt, cosk, sink, *, tl):
    bsz, seq, d = x.shape
    hp = HEAD_PAD
    fw = FNET_WIDTH
    tab = pl.BlockSpec((tl, LANES), lambda b, t: (t, 0))
    tab_t = pl.BlockSpec((QK_ROPE, tl), lambda b, t: (0, t))
    return pl.pallas_call(
        _cd_proj_kernel,
        grid=(bsz, seq // tl),
        in_specs=[pl.BlockSpec((1, tl, d), lambda b, t: (b, t, 0)),
                  pl.BlockSpec((1, 1, 3 * d), lambda b, t: (b, 0, 0)),
                  _const_spec(w1), _const_spec(cs), _const_spec(q_norm), _const_spec(wqt),
                  _const_spec(kv_norm), _const_spec(wk), _const_spec(wvt), tab_t, tab_t, tab, tab],
        out_specs=[pl.BlockSpec((1, 2, tl, fw), lambda b, t: (b, 0, t, 0)),
                   pl.BlockSpec((1, tl, fw), lambda b, t: (b, t, 0)),
                   pl.BlockSpec((1, tl, MLA_WIDTH), lambda b, t: (b, t, 0)),
                   pl.BlockSpec((1, MLA_HEADS, hp, tl), lambda b, t: (b, 0, 0, t)),
                   pl.BlockSpec((1, MLA_HEADS, tl, hp), lambda b, t: (b, 0, t, 0)),
                   pl.BlockSpec((1, MLA_WIDTH, tl), lambda b, t: (b, 0, t))],
        out_shape=[jax.ShapeDtypeStruct((bsz, 2, seq, fw), BF16),
                   jax.ShapeDtypeStruct((bsz, seq, fw), BF16),
                   jax.ShapeDtypeStruct((bsz, seq, MLA_WIDTH), BF16),
                   jax.ShapeDtypeStruct((bsz, MLA_HEADS, hp, seq), BF16),
                   jax.ShapeDtypeStruct((bsz, MLA_HEADS, seq, hp), BF16),
                   jax.ShapeDtypeStruct((bsz, MLA_WIDTH, seq), BF16)],
        compiler_params=pltpu.CompilerParams(dimension_semantics=("parallel", "arbitrary"),
                                             vmem_limit_bytes=VMEM_LIMIT),
        name="cd_proj",
    )(x, mod, w1, cs, q_norm, wqt, kv_norm, wk, wvt, cosqt, sinqt, cosk, sink)


DFT_FINE = 32
FOLD_BLOCK = MXU_TILE
DFT_BLOCK = MXU_TILE
DFT_AFTER_HEADS = (1, 3)
KEY_CHUNK = MXU_TILE
AHEAD = 4
TILES_PER_STEP = 2


def _fold_rows(s, op):
    while s.shape[0] > SUBLANES:
        half = s.shape[0] // 2
        s = op(s[:half], s[half:])
    return s


def _fold_sequence(xcs_ref, mirror_ref, fold_ref, seq_len):
    half = seq_len // 2
    blk = FOLD_BLOCK
    n_blk = seq_len // blk
    first_row = lax.broadcasted_iota(jnp.int32, (BF16_SUBLANES, FNET_WIDTH), 0) == 0
    for part, sign in ((0, 1.0), (1, -1.0)):
        base = part * seq_len
        for b in range(half // blk):
            low = xcs_ref[0, base + b * blk:base + (b + 1) * blk, :].astype(F32)
            upper = xcs_ref[0, base + (n_blk - 1 - b) * blk:base + (n_blk - b) * blk, :]
            folded = low + sign * _dot(mirror_ref[...], upper)
            if b > 0 or part == 1:
                src = base + (n_blk - b) * blk if b > 0 else half
                partner = xcs_ref[0, src:src + BF16_SUBLANES, :].astype(F32)
                head = low[0:BF16_SUBLANES] + sign * partner if b > 0 else partner
                folded = jnp.concatenate(
                    [jnp.where(first_row, head, folded[0:BF16_SUBLANES]), folded[BF16_SUBLANES:]], axis=0)
            fold_ref[part * half + b * blk:part * half + (b + 1) * blk, :] = folded.astype(BF16)


def _cd_main_kernel(qt_ref, qtn_ref, kc_ref, kl_ref, vtc_ref, vtl_ref, xcs_ref, mirror_ref, ac_ref, as_ref,
                    bc_ref, bs_ref, gf_ref, gd_ref, x_ref, mod_ref, fw_ref, wout_ref, lng_ref, lnb_ref,
                    o_ref, cls_ref, fold_ref, attnt_ref, m_ref, *s_refs, tq, seq_len):
    lc = kc_ref.shape[2]
    n_chunks = (lc + seq_len) // KEY_CHUNK
    ctx_chunks = lc // KEY_CHUNK
    half = seq_len // 2
    tiles = qt_ref.shape[3] // tq
    n_items = tiles * MLA_HEADS

    def query_of(item):
        tile, hd = divmod(item, MLA_HEADS)
        if tile < tiles:
            return qt_ref[0, hd, :, tile * tq:(tile + 1) * tq]
        return qtn_ref[0, hd]

    def chunk_of(ctx_ref, lat_ref, hd, c, lanes):
        ref, c = (ctx_ref, c) if c < ctx_chunks else (lat_ref, c - ctx_chunks)
        span = slice(c * KEY_CHUNK, (c + 1) * KEY_CHUNK)
        return ref[0, hd, :, span] if lanes else ref[0, hd, span, :]

    def score_chunk(s_ref, item, c, m8):
        hd = item % MLA_HEADS
        s = _dot(chunk_of(kc_ref, kl_ref, hd, c, False), query_of(item))
        s_ref[c * KEY_CHUNK:(c + 1) * KEY_CHUNK, :] = s
        part = _fold_rows(s, jnp.maximum)
        return part if m8 is None else jnp.maximum(m8, part)

    ones_rows = jnp.ones((BF16_SUBLANES, KEY_CHUNK), BF16)

    def prob_chunk(s, hd, c, m, acc):
        p = jnp.exp2(s - m)
        vt_ext = jnp.concatenate([chunk_of(vtc_ref, vtl_ref, hd, c, True), ones_rows], axis=0)
        o_t = _dot(vt_ext, p.astype(BF16))
        return o_t if acc is None else acc + o_t

    def item_step(item, m8):
        s_ref, hd = s_refs[item % AHEAD], item % MLA_HEADS
        m = jnp.max(m8, axis=0, keepdims=True)
        m8, acc = None, None
        for c in range(n_chunks):
            s_old = s_ref[c * KEY_CHUNK:(c + 1) * KEY_CHUNK, :]
            m8 = score_chunk(s_ref, item + AHEAD, c, m8)
            acc = prob_chunk(s_old, hd, c, m, acc)
        attnt_ref[hd] = acc[0:V_DIM] * (1.0 / acc[V_DIM:V_DIM + 1])
        return m8

    @pl.when(pl.program_id(1) == 0)
    def _():
        _fold_sequence(xcs_ref, mirror_ref, fold_ref, seq_len)
        m8s = [None] * AHEAD
        for c in range(n_chunks):
            for item in range(AHEAD):
                m8s[item] = score_chunk(s_refs[item], item, c, m8s[item])
        for item in range(AHEAD):
            m_ref[item] = m8s[item]

    _, _, gate = _split_mod(mod_ref)
    bc, bs = bc_ref[...], bs_ref[...]
    coarse = tq // DFT_FINE
    m8s = [m_ref[slot] for slot in range(AHEAD)]
    for tile in range(tiles):
        span = slice(tile * tq, (tile + 1) * tq)
        for i in range(coarse):
            ac = ac_ref[tile * coarse + i:tile * coarse + i + 1, :]
            asn = as_ref[tile * coarse + i:tile * coarse + i + 1, :]
            rows = slice(i * DFT_FINE, (i + 1) * DFT_FINE)
            cls_ref[tile, rows, 0:half] = (ac * bc - asn * bs).astype(BF16)
            cls_ref[tile, rows, half:2 * half] = (asn * bc + ac * bs).astype(BF16)
        specs, yc = [], None
        for hd in range(MLA_HEADS):
            item = tile * MLA_HEADS + hd
            m8s[item % AHEAD] = item_step(item, m8s[item % AHEAD])
            if hd in DFT_AFTER_HEADS:
                j = DFT_AFTER_HEADS.index(hd)
                cols = slice(j * DFT_BLOCK, (j + 1) * DFT_BLOCK)
                specs.append(_dot(cls_ref[tile], fold_ref[:, cols]).astype(BF16))
            if hd == DFT_AFTER_HEADS[-1] + 1:
                yc = _dot(jnp.concatenate(specs, axis=-1), fw_ref[...]) * gf_ref[0, span, :].astype(F32)
        yd = attnt_ref[...].reshape(MLA_WIDTH, tq).T * gd_ref[0, span, :].astype(F32)
        y = _dot(jnp.concatenate([yc.astype(BF16), yd.astype(BF16)], axis=-1), wout_ref[...])
        o_ref[0, span, :] = _deep_norm_rows(x_ref[0, span, :], y, gate * (1.0 / ALPHA), lng_ref[...], lnb_ref[...])
    assert n_items % AHEAD == 0
    for slot in range(AHEAD):
        m_ref[slot] = m8s[slot]


def _cd_main(qt, kc, kl, vtc, vtl, xcs, mirror, a_cos, a_sin, b_cos, b_sin, gf, gd, x, mod, fnet_w, w_out,
             ln_g, ln_b, *, tq):
    bsz, seq, d = x.shape
    lc = kc.shape[2]
    hp = HEAD_PAD
    nh = MLA_HEADS
    per_b4 = lambda b, t: (b, 0, 0, 0)
    per_b3 = lambda b, t: (b, 0, 0)
    kern = functools.partial(_cd_main_kernel, tq=tq, seq_len=seq)
    tiles = TILES_PER_STEP
    rows = tiles * tq
    coarse = rows // DFT_FINE
    last_tile = seq // tq - 1
    return pl.pallas_call(
        kern,
        grid=(bsz, seq // rows),
        in_specs=[pl.BlockSpec((1, nh, hp, rows), lambda b, t: (b, 0, 0, t)),
                  pl.BlockSpec((1, nh, hp, tq), lambda b, t: (b, 0, 0, jnp.minimum(tiles * (t + 1), last_tile))),
                  pl.BlockSpec((1, nh, lc, hp), per_b4),
                  pl.BlockSpec((1, nh, seq, hp), per_b4),
                  pl.BlockSpec((1, nh, V_DIM, lc), per_b4),
                  pl.BlockSpec((1, nh, V_DIM, seq), per_b4),
                  pl.BlockSpec((1, 2 * seq, FNET_WIDTH), per_b3),
                  _const_spec(mirror),
                  pl.BlockSpec((coarse, seq // 2), lambda b, t: (t, 0)),
                  pl.BlockSpec((coarse, seq // 2), lambda b, t: (t, 0)),
                  _const_spec(b_cos), _const_spec(b_sin),
                  pl.BlockSpec((1, rows, FNET_WIDTH), lambda b, t: (b, t, 0)),
                  pl.BlockSpec((1, rows, MLA_WIDTH), lambda b, t: (b, t, 0)),
                  pl.BlockSpec((1, rows, d), lambda b, t: (b, t, 0)),
                  pl.BlockSpec((1, 1, 3 * d), lambda b, t: (b, 0, 0)),
                  _const_spec(fnet_w), _const_spec(w_out), _const_spec(ln_g), _const_spec(ln_b)],
        out_specs=pl.BlockSpec((1, rows, d), lambda b, t: (b, t, 0)),
        out_shape=jax.ShapeDtypeStruct(x.shape, F32),
        scratch_shapes=[pltpu.VMEM((tiles, tq, seq), BF16),
                        pltpu.VMEM((seq, FNET_WIDTH), BF16),
                        pltpu.VMEM((nh, V_DIM, tq), F32),
                        pltpu.VMEM((AHEAD, SUBLANES, tq), F32)]
                       + [pltpu.VMEM((lc + seq, tq), F32)] * AHEAD,
        compiler_params=pltpu.CompilerParams(dimension_semantics=("parallel", "arbitrary"),
                                             vmem_limit_bytes=VMEM_LIMIT),
        name="cd_main",
    )(qt, qt, kc, kl, vtc, vtl, xcs, mirror, a_cos, a_sin, b_cos, b_sin, gf, gd, x, mod, fnet_w, w_out, ln_g, ln_b)


def _rope_tables(seq):
    pos = np.arange(seq)
    row = (pos // GRID_W).astype(np.float64)
    col = (pos % GRID_W).astype(np.float64)
    half = ROPE_AXIS // 2
    inv = (ROPE_BASE ** (-np.arange(0, ROPE_AXIS, 2, dtype=np.float32) / ROPE_AXIS)).astype(np.float32)
    cos = np.ones((seq, LANES), np.float32)
    sin = np.zeros((seq, LANES), np.float32)
    for j in range(QK_ROPE):
        coord = row if j < ROPE_AXIS else col
        ang = (coord.astype(np.float32) * inv[j % half]).astype(np.float32)
        cos[:, QK_NOPE + j] = np.cos(ang)
        sign = -1.0 if (j % ROPE_AXIS) < half else 1.0
        sin[:, QK_NOPE + j] = sign * np.sin(ang)
    return cos, sin


def _rope_partner(j):
    half = ROPE_AXIS // 2
    return j + half if (j % ROPE_AXIS) < half else j - half


def _dft_tables(seq):
    l = np.arange(seq // 2, dtype=np.float64)
    k1 = np.arange(seq // DFT_FINE, dtype=np.float64)[:, None]
    k0 = np.arange(DFT_FINE, dtype=np.float64)[:, None]
    ang_a = 2.0 * np.pi * ((k1 * DFT_FINE * l) % seq) / seq
    ang_b = 2.0 * np.pi * ((k0 * l) % seq) / seq
    norm = 1.0 / np.sqrt(seq * GROUP_DIM)
    b_sin = np.sin(ang_b) * norm
    b_sin[:, 0] = np.where(np.arange(DFT_FINE) % 2 == 0, norm, -norm)
    return (np.cos(ang_a).astype(np.float32), np.sin(ang_a).astype(np.float32),
            (np.cos(ang_b) * norm).astype(np.float32), b_sin.astype(np.float32))


def _mirror_matrix():
    m = np.zeros((FOLD_BLOCK, FOLD_BLOCK), np.float32)
    i = np.arange(1, FOLD_BLOCK)
    m[i, FOLD_BLOCK - i] = 1.0
    return m


def _channel_dft():
    c = np.arange(GROUP_DIM, dtype=np.float64)
    ang = 2.0 * np.pi * ((c[:, None] * c[None, :]) % GROUP_DIM) / GROUP_DIM
    return np.concatenate([np.cos(ang), -np.sin(ang)], axis=1).astype(np.float32)


def kernel(x, c, ctx, c_ctx, ab_w_mod, ab_b_mod, ab_w_in, ab_pool_w, ab_pool_scale, ab_sgu_w, ab_sgu_b,
           ab_w_out, ab_ln_g, ab_ln_b, cd_w_mod, cd_b_mod, cd_w_in, cd_fnet_w, cd_q_norm, cd_kv_norm,
           cd_w_q_up, cd_w_kv_up, cd_w_out, cd_ln_g, cd_ln_b):
    bsz, seq, d = x.shape
    nh, hp = MLA_HEADS, HEAD_PAD
    ctx_len = ctx.shape[1]
    assert d == D_MODEL and seq % ROW_TILE == 0 and seq % (2 * FOLD_BLOCK) == 0 and seq % GRID_W == 0
    assert seq % (TILES_PER_STEP * QUERY_TILE) == 0 and QUERY_TILE % DFT_FINE == 0
    assert FNET_WIDTH == len(DFT_AFTER_HEADS) * DFT_BLOCK and (TILES_PER_STEP * MLA_HEADS) % AHEAD == 0
    assert ctx_len % KEY_CHUNK == 0 and ctx_len % CHUNK == 0 and ROW_TILE % OUT_SLAB == 0 and seq % KEY_CHUNK == 0
    assert max(POOL_WINDOWS) // 2 <= SUBLANES <= HALO and MLA_HEADS % AHEAD == 0

    rows = -(-(bsz + 1) // 8) * 8
    cond = jnp.zeros((rows, d), F32).at[:bsz].set(c).at[bsz].set(c_ctx)
    mod_ab = _modulation(cond, ab_w_mod[0], ab_b_mod[0])
    mod_cd = _modulation(cond, cd_w_mod[0], cd_b_mod[0])
    lat_row = lambda b: b
    ctx_row = lambda b: bsz

    ab_args = (ab_w_in[0].astype(BF16), ab_pool_w[0].astype(BF16), ab_pool_scale[0].reshape(1, -1),
               ab_sgu_w[0].astype(BF16),
               jnp.broadcast_to(ab_sgu_b[0][:, :, None], (SGU_HEADS, CHUNK, GROUP_DIM)),
               ab_w_out[0].astype(BF16), ab_ln_g[0].reshape(1, -1), ab_ln_b[0].reshape(1, -1))
    x1 = _ab_layer(x, mod_ab, lat_row, *ab_args, tl=ROW_TILE)
    ctx1 = _ab_layer(ctx, mod_ab, ctx_row, *ab_args, tl=ctx.shape[1])

    w = cd_w_in[0]
    o_cq = 2 * FNET_WIDTH
    o_ckv = o_cq + Q_LORA
    o_kr = o_ckv + KV_LORA
    o_dg = o_kr + QK_ROPE
    partner = np.array([_rope_partner(j) for j in range(QK_ROPE)])
    zeros = lambda n: jnp.zeros((d, n), F32)
    kr_cols = w[:, o_kr:o_dg]
    kr_lanes = jnp.concatenate([zeros(QK_NOPE), kr_cols, zeros(LANES - QK_NOPE - QK_ROPE)], axis=1)
    kr_both = jnp.concatenate([kr_cols[:, partner], zeros(QK_NOPE - QK_ROPE), kr_cols,
                               zeros(LANES - QK_NOPE - QK_ROPE)], axis=1)
    w1 = jnp.concatenate([w[:, 0:o_cq], w[:, o_dg:], w[:, o_cq:o_kr], kr_both], axis=1).astype(BF16)
    w_kv_ctx = jnp.concatenate([w[:, o_ckv:o_kr], kr_lanes], axis=1).astype(BF16)

    wq = cd_w_q_up[0].reshape(Q_LORA, nh, QK_NOPE + QK_ROPE)
    wqt = jnp.concatenate([wq.reshape(Q_LORA, -1), wq[:, :, QK_NOPE + partner].reshape(Q_LORA, -1)],
                          axis=1).T.astype(BF16)

    wkv = cd_w_kv_up[0].reshape(KV_LORA, nh, QK_NOPE + V_DIM)
    wk_pad = jnp.concatenate([wkv[:, :, :QK_NOPE], jnp.zeros((KV_LORA, nh, hp - QK_NOPE), F32)], axis=-1)
    lane = np.arange(hp)
    sel = np.diag(((lane >= QK_NOPE) & (lane < QK_NOPE + QK_ROPE)).astype(np.float32))
    wk = jnp.concatenate([wk_pad.reshape(KV_LORA, nh * hp),
                          jnp.asarray(np.tile(sel, (1, nh)))], axis=0).astype(BF16)
    wvt = wkv[:, :, QK_NOPE:].reshape(KV_LORA, nh * V_DIM).T.astype(BF16)

    q_norm = cd_q_norm[0].reshape(1, -1)
    kv_norm = cd_kv_norm[0].reshape(1, -1)
    cos, sin = _rope_tables(seq)
    rope_lanes = slice(QK_NOPE, QK_NOPE + QK_ROPE)
    cosq_t = (cos[:, rope_lanes] * np.float32(Q_SCALE)).T.copy()
    sinq_t = (sin[:, rope_lanes] * np.float32(Q_SCALE)).T.copy()
    cs = jnp.asarray(_channel_dft()).astype(BF16)

    kc, vtc = _cd_proj_ctx(ctx1, mod_cd, ctx_row, w_kv_ctx, kv_norm, wk, wvt)
    xcs, gf, gd, qt, kl, vtl = _cd_proj(x1, mod_cd, w1, cs, q_norm, wqt, kv_norm, wk, wvt,
                                        jnp.asarray(cosq_t), jnp.asarray(sinq_t),
                                        jnp.asarray(cos), jnp.asarray(sin), tl=ROW_TILE)
    tq = QUERY_TILE
    a_cos, a_sin, b_cos, b_sin = (jnp.asarray(t) for t in _dft_tables(seq))
    mirror = jnp.asarray(_mirror_matrix()).astype(BF16)
    vtc = vtc.reshape(bsz, nh, V_DIM, -1)
    vtl = vtl.reshape(bsz, nh, V_DIM, -1)
    return _cd_main(qt, kc, kl, vtc, vtl, xcs.reshape(bsz, 2 * seq, FNET_WIDTH), mirror,
                    a_cos, a_sin, b_cos, b_sin,
                    gf, gd, x1, mod_cd, cd_fnet_w[0].astype(BF16), cd_w_out[0].astype(BF16),
                    cd_ln_g[0].reshape(1, -1), cd_ln_b[0].reshape(1, -1), tq=tq)
```

```python
import functools

import numpy as np
import jax
import jax.numpy as jnp
from jax import lax
from jax.experimental import pallas as pl
from jax.experimental.pallas import tpu as pltpu

D_MODEL = 1024
GRID_W = 64
EPS = 1e-6
DEPTH = 2
POOL_WINDOWS = (2, 4, 8, 16)
GROUP_DIM = 128
POOL_WIDTH = 512
SGU_HEADS = 4
SGU_WIDTH = 512
CHUNK = 128
FNET_HEADS = 4
FNET_WIDTH = 512
MLA_HEADS = 8
QK_NOPE = 64
QK_ROPE = 32
V_DIM = 64
Q_LORA = 256
KV_LORA = 128
MLA_WIDTH = 512
ROPE_AXIS = QK_ROPE // 2
ROPE_BASE = 10000.0
ALPHA = (2 * DEPTH) ** 0.25
Q_SCALE = float((QK_NOPE + QK_ROPE) ** -0.5 * np.log2(np.e))

LANES = 128
SUBLANES = 8
BF16_SUBLANES = 16
MXU_TILE = 256
V7X_VMEM_BYTES = 64 * 1024 * 1024
VMEM_LIMIT = V7X_VMEM_BYTES * 7 // 8

HALO = BF16_SUBLANES
HEAD_PAD = LANES
OUT_SLAB = 2 * MXU_TILE
ROW_TILE = 1024
QUERY_TILE = 256
MOD_COL_TILE = 768

F32 = jnp.float32
BF16 = jnp.bfloat16


def _silu(v):
    half = 0.5 * v
    return half * (1.0 + jnp.tanh(half))


def _dot(a, b):
    return jnp.dot(a, b, preferred_element_type=F32)


def _deep_norm_rows(x, y, gate_scaled, g, b):
    r = x + gate_scaled * y
    mu = jnp.mean(r, axis=-1, keepdims=True)
    rc = r - mu
    var = jnp.mean(rc * rc, axis=-1, keepdims=True)
    return rc * lax.rsqrt(var + EPS / (ALPHA * ALPHA)) * g + b


def _rms_rows(v, g):
    return v * lax.rsqrt(jnp.mean(v * v, axis=-1, keepdims=True) + EPS) * g


def _mod_kernel(c_ref, w_ref, b_ref, o_ref):
    s = _silu(c_ref[...]).astype(BF16)
    o_ref[...] = _dot(s, w_ref[...].astype(BF16)) + b_ref[...]


def _modulation(cond, w_mod, b_mod):
    rows, d = cond.shape
    n = w_mod.shape[1]
    tn = MOD_COL_TILE
    out = pl.pallas_call(
        _mod_kernel,
        grid=(n // tn,),
        in_specs=[pl.BlockSpec((rows, d), lambda j: (0, 0)),
                  pl.BlockSpec((d, tn), lambda j: (0, j)),
                  pl.BlockSpec((1, tn), lambda j: (0, j))],
        out_specs=pl.BlockSpec((rows, tn), lambda j: (0, j)),
        out_shape=jax.ShapeDtypeStruct((rows, n), F32),
        name="modulation",
    )(cond, w_mod, b_mod.reshape(1, n))
    return out.reshape(rows, 1, n)


def _split_mod(mod_ref):
    d = D_MODEL
    return mod_ref[0, :, 0:d], 1.0 + mod_ref[0, :, d:2 * d], mod_ref[0, :, 2 * d:3 * d]


def _ab_kernel(x_ref, xp_ref, xn_ref, mod_ref, win_ref, poolw_ref, pscale_ref, sguw_ref, sgub_ref,
               wout_ref, lng_ref, lnb_ref, o_ref, h_ref, a_ref, mix_ref, *, tl, seq_len):
    t = pl.program_id(1)
    nt = pl.num_programs(1)
    shift, scale1, gate = _split_mod(mod_ref)
    x = x_ref[0]
    h_ref[HALO:HALO + tl, :] = (x * scale1 + shift).astype(BF16)
    h_ref[0:HALO, :] = (xp_ref[0] * scale1 + shift).astype(BF16)
    h_ref[HALO + tl:2 * HALO + tl, :] = (xn_ref[0] * scale1 + shift).astype(BF16)

    a_ext = _dot(h_ref[...], win_ref[:, 0:POOL_WIDTH])
    a_ref[HALO:HALO + tl, :] = a_ext[HALO:HALO + tl]
    a_ref[0:HALO, :] = jnp.where(t > 0, a_ext[0:HALO], 0.0)
    a_ref[HALO + tl:2 * HALO + tl, :] = jnp.where(t < nt - 1, a_ext[HALO + tl:2 * HALO + tl], 0.0)

    base = 2 * POOL_WIDTH
    slab = min(OUT_SLAB, tl)
    slabs = [slice(r0, r0 + slab) for r0 in range(0, tl, slab)]

    def wide_projections(rows):
        h = h_ref[HALO + rows.start:HALO + rows.stop, :]
        v = _dot(h, win_ref[:, base + SGU_WIDTH:base + 2 * SGU_WIDTH])
        gate_a = _silu(_dot(h, win_ref[:, POOL_WIDTH:2 * POOL_WIDTH]))
        return v, gate_a

    def mix_and_project(rows, v, gate_a):
        n = rows.stop - rows.start
        h = h_ref[HALO + rows.start:HALO + rows.stop, :]
        u = _dot(h, win_ref[:, base:base + SGU_WIDTH])
        gate_b = _silu(_dot(h, win_ref[:, base + 2 * SGU_WIDTH:base + 3 * SGU_WIDTH]))
        pos = t * tl + rows.start + lax.broadcasted_iota(jnp.int32, (SUBLANES, LANES), 0)
        for g, w in enumerate(POOL_WINDOWS):
            cols = slice(g * GROUP_DIM, (g + 1) * GROUP_DIM)
            first = HALO + rows.start - w // 2
            s = a_ref[first:first + n, cols]
            for j in range(1, w):
                s = s + a_ref[first + j:first + j + n, cols]
            edges = []
            for r0 in (0, n - SUBLANES):
                p = pos + r0
                count = jnp.minimum(p + (w - w // 2), seq_len) - jnp.maximum(p - w // 2, 0)
                edges.append(s[r0:r0 + SUBLANES] / count.astype(F32))
            pooled = jnp.concatenate([edges[0], s[SUBLANES:n - SUBLANES] * (1.0 / w), edges[1]], axis=0)
            diff = pooled - a_ref[HALO + rows.start:HALO + rows.stop, cols]
            ya = _dot(diff.astype(BF16), poolw_ref[g]) * pscale_ref[:, cols] * gate_a[:, cols]
            mix_ref[rows, cols] = ya.astype(BF16)

        for hd in range(SGU_HEADS):
            cols = slice(hd * GROUP_DIM, (hd + 1) * GROUP_DIM)
            vh = v[:, cols]
            mu = jnp.mean(vh, axis=-1, keepdims=True)
            vc = vh - mu
            var = jnp.mean(vc * vc, axis=-1, keepdims=True)
            vn = (vc * lax.rsqrt(var + EPS)).astype(BF16)
            f = jnp.concatenate(
                [_dot(sguw_ref[hd], vn[ci * CHUNK:(ci + 1) * CHUNK, :]) + sgub_ref[hd]
                 for ci in range(n // CHUNK)], axis=0)
            yb = u[:, cols] * f * gate_b[:, cols]
            mix_ref[rows, POOL_WIDTH + hd * GROUP_DIM:POOL_WIDTH + (hd + 1) * GROUP_DIM] = yb.astype(BF16)

        y = _dot(mix_ref[rows, :], wout_ref[...])
        o_ref[0, rows, :] = _deep_norm_rows(x_ref[0, rows, :], y, gate * (1.0 / ALPHA), lng_ref[...], lnb_ref[...])

    pending = None
    for rows in slabs:
        wide = wide_projections(rows)
        if pending is not None:
            mix_and_project(*pending)
        pending = (rows,) + wide
    mix_and_project(*pending)


def _ab_layer(x, mod, mod_row, w_in, pool_w, pool_scale, sgu_w, sgu_b, w_out, ln_g, ln_b, *, tl):
    bsz, seq, d = x.shape
    nt = seq // tl
    hb = tl // HALO
    last_hb = seq // HALO - 1
    const2 = lambda b, t: (0, 0)
    const3 = lambda b, t: (0, 0, 0)
    kern = functools.partial(_ab_kernel, tl=tl, seq_len=seq)
    return pl.pallas_call(
        kern,
        grid=(bsz, nt),
        in_specs=[
            pl.BlockSpec((1, tl, d), lambda b, t: (b, t, 0)),
            pl.BlockSpec((1, HALO, d), lambda b, t: (b, jnp.maximum(t * hb - 1, 0), 0)),
            pl.BlockSpec((1, HALO, d), lambda b, t: (b, jnp.minimum((t + 1) * hb, last_hb), 0)),
            pl.BlockSpec((1, 1, 3 * d), lambda b, t: (mod_row(b), 0, 0)),
            pl.BlockSpec(w_in.shape, const2),
            pl.BlockSpec(pool_w.shape, const3),
            pl.BlockSpec(pool_scale.shape, const2),
            pl.BlockSpec(sgu_w.shape, const3),
            pl.BlockSpec(sgu_b.shape, const3),
            pl.BlockSpec(w_out.shape, const2),
            pl.BlockSpec(ln_g.shape, const2),
            pl.BlockSpec(ln_b.shape, const2),
        ],
        out_specs=pl.BlockSpec((1, tl, d), lambda b, t: (b, t, 0)),
        out_shape=jax.ShapeDtypeStruct(x.shape, F32),
        scratch_shapes=[pltpu.VMEM((tl + 2 * HALO, d), BF16),
                        pltpu.VMEM((tl + 2 * HALO, POOL_WIDTH), F32),
                        pltpu.VMEM((tl, POOL_WIDTH + SGU_WIDTH), BF16)],
        compiler_params=pltpu.CompilerParams(dimension_semantics=("parallel", "arbitrary"),
                                             vmem_limit_bytes=VMEM_LIMIT),
        name="ab_layer",
    )(x, x, x, mod, w_in, pool_w, pool_scale, sgu_w, sgu_b, w_out, ln_g, ln_b)


def _dot_nt(a, b):
    return lax.dot_general(a, b, (((1,), (1,)), ((), ())), preferred_element_type=F32)


def _store_kv(ckv, kr_lanes, kvnorm_ref, wk_ref, wvt_ref, k_ref, vt_ref, e=0):
    ckv_n = _rms_rows(ckv, kvnorm_ref[...]).astype(BF16)
    hk = jnp.concatenate([ckv_n, kr_lanes.astype(BF16)], axis=-1)
    kk = _dot(hk, wk_ref[...]).astype(BF16)
    for hd in range(MLA_HEADS):
        k_ref[e, hd] = kk[:, hd * HEAD_PAD:(hd + 1) * HEAD_PAD]
    vt_ref[e] = _dot_nt(wvt_ref[...], ckv_n).astype(BF16)


def _cd_proj_ctx_kernel(x_ref, mod_ref, w_ref, kvnorm_ref, wk_ref, wvt_ref, k_ref, vt_ref):
    shift, scale1, _ = _split_mod(mod_ref)
    for e in range(x_ref.shape[0]):
        h = (x_ref[e] * scale1 + shift).astype(BF16)
        z = _dot(h, w_ref[...])
        _store_kv(z[:, 0:KV_LORA], z[:, KV_LORA:KV_LORA + LANES], kvnorm_ref, wk_ref, wvt_ref, k_ref, vt_ref, e)


def _cd_proj_kernel(x_ref, mod_ref, w_ref, cs_ref, qnorm_ref, wqt_ref, kvnorm_ref, wk_ref, wvt_ref,
                    cosqt_ref, sinqt_ref, cosk_ref, sink_ref,
                    xcs_ref, gf_ref, gd_ref, qt_ref, k_ref, vt_ref):
    shift, scale1, _ = _split_mod(mod_ref)
    h = (x_ref[0] * scale1 + shift).astype(BF16)
    fw = FNET_WIDTH
    o = 2 * fw + MLA_WIDTH
    small = _dot(h, w_ref[:, o:o + Q_LORA + KV_LORA + LANES])
    cq = small[:, 0:Q_LORA]
    ckv = small[:, Q_LORA:Q_LORA + KV_LORA]
    kr = small[:, Q_LORA + KV_LORA:Q_LORA + KV_LORA + LANES]
    f_in = _dot(h, w_ref[:, 0:fw]).astype(BF16)
    gf_ref[0] = _silu(_dot(h, w_ref[:, fw:2 * fw])).astype(BF16)
    gd_ref[0] = _silu(_dot(h, w_ref[:, 2 * fw:2 * fw + MLA_WIDTH])).astype(BF16)

    qn = _rms_rows(cq, qnorm_ref[...]).astype(BF16)
    q2t = _dot_nt(wqt_ref[...], qn)
    head_dim = QK_NOPE + QK_ROPE
    swap0 = MLA_HEADS * head_dim
    cosqt, sinqt = cosqt_ref[...], sinqt_ref[...]
    tl = q2t.shape[1]
    pad = jnp.zeros((HEAD_PAD - head_dim, tl), F32)
    for hd in range(MLA_HEADS):
        nope = q2t[hd * head_dim:hd * head_dim + QK_NOPE, :]
        rope = q2t[hd * head_dim + QK_NOPE:(hd + 1) * head_dim, :]
        rope_sw = q2t[swap0 + hd * QK_ROPE:swap0 + (hd + 1) * QK_ROPE, :]
        qt_ref[0, hd] = jnp.concatenate(
            [nope * Q_SCALE, rope * cosqt + rope_sw * sinqt, pad], axis=0).astype(BF16)

    for hd in range(FNET_HEADS):
        cols = slice(hd * GROUP_DIM, (hd + 1) * GROUP_DIM)
        xcs = _dot(f_in[:, cols], cs_ref[...]).astype(BF16)
        xcs_ref[0, 0, :, cols] = xcs[:, 0:GROUP_DIM]
        xcs_ref[0, 1, :, cols] = xcs[:, GROUP_DIM:2 * GROUP_DIM]

    kr_rot = kr * cosk_ref[...] + pltpu.roll(kr, LANES // 2, 1) * sink_ref[...]
    _store_kv(ckv, kr_rot, kvnorm_ref, wk_ref, wvt_ref, k_ref, vt_ref)


def _const_spec(a):
    nd = a.ndim
    return pl.BlockSpec(a.shape, lambda b, t: (0,) * nd)


def _cd_proj_ctx(ctx, mod, mod_row, w_kv, kv_norm, wk, wvt):
    bsz, seq, d = ctx.shape
    hp = HEAD_PAD
    grp = max(g for g in (4, 2, 1) if bsz % g == 0)
    return pl.pallas_call(
        _cd_proj_ctx_kernel,
        grid=(bsz // grp, 1),
        in_specs=[pl.BlockSpec((grp, seq, d), lambda b, t: (b, 0, 0)),
                  pl.BlockSpec((1, 1, 3 * d), lambda b, t: (mod_row(b), 0, 0)),
                  _const_spec(w_kv), _const_spec(kv_norm), _const_spec(wk), _const_spec(wvt)],
        out_specs=[pl.BlockSpec((grp, MLA_HEADS, seq, hp), lambda b, t: (b, 0, 0, 0)),
                   pl.BlockSpec((grp, MLA_WIDTH, seq), lambda b, t: (b, 0, 0))],
        out_shape=[jax.ShapeDtypeStruct((bsz, MLA_HEADS, seq, hp), BF16),
                   jax.ShapeDtypeStruct((bsz, MLA_WIDTH, seq), BF16)],
        compiler_params=pltpu.CompilerParams(dimension_semantics=("parallel", "arbitrary"),
                                             vmem_limit_bytes=VMEM_LIMIT),
        name="cd_proj_ctx",
    )(ctx, mod, w_kv, kv_norm, wk, wvt)


def _cd_proj(x, mod, w1, cs, q_norm, wqt, kv_norm, wk, wvt, cosqt, sinqt, cosk, sink, *, tl):
    bsz, seq, d = x.shape
    hp = HEAD_PAD
    fw = FNET_WIDTH
    tab = pl.BlockSpec((tl, LANES), lambda b, t: (t, 0))
    tab_t = pl.BlockSpec((QK_ROPE, tl), lambda b, t: (0, t))
    return pl.pallas_call(
        _cd_proj_kernel,
        grid=(bsz, seq // tl),
        in_specs=[pl.BlockSpec((1, tl, d), lambda b, t: (b, t, 0)),
                  pl.BlockSpec((1, 1, 3 * d), lambda b, t: (b, 0, 0)),
                  _const_spec(w1), _const_spec(cs), _const_spec(q_norm), _const_spec(wqt),
                  _const_spec(kv_norm), _const_spec(wk), _const_spec(wvt), tab_t, tab_t, tab, tab],
        out_specs=[pl.BlockSpec((1, 2, tl, fw), lambda b, t: (b, 0, t, 0)),
                   pl.BlockSpec((1, tl, fw), lambda b, t: (b, t, 0)),
                   pl.BlockSpec((1, tl, MLA_WIDTH), lambda b, t: (b, t, 0)),
                   pl.BlockSpec((1, MLA_HEADS, hp, tl), lambda b, t: (b, 0, 0, t)),
                   pl.BlockSpec((1, MLA_HEADS, tl, hp), lambda b, t: (b, 0, t, 0)),
                   pl.BlockSpec((1, MLA_WIDTH, tl), lambda b, t: (b, 0, t))],
        out_shape=[jax.ShapeDtypeStruct((bsz, 2, seq, fw), BF16),
                   jax.ShapeDtypeStruct((bsz, seq, fw), BF16),
                   jax.ShapeDtypeStruct((bsz, seq, MLA_WIDTH), BF16),
                   jax.ShapeDtypeStruct((bsz, MLA_HEADS, hp, seq), BF16),
                   jax.ShapeDtypeStruct((bsz, MLA_HEADS, seq, hp), BF16),
                   jax.ShapeDtypeStruct((bsz, MLA_WIDTH, seq), BF16)],
        compiler_params=pltpu.CompilerParams(dimension_semantics=("parallel", "arbitrary"),
                                             vmem_limit_bytes=VMEM_LIMIT),
        name="cd_proj",
    )(x, mod, w1, cs, q_norm, wqt, kv_norm, wk, wvt, cosqt, sinqt, cosk, sink)


DFT_FINE = 32
FOLD_BLOCK = MXU_TILE
DFT_BLOCK = MXU_TILE
DFT_AFTER_HEADS = (1, 3)
KEY_CHUNK = MXU_TILE
AHEAD = 4
TILES_PER_STEP = 2


def _fold_rows(s, op):
    while s.shape[0] > SUBLANES:
        half = s.shape[0] // 2
        s = op(s[:half], s[half:])
    return s


def _fold_sequence(xcs_ref, mirror_ref, fold_ref, seq_len):
    half = seq_len // 2
    blk = FOLD_BLOCK
    n_blk = seq_len // blk
    first_row = lax.broadcasted_iota(jnp.int32, (BF16_SUBLANES, FNET_WIDTH), 0) == 0
    for part, sign in ((0, 1.0), (1, -1.0)):
        base = part * seq_len
        for b in range(half // blk):
            low = xcs_ref[0, base + b * blk:base + (b + 1) * blk, :].astype(F32)
            upper = xcs_ref[0, base + (n_blk - 1 - b) * blk:base + (n_blk - b) * blk, :]
            folded = low + sign * _dot(mirror_ref[...], upper)
            if b > 0 or part == 1:
                src = base + (n_blk - b) * blk if b > 0 else half
                partner = xcs_ref[0, src:src + BF16_SUBLANES, :].astype(F32)
                head = low[0:BF16_SUBLANES] + sign * partner if b > 0 else partner
                folded = jnp.concatenate(
                    [jnp.where(first_row, head, folded[0:BF16_SUBLANES]), folded[BF16_SUBLANES:]], axis=0)
            fold_ref[part * half + b * blk:part * half + (b + 1) * blk, :] = folded.astype(BF16)


def _cd_main_kernel(qt_ref, qtn_ref, kc_ref, kl_ref, vtc_ref, vtl_ref, xcs_ref, mirror_ref, ac_ref, as_ref,
                    bc_ref, bs_ref, gf_ref, gd_ref, x_ref, mod_ref, fw_ref, wout_ref, lng_ref, lnb_ref,
                    o_ref, cls_ref, fold_ref, attnt_ref, m_ref, *s_refs, tq, seq_len):
    lc = kc_ref.shape[2]
    n_chunks = (lc + seq_len) // KEY_CHUNK
    ctx_chunks = lc // KEY_CHUNK
    half = seq_len // 2
    tiles = qt_ref.shape[3] // tq
    n_items = tiles * MLA_HEADS

    def query_of(item):
        tile, hd = divmod(item, MLA_HEADS)
        if tile < tiles:
            return qt_ref[0, hd, :, tile * tq:(tile + 1) * tq]
        return qtn_ref[0, hd]

    def chunk_of(ctx_ref, lat_ref, hd, c, lanes):
        ref, c = (ctx_ref, c) if c < ctx_chunks else (lat_ref, c - ctx_chunks)
        span = slice(c * KEY_CHUNK, (c + 1) * KEY_CHUNK)
        return ref[0, hd, :, span] if lanes else ref[0, hd, span, :]

    def score_chunk(s_ref, item, c, m8):
        hd = item % MLA_HEADS
        s = _dot(chunk_of(kc_ref, kl_ref, hd, c, False), query_of(item))
        s_ref[c * KEY_CHUNK:(c + 1) * KEY_CHUNK, :] = s
        part = _fold_rows(s, jnp.maximum)
        return part if m8 is None else jnp.maximum(m8, part)

    ones_rows = jnp.ones((BF16_SUBLANES, KEY_CHUNK), BF16)

    def prob_chunk(s, hd, c, m, acc):
        p = jnp.exp2(s - m)
        vt_ext = jnp.concatenate([chunk_of(vtc_ref, vtl_ref, hd, c, True), ones_rows], axis=0)
        o_t = _dot(vt_ext, p.astype(BF16))
        return o_t if acc is None else acc + o_t

    def item_step(item, m8):
        s_ref, hd = s_refs[item % AHEAD], item % MLA_HEADS
        m = jnp.max(m8, axis=0, keepdims=True)
        m8, acc = None, None
        for c in range(n_chunks):
            s_old = s_ref[c * KEY_CHUNK:(c + 1) * KEY_CHUNK, :]
            m8 = score_chunk(s_ref, item + AHEAD, c, m8)
            acc = prob_chunk(s_old, hd, c, m, acc)
        attnt_ref[hd] = acc[0:V_DIM] * (1.0 / acc[V_DIM:V_DIM + 1])
        return m8

    @pl.when(pl.program_id(1) == 0)
    def _():
        _fold_sequence(xcs_ref, mirror_ref, fold_ref, seq_len)
        m8s = [None] * AHEAD
        for c in range(n_chunks):
            for item in range(AHEAD):
                m8s[item] = score_chunk(s_refs[item], item, c, m8s[item])
        for item in range(AHEAD):
            m_ref[item] = m8s[item]

    _, _, gate = _split_mod(mod_ref)
    bc, bs = bc_ref[...], bs_ref[...]
    coarse = tq // DFT_FINE
    m8s = [m_ref[slot] for slot in range(AHEAD)]
    for tile in range(tiles):
        span = slice(tile * tq, (tile + 1) * tq)
        for i in range(coarse):
            ac = ac_ref[tile * coarse + i:tile * coarse + i + 1, :]
            asn = as_ref[tile * coarse + i:tile * coarse + i + 1, :]
            rows = slice(i * DFT_FINE, (i + 1) * DFT_FINE)
            cls_ref[tile, rows, 0:half] = (ac * bc - asn * bs).astype(BF16)
            cls_ref[tile, rows, half:2 * half] = (asn * bc + ac * bs).astype(BF16)
        specs, yc = [], None
        for hd in range(MLA_HEADS):
            item = tile * MLA_HEADS + hd
            m8s[item % AHEAD] = item_step(item, m8s[item % AHEAD])
            if hd in DFT_AFTER_HEADS:
                j = DFT_AFTER_HEADS.index(hd)
                cols = slice(j * DFT_BLOCK, (j + 1) * DFT_BLOCK)
                specs.append(_dot(cls_ref[tile], fold_ref[:, cols]).astype(BF16))
            if hd == DFT_AFTER_HEADS[-1] + 1:
                yc = _dot(jnp.concatenate(specs, axis=-1), fw_ref[...]) * gf_ref[0, span, :].astype(F32)
        yd = attnt_ref[...].reshape(MLA_WIDTH, tq).T * gd_ref[0, span, :].astype(F32)
        y = _dot(jnp.concatenate([yc.astype(BF16), yd.astype(BF16)], axis=-1), wout_ref[...])
        o_ref[0, span, :] = _deep_norm_rows(x_ref[0, span, :], y, gate * (1.0 / ALPHA), lng_ref[...], lnb_ref[...])
    assert n_items % AHEAD == 0
    for slot in range(AHEAD):
        m_ref[slot] = m8s[slot]


def _cd_main(qt, kc, kl, vtc, vtl, xcs, mirror, a_cos, a_sin, b_cos, b_sin, gf, gd, x, mod, fnet_w, w_out,
             ln_g, ln_b, *, tq):
    bsz, seq, d = x.shape
    lc = kc.shape[2]
    hp = HEAD_PAD
    nh = MLA_HEADS
    per_b4 = lambda b, t: (b, 0, 0, 0)
    per_b3 = lambda b, t: (b, 0, 0)
    kern = functools.partial(_cd_main_kernel, tq=tq, seq_len=seq)
    tiles = TILES_PER_STEP
    rows = tiles * tq
    coarse = rows // DFT_FINE
    last_tile = seq // tq - 1
    return pl.pallas_call(
        kern,
        grid=(bsz, seq // rows),
        in_specs=[pl.BlockSpec((1, nh, hp, rows), lambda b, t: (b, 0, 0, t)),
                  pl.BlockSpec((1, nh, hp, tq), lambda b, t: (b, 0, 0, jnp.minimum(tiles * (t + 1), last_tile))),
                  pl.BlockSpec((1, nh, lc, hp), per_b4),
                  pl.BlockSpec((1, nh, seq, hp), per_b4),
                  pl.BlockSpec((1, nh, V_DIM, lc), per_b4),
                  pl.BlockSpec((1, nh, V_DIM, seq), per_b4),
                  pl.BlockSpec((1, 2 * seq, FNET_WIDTH), per_b3),
                  _const_spec(mirror),
                  pl.BlockSpec((coarse, seq // 2), lambda b, t: (t, 0)),
                  pl.BlockSpec((coarse, seq // 2), lambda b, t: (t, 0)),
                  _const_spec(b_cos), _const_spec(b_sin),
                  pl.BlockSpec((1, rows, FNET_WIDTH), lambda b, t: (b, t, 0)),
                  pl.BlockSpec((1, rows, MLA_WIDTH), lambda b, t: (b, t, 0)),
                  pl.BlockSpec((1, rows, d), lambda b, t: (b, t, 0)),
                  pl.BlockSpec((1, 1, 3 * d), lambda b, t: (b, 0, 0)),
                  _const_spec(fnet_w), _const_spec(w_out), _const_spec(ln_g), _const_spec(ln_b)],
        out_specs=pl.BlockSpec((1, rows, d), lambda b, t: (b, t, 0)),
        out_shape=jax.ShapeDtypeStruct(x.shape, F32),
        scratch_shapes=[pltpu.VMEM((tiles, tq, seq), BF16),
                        pltpu.VMEM((seq, FNET_WIDTH), BF16),
                        pltpu.VMEM((nh, V_DIM, tq), F32),
                        pltpu.VMEM((AHEAD, SUBLANES, tq), F32)]
                       + [pltpu.VMEM((lc + seq, tq), F32)] * AHEAD,
        compiler_params=pltpu.CompilerParams(dimension_semantics=("parallel", "arbitrary"),
                                             vmem_limit_bytes=VMEM_LIMIT),
        name="cd_main",
    )(qt, qt, kc, kl, vtc, vtl, xcs, mirror, a_cos, a_sin, b_cos, b_sin, gf, gd, x, mod, fnet_w, w_out, ln_g, ln_b)


def _rope_tables(seq):
    pos = np.arange(seq)
    row = (pos // GRID_W).astype(np.float64)
    col = (pos % GRID_W).astype(np.float64)
    half = ROPE_AXIS // 2
    inv = (ROPE_BASE ** (-np.arange(0, ROPE_AXIS, 2, dtype=np.float32) / ROPE_AXIS)).astype(np.float32)
    cos = np.ones((seq, LANES), np.float32)
    sin = np.zeros((seq, LANES), np.float32)
    for j in range(QK_ROPE):
        coord = row if j < ROPE_AXIS else col
        ang = (coord.astype(np.float32) * inv[j % half]).astype(np.float32)
        cos[:, QK_NOPE + j] = np.cos(ang)
        sign = -1.0 if (j % ROPE_AXIS) < half else 1.0
        sin[:, QK_NOPE + j] = sign * np.sin(ang)
    return cos, sin


def _rope_partner(j):
    half = ROPE_AXIS // 2
    return j + half if (j % ROPE_AXIS) < half else j - half


def _dft_tables(seq):
    l = np.arange(seq // 2, dtype=np.float64)
    k1 = np.arange(seq // DFT_FINE, dtype=np.float64)[:, None]
    k0 = np.arange(DFT_FINE, dtype=np.float64)[:, None]
    ang_a = 2.0 * np.pi * ((k1 * DFT_FINE * l) % seq) / seq
    ang_b = 2.0 * np.pi * ((k0 * l) % seq) / seq
    norm = 1.0 / np.sqrt(seq * GROUP_DIM)
    b_sin = np.sin(ang_b) * norm
    b_sin[:, 0] = np.where(np.arange(DFT_FINE) % 2 == 0, norm, -norm)
    return (np.cos(ang_a).astype(np.float32), np.sin(ang_a).astype(np.float32),
            (np.cos(ang_b) * norm).astype(np.float32), b_sin.astype(np.float32))


def _mirror_matrix():
    m = np.zeros((FOLD_BLOCK, FOLD_BLOCK), np.float32)
    i = np.arange(1, FOLD_BLOCK)
    m[i, FOLD_BLOCK - i] = 1.0
    return m


def _channel_dft():
    c = np.arange(GROUP_DIM, dtype=np.float64)
    ang = 2.0 * np.pi * ((c[:, None] * c[None, :]) % GROUP_DIM) / GROUP_DIM
    return np.concatenate([np.cos(ang), -np.sin(ang)], axis=1).astype(np.float32)


def kernel(x, c, ctx, c_ctx, ab_w_mod, ab_b_mod, ab_w_in, ab_pool_w, ab_pool_scale, ab_sgu_w, ab_sgu_b,
           ab_w_out, ab_ln_g, ab_ln_b, cd_w_mod, cd_b_mod, cd_w_in, cd_fnet_w, cd_q_norm, cd_kv_norm,
           cd_w_q_up, cd_w_kv_up, cd_w_out, cd_ln_g, cd_ln_b):
    bsz, seq, d = x.shape
    nh, hp = MLA_HEADS, HEAD_PAD
    ctx_len = ctx.shape[1]
    assert d == D_MODEL and seq % ROW_TILE == 0 and seq % (2 * FOLD_BLOCK) == 0 and seq % GRID_W == 0
    assert seq % (TILES_PER_STEP * QUERY_TILE) == 0 and QUERY_TILE % DFT_FINE == 0
    assert FNET_WIDTH == len(DFT_AFTER_HEADS) * DFT_BLOCK and (TILES_PER_STEP * MLA_HEADS) % AHEAD == 0
    assert ctx_len % KEY_CHUNK == 0 and ctx_len % CHUNK == 0 and ROW_TILE % OUT_SLAB == 0 and seq % KEY_CHUNK == 0
    assert max(POOL_WINDOWS) // 2 <= SUBLANES <= HALO and MLA_HEADS % AHEAD == 0

    rows = -(-(bsz + 1) // 8) * 8
    cond = jnp.zeros((rows, d), F32).at[:bsz].set(c).at[bsz].set(c_ctx)
    mod_ab = _modulation(cond, ab_w_mod[0], ab_b_mod[0])
    mod_cd = _modulation(cond, cd_w_mod[0], cd_b_mod[0])
    lat_row = lambda b: b
    ctx_row = lambda b: bsz

    ab_args = (ab_w_in[0].astype(BF16), ab_pool_w[0].astype(BF16), ab_pool_scale[0].reshape(1, -1),
               ab_sgu_w[0].astype(BF16),
               jnp.broadcast_to(ab_sgu_b[0][:, :, None], (SGU_HEADS, CHUNK, GROUP_DIM)),
               ab_w_out[0].astype(BF16), ab_ln_g[0].reshape(1, -1), ab_ln_b[0].reshape(1, -1))
    x1 = _ab_layer(x, mod_ab, lat_row, *ab_args, tl=ROW_TILE)
    ctx1 = _ab_layer(ctx, mod_ab, ctx_row, *ab_args, tl=ctx.shape[1])

    w = cd_w_in[0]
    o_cq = 2 * FNET_WIDTH
    o_ckv = o_cq + Q_LORA
    o_kr = o_ckv + KV_LORA
    o_dg = o_kr + QK_ROPE
    partner = np.array([_rope_partner(j) for j in range(QK_ROPE)])
    zeros = lambda n: jnp.zeros((d, n), F32)
    kr_cols = w[:, o_kr:o_dg]
    kr_lanes = jnp.concatenate([zeros(QK_NOPE), kr_cols, zeros(LANES - QK_NOPE - QK_ROPE)], axis=1)
    kr_both = jnp.concatenate([kr_cols[:, partner], zeros(QK_NOPE - QK_ROPE), kr_cols,
                               zeros(LANES - QK_NOPE - QK_ROPE)], axis=1)
    w1 = jnp.concatenate([w[:, 0:o_cq], w[:, o_dg:], w[:, o_cq:o_kr], kr_both], axis=1).astype(BF16)
    w_kv_ctx = jnp.concatenate([w[:, o_ckv:o_kr], kr_lanes], axis=1).astype(BF16)

    wq = cd_w_q_up[0].reshape(Q_LORA, nh, QK_NOPE + QK_ROPE)
    wqt = jnp.concatenate([wq.reshape(Q_LORA, -1), wq[:, :, QK_NOPE + partner].reshape(Q_LORA, -1)],
                          axis=1).T.astype(BF16)

    wkv = cd_w_kv_up[0].reshape(KV_LORA, nh, QK_NOPE + V_DIM)
    wk_pad = jnp.concatenate([wkv[:, :, :QK_NOPE], jnp.zeros((KV_LORA, nh, hp - QK_NOPE), F32)], axis=-1)
    lane = np.arange(hp)
    sel = np.diag(((lane >= QK_NOPE) & (lane < QK_NOPE + QK_ROPE)).astype(np.float32))
    wk = jnp.concatenate([wk_pad.reshape(KV_LORA, nh * hp),
                          jnp.asarray(np.tile(sel, (1, nh)))], axis=0).astype(BF16)
    wvt = wkv[:, :, QK_NOPE:].reshape(KV_LORA, nh * V_DIM).T.astype(BF16)

    q_norm = cd_q_norm[0].reshape(1, -1)
    kv_norm = cd_kv_norm[0].reshape(1, -1)
    cos, sin = _rope_tables(seq)
    rope_lanes = slice(QK_NOPE, QK_NOPE + QK_ROPE)
    cosq_t = (cos[:, rope_lanes] * np.float32(Q_SCALE)).T.copy()
    sinq_t = (sin[:, rope_lanes] * np.float32(Q_SCALE)).T.copy()
    cs = jnp.asarray(_channel_dft()).astype(BF16)

    kc, vtc = _cd_proj_ctx(ctx1, mod_cd, ctx_row, w_kv_ctx, kv_norm, wk, wvt)
    xcs, gf, gd, qt, kl, vtl = _cd_proj(x1, mod_cd, w1, cs, q_norm, wqt, kv_norm, wk, wvt,
                                        jnp.asarray(cosq_t), jnp.asarray(sinq_t),
                                        jnp.asarray(cos), jnp.asarray(sin), tl=ROW_TILE)
    tq = QUERY_TILE
    a_cos, a_sin, b_cos, b_sin = (jnp.asarray(t) for t in _dft_tables(seq))
    mirror = jnp.asarray(_mirror_matrix()).astype(BF16)
    vtc = vtc.reshape(bsz, nh, V_DIM, -1)
    vtl = vtl.reshape(bsz, nh, V_DIM, -1)
    return _cd_main(qt, kc, kl, vtc, vtl, xcs.reshape(bsz, 2 * seq, FNET_WIDTH), mirror,
                    a_cos, a_sin, b_cos, b_sin,
                    gf, gd, x1, mod_cd, cd_fnet_w[0].astype(BF16), cd_w_out[0].astype(BF16),
                    cd_ln_g[0].reshape(1, -1), cd_ln_b[0].reshape(1, -1), tq=tq)
```

```python
import functools

import numpy as np
import jax
import jax.numpy as jnp
from jax import lax
from jax.experimental import pallas as pl
from jax.experimental.pallas import tpu as pltpu

D_MODEL = 1024
GRID_W = 64
EPS = 1e-6
DEPTH = 2
POOL_WINDOWS = (2, 4, 8, 16)
GROUP_DIM = 128
POOL_WIDTH = 512
SGU_HEADS = 4
SGU_WIDTH = 512
CHUNK = 128
FNET_HEADS = 4
FNET_WIDTH = 512
MLA_HEADS = 8
QK_NOPE = 64
QK_ROPE = 32
V_DIM = 64
Q_LORA = 256
KV_LORA = 128
MLA_WIDTH = 512
ROPE_AXIS = QK_ROPE // 2
ROPE_BASE = 10000.0
ALPHA = (2 * DEPTH) ** 0.25
Q_SCALE = float((QK_NOPE + QK_ROPE) ** -0.5 * np.log2(np.e))

LANES = 128
SUBLANES = 8
BF16_SUBLANES = 16
MXU_TILE = 256
V7X_VMEM_BYTES = 64 * 1024 * 1024
VMEM_LIMIT = V7X_VMEM_BYTES * 7 // 8

HALO = BF16_SUBLANES
HEAD_PAD = LANES
OUT_SLAB = 2 * MXU_TILE
ROW_TILE = 1024
QUERY_TILE = 256
MOD_COL_TILE = 768

F32 = jnp.float32
BF16 = jnp.bfloat16


def _silu(v):
    half = 0.5 * v
    return half * (1.0 + jnp.tanh(half))


def _dot(a, b):
    return jnp.dot(a, b, preferred_element_type=F32)


def _deep_norm_rows(x, y, gate_scaled, g, b):
    r = x + gate_scaled * y
    mu = jnp.mean(r, axis=-1, keepdims=True)
    rc = r - mu
    var = jnp.mean(rc * rc, axis=-1, keepdims=True)
    return rc * lax.rsqrt(var + EPS / (ALPHA * ALPHA)) * g + b


def _rms_rows(v, g):
    return v * lax.rsqrt(jnp.mean(v * v, axis=-1, keepdims=True) + EPS) * g


def _mod_kernel(c_ref, w_ref, b_ref, o_ref):
    s = _silu(c_ref[...]).astype(BF16)
    o_ref[...] = _dot(s, w_ref[...].astype(BF16)) + b_ref[...]


def _modulation(cond, w_mod, b_mod):
    rows, d = cond.shape
    n = w_mod.shape[1]
    tn = MOD_COL_TILE
    out = pl.pallas_call(
        _mod_kernel,
        grid=(n // tn,),
        in_specs=[pl.BlockSpec((rows, d), lambda j: (0, 0)),
                  pl.BlockSpec((d, tn), lambda j: (0, j)),
                  pl.BlockSpec((1, tn), lambda j: (0, j))],
        out_specs=pl.BlockSpec((rows, tn), lambda j: (0, j)),
        out_shape=jax.ShapeDtypeStruct((rows, n), F32),
        name="modulation",
    )(cond, w_mod, b_mod.reshape(1, n))
    return out.reshape(rows, 1, n)


def _split_mod(mod_ref):
    d = D_MODEL
    return mod_ref[0, :, 0:d], 1.0 + mod_ref[0, :, d:2 * d], mod_ref[0, :, 2 * d:3 * d]


def _ab_kernel(x_ref, xp_ref, xn_ref, mod_ref, win_ref, poolw_ref, pscale_ref, sguw_ref, sgub_ref,
               wout_ref, lng_ref, lnb_ref, o_ref, h_ref, a_ref, mix_ref, *, tl, seq_len):
    t = pl.program_id(1)
    nt = pl.num_programs(1)
    shift, scale1, gate = _split_mod(mod_ref)
    x = x_ref[0]
    h_ref[HALO:HALO + tl, :] = (x * scale1 + shift).astype(BF16)
    h_ref[0:HALO, :] = (xp_ref[0] * scale1 + shift).astype(BF16)
    h_ref[HALO + tl:2 * HALO + tl, :] = (xn_ref[0] * scale1 + shift).astype(BF16)

    a_ext = _dot(h_ref[...], win_ref[:, 0:POOL_WIDTH])
    a_ref[HALO:HALO + tl, :] = a_ext[HALO:HALO + tl]
    a_ref[0:HALO, :] = jnp.where(t > 0, a_ext[0:HALO], 0.0)
    a_ref[HALO + tl:2 * HALO + tl, :] = jnp.where(t < nt - 1, a_ext[HALO + tl:2 * HALO + tl], 0.0)

    base = 2 * POOL_WIDTH
    slab = min(OUT_SLAB, tl)
    slabs = [slice(r0, r0 + slab) for r0 in range(0, tl, slab)]

    def wide_projections(rows):
        h = h_ref[HALO + rows.start:HALO + rows.stop, :]
        return (_dot(h, win_ref[:, base + SGU_WIDTH:base + 2 * SGU_WIDTH]),)

    def mix_and_project(rows, v):
        n = rows.stop - rows.start
        h = h_ref[HALO + rows.start:HALO + rows.stop, :]
        pos = t * tl + rows.start + lax.broadcasted_iota(jnp.int32, (SUBLANES, LANES), 0)
        for g, w in enumerate(POOL_WINDOWS):
            cols = slice(g * GROUP_DIM, (g + 1) * GROUP_DIM)
            if g % 2 == 0:
                pc = slice(POOL_WIDTH + g * GROUP_DIM, POOL_WIDTH + (g + 2) * GROUP_DIM)
                ga2 = _silu(_dot(h, win_ref[:, pc]))
            loc = slice((g % 2) * GROUP_DIM, (g % 2 + 1) * GROUP_DIM)
            first = HALO + rows.start - w // 2
            s = a_ref[first:first + n, cols]
            for j in range(1, w):
                s = s + a_ref[first + j:first + j + n, cols]
            edges = []
            for r0 in (0, n - SUBLANES):
                p = pos + r0
                count = jnp.minimum(p + (w - w // 2), seq_len) - jnp.maximum(p - w // 2, 0)
                edges.append(s[r0:r0 + SUBLANES] / count.astype(F32))
            pooled = jnp.concatenate([edges[0], s[SUBLANES:n - SUBLANES] * (1.0 / w), edges[1]], axis=0)
            diff = pooled - a_ref[HALO + rows.start:HALO + rows.stop, cols]
            ya = _dot(diff.astype(BF16), poolw_ref[g]) * pscale_ref[:, cols] * ga2[:, loc]
            mix_ref[rows, cols] = ya.astype(BF16)

        for hd in range(SGU_HEADS):
            cols = slice(hd * GROUP_DIM, (hd + 1) * GROUP_DIM)
            if hd % 2 == 0:
                pc = slice(hd * GROUP_DIM, (hd + 2) * GROUP_DIM)
                u2 = _dot(h, win_ref[:, base + pc.start:base + pc.stop])
                gb2 = _silu(_dot(h, win_ref[:, base + 2 * SGU_WIDTH + pc.start:base + 2 * SGU_WIDTH + pc.stop]))
            loc = slice((hd % 2) * GROUP_DIM, (hd % 2 + 1) * GROUP_DIM)
            vh = v[:, cols]
            mu = jnp.mean(vh, axis=-1, keepdims=True)
            vc = vh - mu
            var = jnp.mean(vc * vc, axis=-1, keepdims=True)
            vn = (vc * lax.rsqrt(var + EPS)).astype(BF16)
            f = jnp.concatenate(
                [_dot(sguw_ref[hd], vn[ci * CHUNK:(ci + 1) * CHUNK, :]) + sgub_ref[hd]
                 for ci in range(n // CHUNK)], axis=0)
            yb = u2[:, loc] * f * gb2[:, loc]
            mix_ref[rows, POOL_WIDTH + hd * GROUP_DIM:POOL_WIDTH + (hd + 1) * GROUP_DIM] = yb.astype(BF16)

        y = _dot(mix_ref[rows, :], wout_ref[...])
        o_ref[0, rows, :] = _deep_norm_rows(x_ref[0, rows, :], y, gate * (1.0 / ALPHA), lng_ref[...], lnb_ref[...])

    pending = None
    for rows in slabs:
        wide = wide_projections(rows)
        if pending is not None:
            mix_and_project(*pending)
        pending = (rows,) + wide
    mix_and_project(*pending)


def _ab_layer(x, mod, mod_row, w_in, pool_w, pool_scale, sgu_w, sgu_b, w_out, ln_g, ln_b, *, tl):
    bsz, seq, d = x.shape
    nt = seq // tl
    hb = tl // HALO
    last_hb = seq // HALO - 1
    const2 = lambda b, t: (0, 0)
    const3 = lambda b, t: (0, 0, 0)
    kern = functools.partial(_ab_kernel, tl=tl, seq_len=seq)
    return pl.pallas_call(
        kern,
        grid=(bsz, nt),
        in_specs=[
            pl.BlockSpec((1, tl, d), lambda b, t: (b, t, 0)),
            pl.BlockSpec((1, HALO, d), lambda b, t: (b, jnp.maximum(t * hb - 1, 0), 0)),
            pl.BlockSpec((1, HALO, d), lambda b, t: (b, jnp.minimum((t + 1) * hb, last_hb), 0)),
            pl.BlockSpec((1, 1, 3 * d), lambda b, t: (mod_row(b), 0, 0)),
            pl.BlockSpec(w_in.shape, const2),
            pl.BlockSpec(pool_w.shape, const3),
            pl.BlockSpec(pool_scale.shape, const2),
            pl.BlockSpec(sgu_w.shape, const3),
            pl.BlockSpec(sgu_b.shape, const3),
            pl.BlockSpec(w_out.shape, const2),
            pl.BlockSpec(ln_g.shape, const2),
            pl.BlockSpec(ln_b.shape, const2),
        ],
        out_specs=pl.BlockSpec((1, tl, d), lambda b, t: (b, t, 0)),
        out_shape=jax.ShapeDtypeStruct(x.shape, F32),
        scratch_shapes=[pltpu.VMEM((tl + 2 * HALO, d), BF16),
                        pltpu.VMEM((tl + 2 * HALO, POOL_WIDTH), F32),
                        pltpu.VMEM((tl, POOL_WIDTH + SGU_WIDTH), BF16)],
        compiler_params=pltpu.CompilerParams(dimension_semantics=("parallel", "arbitrary"),
                                             vmem_limit_bytes=VMEM_LIMIT),
        name="ab_layer",
    )(x, x, x, mod, w_in, pool_w, pool_scale, sgu_w, sgu_b, w_out, ln_g, ln_b)


def _dot_nt(a, b):
    return lax.dot_general(a, b, (((1,), (1,)), ((), ())), preferred_element_type=F32)


def _store_kv(ckv, kr_lanes, kvnorm_ref, wk_ref, wvt_ref, k_ref, vt_ref, e=0):
    ckv_n = _rms_rows(ckv, kvnorm_ref[...]).astype(BF16)
    hk = jnp.concatenate([ckv_n, kr_lanes.astype(BF16)], axis=-1)
    kk = _dot(hk, wk_ref[...]).astype(BF16)
    for hd in range(MLA_HEADS):
        k_ref[e, hd] = kk[:, hd * HEAD_PAD:(hd + 1) * HEAD_PAD]
    vt_ref[e] = _dot_nt(wvt_ref[...], ckv_n).astype(BF16)


def _cd_proj_ctx_kernel(x_ref, mod_ref, w_ref, kvnorm_ref, wk_ref, wvt_ref, k_ref, vt_ref):
    shift, scale1, _ = _split_mod(mod_ref)
    for e in range(x_ref.shape[0]):
        h = (x_ref[e] * scale1 + shift).astype(BF16)
        z = _dot(h, w_ref[...])
        _store_kv(z[:, 0:KV_LORA], z[:, KV_LORA:KV_LORA + LANES], kvnorm_ref, wk_ref, wvt_ref, k_ref, vt_ref, e)


def _cd_proj_kernel(x_ref, mod_ref, w_ref, cs_ref, qnorm_ref, wqt_ref, kvnorm_ref, wk_ref, wvt_ref,
                    cosqt_ref, sinqt_ref, cosk_ref, sink_ref,
                    xcs_ref, gf_ref, gd_ref, qt_ref, k_ref, vt_ref):
    shift, scale1, _ = _split_mod(mod_ref)
    h = (x_ref[0] * scale1 + shift).astype(BF16)
    fw = FNET_WIDTH
    o = 2 * fw + MLA_WIDTH
    small = _dot(h, w_ref[:, o:o + Q_LORA + KV_LORA + LANES])
    cq = small[:, 0:Q_LORA]
    ckv = small[:, Q_LORA:Q_LORA + KV_LORA]
    kr = small[:, Q_LORA + KV_LORA:Q_LORA + KV_LORA + LANES]
    f_in = _dot(h, w_ref[:, 0:fw]).astype(BF16)
    gf_ref[0] = _silu(_dot(h, w_ref[:, fw:2 * fw])).astype(BF16)
    gd_ref[0] = _silu(_dot(h, w_ref[:, 2 * fw:2 * fw + MLA_WIDTH])).astype(BF16)

    qn = _rms_rows(cq, qnorm_ref[...]).astype(BF16)
    q2t = _dot_nt(wqt_ref[...], qn)
    head_dim = QK_NOPE + QK_ROPE
    swap0 = MLA_HEADS * head_dim
    cosqt, sinqt = cosqt_ref[...], sinqt_ref[...]
    tl = q2t.shape[1]
    pad = jnp.zeros((HEAD_PAD - head_dim, tl), F32)
    for hd in range(MLA_HEADS):
        nope = q2t[hd * head_dim:hd * head_dim + QK_NOPE, :]
        rope = q2t[hd * head_dim + QK_NOPE:(hd + 1) * head_dim, :]
        rope_sw = q2t[swap0 + hd * QK_ROPE:swap0 + (hd + 1) * QK_ROPE, :]
        qt_ref[0, hd] = jnp.concatenate(
            [nope * Q_SCALE, rope * cosqt + rope_sw * sinqt, pad], axis=0).astype(BF16)

    for hd in range(FNET_HEADS):
        cols = slice(hd * GROUP_DIM, (hd + 1) * GROUP_DIM)
        xcs = _dot(f_in[:, cols], cs_ref[...]).astype(BF16)
        xcs_ref[0, 0, :, cols] = xcs[:, 0:GROUP_DIM]
        xcs_ref[0, 1, :, cols] = xcs[:, GROUP_DIM:2 * GROUP_DIM]

    kr_rot = kr * cosk_ref[...] + pltpu.roll(kr, LANES // 2, 1) * sink_ref[...]
    _store_kv(ckv, kr_rot, kvnorm_ref, wk_ref, wvt_ref, k_ref, vt_ref)


def _const_spec(a):
    nd = a.ndim
    return pl.BlockSpec(a.shape, lambda b, t: (0,) * nd)


def _cd_proj_ctx(ctx, mod, mod_row, w_kv, kv_norm, wk, wvt):
    bsz, seq, d = ctx.shape
    hp = HEAD_PAD
    grp = max(g for g in (4, 2, 1) if bsz % g == 0)
    return pl.pallas_call(
        _cd_proj_ctx_kernel,
        grid=(bsz // grp, 1),
        in_specs=[pl.BlockSpec((grp, seq, d), lambda b, t: (b, 0, 0)),
                  pl.BlockSpec((1, 1, 3 * d), lambda b, t: (mod_row(b), 0, 0)),
                  _const_spec(w_kv), _const_spec(kv_norm), _const_spec(wk), _const_spec(wvt)],
        out_specs=[pl.BlockSpec((grp, MLA_HEADS, seq, hp), lambda b, t: (b, 0, 0, 0)),
                   pl.BlockSpec((grp, MLA_WIDTH, seq), lambda b, t: (b, 0, 0))],
        out_shape=[jax.ShapeDtypeStruct((bsz, MLA_HEADS, seq, hp), BF16),
                   jax.ShapeDtypeStruct((bsz, MLA_WIDTH, seq), BF16)],
        compiler_params=pltpu.CompilerParams(dimension_semantics=("parallel", "arbitrary"),
                                             vmem_limit_bytes=VMEM_LIMIT),
        name="cd_proj_ctx",
    )(ctx, mod, w_kv, kv_norm, wk, wvt)


def _cd_proj(x, mod, w1, cs, q_norm, wqt, kv_norm, wk, wvt, cosqt, sinqt, cosk, sink, *, tl):
    bsz, seq, d = x.shape
    hp = HEAD_PAD
    fw = FNET_WIDTH
    tab = pl.BlockSpec((tl, LANES), lambda b, t: (t, 0))
    tab_t = pl.BlockSpec((QK_ROPE, tl), lambda b, t: (0, t))
    return pl.pallas_call(
        _cd_proj_kernel,
        grid=(bsz, seq // tl),
        in_specs=[pl.BlockSpec((1, tl, d), lambda b, t: (b, t, 0)),
                  pl.BlockSpec((1, 1, 3 * d), lambda b, t: (b, 0, 0)),
                  _const_spec(w1), _const_spec(cs), _const_spec(q_norm), _const_spec(wqt),
                  _const_spec(kv_norm), _const_spec(wk), _const_spec(wvt), tab_t, tab_t, tab, tab],
        out_specs=[pl.BlockSpec((1, 2, tl, fw), lambda b, t: (b, 0, t, 0)),
                   pl.BlockSpec((1, tl, fw), lambda b, t: (b, t, 0)),
                   pl.BlockSpec((1, tl, MLA_WIDTH), lambda b, t: (b, t, 0)),
                   pl.BlockSpec((1, MLA_HEADS, hp, tl), lambda b, t: (b, 0, 0, t)),
                   pl.BlockSpec((1, MLA_HEADS, tl, hp), lambda b, t: (b, 0, t, 0)),
                   pl.BlockSpec((1, MLA_WIDTH, tl), lambda b, t: (b, 0, t))],
        out_shape=[jax.ShapeDtypeStruct((bsz, 2, seq, fw), BF16),
                   jax.ShapeDtypeStruct((bsz, seq, fw), BF16),
                   jax.ShapeDtypeStruct((bsz, seq, MLA_WIDTH), BF16),
                   jax.ShapeDtypeStruct((bsz, MLA_HEADS, hp, seq), BF16),
                   jax.ShapeDtypeStruct((bsz, MLA_HEADS, seq, hp), BF16),
                   jax.ShapeDtypeStruct((bsz, MLA_WIDTH, seq), BF16)],
        compiler_params=pltpu.CompilerParams(dimension_semantics=("parallel", "arbitrary"),
                                             vmem_limit_bytes=VMEM_LIMIT),
        name="cd_proj",
    )(x, mod, w1, cs, q_norm, wqt, kv_norm, wk, wvt, cosqt, sinqt, cosk, sink)


DFT_FINE = 32
FOLD_BLOCK = MXU_TILE
DFT_BLOCK = MXU_TILE
DFT_AFTER_HEADS = (1, 3)
KEY_CHUNK = MXU_TILE
AHEAD = 4
TILES_PER_STEP = 2


def _fold_rows(s, op):
    while s.shape[0] > SUBLANES:
        half = s.shape[0] // 2
        s = op(s[:half], s[half:])
    return s


def _fold_sequence(xcs_ref, mirror_ref, fold_ref, seq_len):
    half = seq_len // 2
    blk = FOLD_BLOCK
    n_blk = seq_len // blk
    first_row = lax.broadcasted_iota(jnp.int32, (BF16_SUBLANES, FNET_WIDTH), 0) == 0
    for part, sign in ((0, 1.0), (1, -1.0)):
        base = part * seq_len
        for b in range(half // blk):
            low = xcs_ref[0, base + b * blk:base + (b + 1) * blk, :].astype(F32)
            upper = xcs_ref[0, base + (n_blk - 1 - b) * blk:base + (n_blk - b) * blk, :]
            folded = low + sign * _dot(mirror_ref[...], upper)
            if b > 0 or part == 1:
                src = base + (n_blk - b) * blk if b > 0 else half
                partner = xcs_ref[0, src:src + BF16_SUBLANES, :].astype(F32)
                head = low[0:BF16_SUBLANES] + sign * partner if b > 0 else partner
                folded = jnp.concatenate(
                    [jnp.where(first_row, head, folded[0:BF16_SUBLANES]), folded[BF16_SUBLANES:]], axis=0)
            fold_ref[part * half + b * blk:part * half + (b + 1) * blk, :] = folded.astype(BF16)


def _cd_main_kernel(qt_ref, qtn_ref, kc_ref, kl_ref, vtc_ref, vtl_ref, xcs_ref, mirror_ref, ac_ref, as_ref,
                    bc_ref, bs_ref, gf_ref, gd_ref, x_ref, mod_ref, fw_ref, wout_ref, lng_ref, lnb_ref,
                    o_ref, cls_ref, fold_ref, attnt_ref, m_ref, *s_refs, tq, seq_len):
    lc = kc_ref.shape[2]
    n_chunks = (lc + seq_len) // KEY_CHUNK
    ctx_chunks = lc // KEY_CHUNK
    half = seq_len // 2
    tiles = qt_ref.shape[3] // tq
    n_items = tiles * MLA_HEADS

    def query_of(item):
        tile, hd = divmod(item, MLA_HEADS)
        if tile < tiles:
            return qt_ref[0, hd, :, tile * tq:(tile + 1) * tq]
        return qtn_ref[0, hd]

    def chunk_of(ctx_ref, lat_ref, hd, c, lanes):
        ref, c = (ctx_ref, c) if c < ctx_chunks else (lat_ref, c - ctx_chunks)
        span = slice(c * KEY_CHUNK, (c + 1) * KEY_CHUNK)
        return ref[0, hd, :, span] if lanes else ref[0, hd, span, :]

    def score_chunk(s_ref, item, c, m8):
        hd = item % MLA_HEADS
        s = _dot(chunk_of(kc_ref, kl_ref, hd, c, False), query_of(item))
        s_ref[c * KEY_CHUNK:(c + 1) * KEY_CHUNK, :] = s
        part = _fold_rows(s, jnp.maximum)
        return part if m8 is None else jnp.maximum(m8, part)

    ones_rows = jnp.ones((BF16_SUBLANES, KEY_CHUNK), BF16)

    def prob_chunk(s, hd, c, m, acc):
        p = jnp.exp2(s - m)
        vt_ext = jnp.concatenate([chunk_of(vtc_ref, vtl_ref, hd, c, True), ones_rows], axis=0)
        o_t = _dot(vt_ext, p.astype(BF16))
        return o_t if acc is None else acc + o_t

    def item_step(item, m8):
        s_ref, hd = s_refs[item % AHEAD], item % MLA_HEADS
        m = jnp.max(m8, axis=0, keepdims=True)
        m8, acc = None, None
        for c in range(n_chunks):
            s_old = s_ref[c * KEY_CHUNK:(c + 1) * KEY_CHUNK, :]
            m8 = score_chunk(s_ref, item + AHEAD, c, m8)
            acc = prob_chunk(s_old, hd, c, m, acc)
        attnt_ref[hd] = acc[0:V_DIM] * (1.0 / acc[V_DIM:V_DIM + 1])
        return m8

    @pl.when(pl.program_id(1) == 0)
    def _():
        _fold_sequence(xcs_ref, mirror_ref, fold_ref, seq_len)
        m8s = [None] * AHEAD
        for c in range(n_chunks):
            for item in range(AHEAD):
                m8s[item] = score_chunk(s_refs[item], item, c, m8s[item])
        for item in range(AHEAD):
            m_ref[item] = m8s[item]

    _, _, gate = _split_mod(mod_ref)
    bc, bs = bc_ref[...], bs_ref[...]
    coarse = tq // DFT_FINE
    m8s = [m_ref[slot] for slot in range(AHEAD)]
    for tile in range(tiles):
        span = slice(tile * tq, (tile + 1) * tq)
        for i in range(coarse):
            ac = ac_ref[tile * coarse + i:tile * coarse + i + 1, :]
            asn = as_ref[tile * coarse + i:tile * coarse + i + 1, :]
            rows = slice(i * DFT_FINE, (i + 1) * DFT_FINE)
            cls_ref[tile, rows, 0:half] = (ac * bc - asn * bs).astype(BF16)
            cls_ref[tile, rows, half:2 * half] = (asn * bc + ac * bs).astype(BF16)
        specs, yc = [], None
        for hd in range(MLA_HEADS):
            item = tile * MLA_HEADS + hd
            m8s[item % AHEAD] = item_step(item, m8s[item % AHEAD])
            if hd in DFT_AFTER_HEADS:
                j = DFT_AFTER_HEADS.index(hd)
                cols = slice(j * DFT_BLOCK, (j + 1) * DFT_BLOCK)
                specs.append(_dot(cls_ref[tile], fold_ref[:, cols]).astype(BF16))
            if hd == DFT_AFTER_HEADS[-1] + 1:
                yc = _dot(jnp.concatenate(specs, axis=-1), fw_ref[...]) * gf_ref[0, span, :].astype(F32)
        yd = attnt_ref[...].reshape(MLA_WIDTH, tq).T * gd_ref[0, span, :].astype(F32)
        y = _dot(jnp.concatenate([yc.astype(BF16), yd.astype(BF16)], axis=-1), wout_ref[...])
        o_ref[0, span, :] = _deep_norm_rows(x_ref[0, span, :], y, gate * (1.0 / ALPHA), lng_ref[...], lnb_ref[...])
    assert n_items % AHEAD == 0
    for slot in range(AHEAD):
        m_ref[slot] = m8s[slot]


def _cd_main(qt, kc, kl, vtc, vtl, xcs, mirror, a_cos, a_sin, b_cos, b_sin, gf, gd, x, mod, fnet_w, w_out,
             ln_g, ln_b, *, tq):
    bsz, seq, d = x.shape
    lc = kc.shape[2]
    hp = HEAD_PAD
    nh = MLA_HEADS
    per_b4 = lambda b, t: (b, 0, 0, 0)
    per_b3 = lambda b, t: (b, 0, 0)
    kern = functools.partial(_cd_main_kernel, tq=tq, seq_len=seq)
    tiles = TILES_PER_STEP
    rows = tiles * tq
    coarse = rows // DFT_FINE
    last_tile = seq // tq - 1
    return pl.pallas_call(
        kern,
        grid=(bsz, seq // rows),
        in_specs=[pl.BlockSpec((1, nh, hp, rows), lambda b, t: (b, 0, 0, t)),
                  pl.BlockSpec((1, nh, hp, tq), lambda b, t: (b, 0, 0, jnp.minimum(tiles * (t + 1), last_tile))),
                  pl.BlockSpec((1, nh, lc, hp), per_b4),
                  pl.BlockSpec((1, nh, seq, hp), per_b4),
                  pl.BlockSpec((1, nh, V_DIM, lc), per_b4),
                  pl.BlockSpec((1, nh, V_DIM, seq), per_b4),
                  pl.BlockSpec((1, 2 * seq, FNET_WIDTH), per_b3),
                  _const_spec(mirror),
                  pl.BlockSpec((coarse, seq // 2), lambda b, t: (t, 0)),
                  pl.BlockSpec((coarse, seq // 2), lambda b, t: (t, 0)),
                  _const_spec(b_cos), _const_spec(b_sin),
                  pl.BlockSpec((1, rows, FNET_WIDTH), lambda b, t: (b, t, 0)),
                  pl.BlockSpec((1, rows, MLA_WIDTH), lambda b, t: (b, t, 0)),
                  pl.BlockSpec((1, rows, d), lambda b, t: (b, t, 0)),
                  pl.BlockSpec((1, 1, 3 * d), lambda b, t: (b, 0, 0)),
                  _const_spec(fnet_w), _const_spec(w_out), _const_spec(ln_g), _const_spec(ln_b)],
        out_specs=pl.BlockSpec((1, rows, d), lambda b, t: (b, t, 0)),
        out_shape=jax.ShapeDtypeStruct(x.shape, F32),
        scratch_shapes=[pltpu.VMEM((tiles, tq, seq), BF16),
                        pltpu.VMEM((seq, FNET_WIDTH), BF16),
                        pltpu.VMEM((nh, V_DIM, tq), F32),
                        pltpu.VMEM((AHEAD, SUBLANES, tq), F32)]
                       + [pltpu.VMEM((lc + seq, tq), F32)] * AHEAD,
        compiler_params=pltpu.CompilerParams(dimension_semantics=("parallel", "arbitrary"),
                                             vmem_limit_bytes=VMEM_LIMIT),
        name="cd_main",
    )(qt, qt, kc, kl, vtc, vtl, xcs, mirror, a_cos, a_sin, b_cos, b_sin, gf, gd, x, mod, fnet_w, w_out, ln_g, ln_b)


def _rope_tables(seq):
    pos = np.arange(seq)
    row = (pos // GRID_W).astype(np.float64)
    col = (pos % GRID_W).astype(np.float64)
    half = ROPE_AXIS // 2
    inv = (ROPE_BASE ** (-np.arange(0, ROPE_AXIS, 2, dtype=np.float32) / ROPE_AXIS)).astype(np.float32)
    cos = np.ones((seq, LANES), np.float32)
    sin = np.zeros((seq, LANES), np.float32)
    for j in range(QK_ROPE):
        coord = row if j < ROPE_AXIS else col
        ang = (coord.astype(np.float32) * inv[j % half]).astype(np.float32)
        cos[:, QK_NOPE + j] = np.cos(ang)
        sign = -1.0 if (j % ROPE_AXIS) < half else 1.0
        sin[:, QK_NOPE + j] = sign * np.sin(ang)
    return cos, sin


def _rope_partner(j):
    half = ROPE_AXIS // 2
    return j + half if (j % ROPE_AXIS) < half else j - half


def _dft_tables(seq):
    l = np.arange(seq // 2, dtype=np.float64)
    k1 = np.arange(seq // DFT_FINE, dtype=np.float64)[:, None]
    k0 = np.arange(DFT_FINE, dtype=np.float64)[:, None]
    ang_a = 2.0 * np.pi * ((k1 * DFT_FINE * l) % seq) / seq
    ang_b = 2.0 * np.pi * ((k0 * l) % seq) / seq
    norm = 1.0 / np.sqrt(seq * GROUP_DIM)
    b_sin = np.sin(ang_b) * norm
    b_sin[:, 0] = np.where(np.arange(DFT_FINE) % 2 == 0, norm, -norm)
    return (np.cos(ang_a).astype(np.float32), np.sin(ang_a).astype(np.float32),
            (np.cos(ang_b) * norm).astype(np.float32), b_sin.astype(np.float32))


def _mirror_matrix():
    m = np.zeros((FOLD_BLOCK, FOLD_BLOCK), np.float32)
    i = np.arange(1, FOLD_BLOCK)
    m[i, FOLD_BLOCK - i] = 1.0
    return m


def _channel_dft():
    c = np.arange(GROUP_DIM, dtype=np.float64)
    ang = 2.0 * np.pi * ((c[:, None] * c[None, :]) % GROUP_DIM) / GROUP_DIM
    return np.concatenate([np.cos(ang), -np.sin(ang)], axis=1).astype(np.float32)


def kernel(x, c, ctx, c_ctx, ab_w_mod, ab_b_mod, ab_w_in, ab_pool_w, ab_pool_scale, ab_sgu_w, ab_sgu_b,
           ab_w_out, ab_ln_g, ab_ln_b, cd_w_mod, cd_b_mod, cd_w_in, cd_fnet_w, cd_q_norm, cd_kv_norm,
           cd_w_q_up, cd_w_kv_up, cd_w_out, cd_ln_g, cd_ln_b):
    bsz, seq, d = x.shape
    nh, hp = MLA_HEADS, HEAD_PAD
    ctx_len = ctx.shape[1]
    assert d == D_MODEL and seq % ROW_TILE == 0 and seq % (2 * FOLD_BLOCK) == 0 and seq % GRID_W == 0
    assert seq % (TILES_PER_STEP * QUERY_TILE) == 0 and QUERY_TILE % DFT_FINE == 0
    assert FNET_WIDTH == len(DFT_AFTER_HEADS) * DFT_BLOCK and (TILES_PER_STEP * MLA_HEADS) % AHEAD == 0
    assert ctx_len % KEY_CHUNK == 0 and ctx_len % CHUNK == 0 and ROW_TILE % OUT_SLAB == 0 and seq % KEY_CHUNK == 0
    assert max(POOL_WINDOWS) // 2 <= SUBLANES <= HALO and MLA_HEADS % AHEAD == 0

    rows = -(-(bsz + 1) // 8) * 8
    cond = jnp.zeros((rows, d), F32).at[:bsz].set(c).at[bsz].set(c_ctx)
    mod_ab = _modulation(cond, ab_w_mod[0], ab_b_mod[0])
    mod_cd = _modulation(cond, cd_w_mod[0], cd_b_mod[0])
    lat_row = lambda b: b
    ctx_row = lambda b: bsz

    ab_args = (ab_w_in[0].astype(BF16), ab_pool_w[0].astype(BF16), ab_pool_scale[0].reshape(1, -1),
               ab_sgu_w[0].astype(BF16),
               jnp.broadcast_to(ab_sgu_b[0][:, :, None], (SGU_HEADS, CHUNK, GROUP_DIM)),
               ab_w_out[0].astype(BF16), ab_ln_g[0].reshape(1, -1), ab_ln_b[0].reshape(1, -1))
    x1 = _ab_layer(x, mod_ab, lat_row, *ab_args, tl=ROW_TILE)
    ctx1 = _ab_layer(ctx, mod_ab, ctx_row, *ab_args, tl=ctx.shape[1])

    w = cd_w_in[0]
    o_cq = 2 * FNET_WIDTH
    o_ckv = o_cq + Q_LORA
    o_kr = o_ckv + KV_LORA
    o_dg = o_kr + QK_ROPE
    partner = np.array([_rope_partner(j) for j in range(QK_ROPE)])
    zeros = lambda n: jnp.zeros((d, n), F32)
    kr_cols = w[:, o_kr:o_dg]
    kr_lanes = jnp.concatenate([zeros(QK_NOPE), kr_cols, zeros(LANES - QK_NOPE - QK_ROPE)], axis=1)
    kr_both = jnp.concatenate([kr_cols[:, partner], zeros(QK_NOPE - QK_ROPE), kr_cols,
                               zeros(LANES - QK_NOPE - QK_ROPE)], axis=1)
    w1 = jnp.concatenate([w[:, 0:o_cq], w[:, o_dg:], w[:, o_cq:o_kr], kr_both], axis=1).astype(BF16)
    w_kv_ctx = jnp.concatenate([w[:, o_ckv:o_kr], kr_lanes], axis=1).astype(BF16)

    wq = cd_w_q_up[0].reshape(Q_LORA, nh, QK_NOPE + QK_ROPE)
    wqt = jnp.concatenate([wq.reshape(Q_LORA, -1), wq[:, :, QK_NOPE + partner].reshape(Q_LORA, -1)],
                          axis=1).T.astype(BF16)

    wkv = cd_w_kv_up[0].reshape(KV_LORA, nh, QK_NOPE + V_DIM)
    wk_pad = jnp.concatenate([wkv[:, :, :QK_NOPE], jnp.zeros((KV_LORA, nh, hp - QK_NOPE), F32)], axis=-1)
    lane = np.arange(hp)
    sel = np.diag(((lane >= QK_NOPE) & (lane < QK_NOPE + QK_ROPE)).astype(np.float32))
    wk = jnp.concatenate([wk_pad.reshape(KV_LORA, nh * hp),
                          jnp.asarray(np.tile(sel, (1, nh)))], axis=0).astype(BF16)
    wvt = wkv[:, :, QK_NOPE:].reshape(KV_LORA, nh * V_DIM).T.astype(BF16)

    q_norm = cd_q_norm[0].reshape(1, -1)
    kv_norm = cd_kv_norm[0].reshape(1, -1)
    cos, sin = _rope_tables(seq)
    rope_lanes = slice(QK_NOPE, QK_NOPE + QK_ROPE)
    cosq_t = (cos[:, rope_lanes] * np.float32(Q_SCALE)).T.copy()
    sinq_t = (sin[:, rope_lanes] * np.float32(Q_SCALE)).T.copy()
    cs = jnp.asarray(_channel_dft()).astype(BF16)

    kc, vtc = _cd_proj_ctx(ctx1, mod_cd, ctx_row, w_kv_ctx, kv_norm, wk, wvt)
    xcs, gf, gd, qt, kl, vtl = _cd_proj(x1, mod_cd, w1, cs, q_norm, wqt, kv_norm, wk, wvt,
                                        jnp.asarray(cosq_t), jnp.asarray(sinq_t),
                                        jnp.asarray(cos), jnp.asarray(sin), tl=ROW_TILE)
    tq = QUERY_TILE
    a_cos, a_sin, b_cos, b_sin = (jnp.asarray(t) for t in _dft_tables(seq))
    mirror = jnp.asarray(_mirror_matrix()).astype(BF16)
    vtc = vtc.reshape(bsz, nh, V_DIM, -1)
    vtl = vtl.reshape(bsz, nh, V_DIM, -1)
    return _cd_main(qt, kc, kl, vtc, vtl, xcs.reshape(bsz, 2 * seq, FNET_WIDTH), mirror,
                    a_cos, a_sin, b_cos, b_sin,
                    gf, gd, x1, mod_cd, cd_fnet_w[0].astype(BF16), cd_w_out[0].astype(BF16),
                    cd_ln_g[0].reshape(1, -1), cd_ln_b[0].reshape(1, -1), tq=tq)
```

```python
import functools

import numpy as np
import jax
import jax.numpy as jnp
from jax import lax
from jax.experimental import pallas as pl
from jax.experimental.pallas import tpu as pltpu

D_MODEL = 1024
GRID_W = 64
EPS = 1e-6
DEPTH = 2
POOL_WINDOWS = (2, 4, 8, 16)
GROUP_DIM = 128
POOL_WIDTH = 512
SGU_HEADS = 4
SGU_WIDTH = 512
CHUNK = 128
FNET_HEADS = 4
FNET_WIDTH = 512
MLA_HEADS = 8
QK_NOPE = 64
QK_ROPE = 32
V_DIM = 64
Q_LORA = 256
KV_LORA = 128
MLA_WIDTH = 512
ROPE_AXIS = QK_ROPE // 2
ROPE_BASE = 10000.0
ALPHA = (2 * DEPTH) ** 0.25
Q_SCALE = float((QK_NOPE + QK_ROPE) ** -0.5 * np.log2(np.e))

LANES = 128
SUBLANES = 8
BF16_SUBLANES = 16
MXU_TILE = 256
V7X_VMEM_BYTES = 64 * 1024 * 1024
VMEM_LIMIT = V7X_VMEM_BYTES * 7 // 8

HALO = BF16_SUBLANES
HEAD_PAD = LANES
OUT_SLAB = 2 * MXU_TILE
ROW_TILE = 1024
QUERY_TILE = 256
MOD_COL_TILE = 768

F32 = jnp.float32
BF16 = jnp.bfloat16


def _silu(v):
    half = 0.5 * v
    return half * (1.0 + jnp.tanh(half))


def _dot(a, b):
    return jnp.dot(a, b, preferred_element_type=F32)


def _deep_norm_rows(x, y, gate_scaled, g, b):
    r = x + gate_scaled * y
    mu = jnp.mean(r, axis=-1, keepdims=True)
    rc = r - mu
    var = jnp.mean(rc * rc, axis=-1, keepdims=True)
    return rc * lax.rsqrt(var + EPS / (ALPHA * ALPHA)) * g + b


def _rms_rows(v, g):
    return v * lax.rsqrt(jnp.mean(v * v, axis=-1, keepdims=True) + EPS) * g


def _mod_kernel(c_ref, w_ref, b_ref, o_ref):
    s = _silu(c_ref[...]).astype(BF16)
    o_ref[...] = _dot(s, w_ref[...].astype(BF16)) + b_ref[...]


def _modulation(cond, w_mod, b_mod):
    rows, d = cond.shape
    n = w_mod.shape[1]
    tn = MOD_COL_TILE
    out = pl.pallas_call(
        _mod_kernel,
        grid=(n // tn,),
        in_specs=[pl.BlockSpec((rows, d), lambda j: (0, 0)),
                  pl.BlockSpec((d, tn), lambda j: (0, j)),
                  pl.BlockSpec((1, tn), lambda j: (0, j))],
        out_specs=pl.BlockSpec((rows, tn), lambda j: (0, j)),
        out_shape=jax.ShapeDtypeStruct((rows, n), F32),
        name="modulation",
    )(cond, w_mod, b_mod.reshape(1, n))
    return out.reshape(rows, 1, n)


def _split_mod(mod_ref):
    d = D_MODEL
    return mod_ref[0, :, 0:d], 1.0 + mod_ref[0, :, d:2 * d], mod_ref[0, :, 2 * d:3 * d]


def _ab_kernel(x_ref, xp_ref, xn_ref, mod_ref, win_ref, poolw_ref, pscale_ref, sguw_ref, sgub_ref,
               wout_ref, lng_ref, lnb_ref, o_ref, h_ref, a_ref, mix_ref, *, tl, seq_len):
    t = pl.program_id(1)
    nt = pl.num_programs(1)
    shift, scale1, gate = _split_mod(mod_ref)
    x = x_ref[0]
    h_ref[HALO:HALO + tl, :] = (x * scale1 + shift).astype(BF16)
    h_ref[0:HALO, :] = (xp_ref[0] * scale1 + shift).astype(BF16)
    h_ref[HALO + tl:2 * HALO + tl, :] = (xn_ref[0] * scale1 + shift).astype(BF16)

    a_ext = _dot(h_ref[...], win_ref[:, 0:POOL_WIDTH])
    a_ref[HALO:HALO + tl, :] = a_ext[HALO:HALO + tl]
    a_ref[0:HALO, :] = jnp.where(t > 0, a_ext[0:HALO], 0.0)
    a_ref[HALO + tl:2 * HALO + tl, :] = jnp.where(t < nt - 1, a_ext[HALO + tl:2 * HALO + tl], 0.0)

    base = 2 * POOL_WIDTH
    slab = min(OUT_SLAB, tl)
    slabs = [slice(r0, r0 + slab) for r0 in range(0, tl, slab)]

    def wide_projections(rows):
        h = h_ref[HALO + rows.start:HALO + rows.stop, :]
        v = _dot(h, win_ref[:, base + SGU_WIDTH:base + 2 * SGU_WIDTH])
        gate_a = _silu(_dot(h, win_ref[:, POOL_WIDTH:2 * POOL_WIDTH]))
        return v, gate_a

    def mix_and_project(rows, v, gate_a):
        n = rows.stop - rows.start
        h = h_ref[HALO + rows.start:HALO + rows.stop, :]
        u = _dot(h, win_ref[:, base:base + SGU_WIDTH])
        gate_b = _silu(_dot(h, win_ref[:, base + 2 * SGU_WIDTH:base + 3 * SGU_WIDTH]))
        pos = t * tl + rows.start + lax.broadcasted_iota(jnp.int32, (SUBLANES, LANES), 0)
        for g, w in enumerate(POOL_WINDOWS):
            cols = slice(g * GROUP_DIM, (g + 1) * GROUP_DIM)
            first = HALO + rows.start - w // 2
            s = a_ref[first:first + n, cols]
            for j in range(1, w):
                s = s + a_ref[first + j:first + j + n, cols]
            edges = []
            for r0 in (0, n - SUBLANES):
                p = pos + r0
                count = jnp.minimum(p + (w - w // 2), seq_len) - jnp.maximum(p - w // 2, 0)
                edges.append(s[r0:r0 + SUBLANES] / count.astype(F32))
            pooled = jnp.concatenate([edges[0], s[SUBLANES:n - SUBLANES] * (1.0 / w), edges[1]], axis=0)
            diff = pooled - a_ref[HALO + rows.start:HALO + rows.stop, cols]
            ya = _dot(diff.astype(BF16), poolw_ref[g]) * pscale_ref[:, cols] * gate_a[:, cols]
            mix_ref[rows, cols] = ya.astype(BF16)

        for hd in range(SGU_HEADS):
            cols = slice(hd * GROUP_DIM, (hd + 1) * GROUP_DIM)
            vh = v[:, cols]
            mu = jnp.mean(vh, axis=-1, keepdims=True)
            vc = vh - mu
            var = jnp.mean(vc * vc, axis=-1, keepdims=True)
            vn = (vc * lax.rsqrt(var + EPS)).astype(BF16)
            f = jnp.concatenate(
                [_dot(sguw_ref[hd], vn[ci * CHUNK:(ci + 1) * CHUNK, :]) + sgub_ref[hd]
                 for ci in range(n // CHUNK)], axis=0)
            yb = u[:, cols] * f * gate_b[:, cols]
            mix_ref[rows, POOL_WIDTH + hd * GROUP_DIM:POOL_WIDTH + (hd + 1) * GROUP_DIM] = yb.astype(BF16)

        y = _dot(mix_ref[rows, :], wout_ref[...])
        o_ref[0, rows, :] = _deep_norm_rows(x_ref[0, rows, :], y, gate * (1.0 / ALPHA), lng_ref[...], lnb_ref[...])

    pending = None
    for rows in slabs:
        wide = wide_projections(rows)
        if pending is not None:
            mix_and_project(*pending)
        pending = (rows,) + wide
    mix_and_project(*pending)


def _ab_layer(x, mod, mod_row, w_in, pool_w, pool_scale, sgu_w, sgu_b, w_out, ln_g, ln_b, *, tl):
    bsz, seq, d = x.shape
    nt = seq // tl
    hb = tl // HALO
    last_hb = seq // HALO - 1
    const2 = lambda b, t: (0, 0)
    const3 = lambda b, t: (0, 0, 0)
    kern = functools.partial(_ab_kernel, tl=tl, seq_len=seq)
    return pl.pallas_call(
        kern,
        grid=(bsz, nt),
        in_specs=[
            pl.BlockSpec((1, tl, d), lambda b, t: (b, t, 0)),
            pl.BlockSpec((1, HALO, d), lambda b, t: (b, jnp.maximum(t * hb - 1, 0), 0)),
            pl.BlockSpec((1, HALO, d), lambda b, t: (b, jnp.minimum((t + 1) * hb, last_hb), 0)),
            pl.BlockSpec((1, 1, 3 * d), lambda b, t: (mod_row(b), 0, 0)),
            pl.BlockSpec(w_in.shape, const2),
            pl.BlockSpec(pool_w.shape, const3),
            pl.BlockSpec(pool_scale.shape, const2),
            pl.BlockSpec(sgu_w.shape, const3),
            pl.BlockSpec(sgu_b.shape, const3),
            pl.BlockSpec(w_out.shape, const2),
            pl.BlockSpec(ln_g.shape, const2),
            pl.BlockSpec(ln_b.shape, const2),
        ],
        out_specs=pl.BlockSpec((1, tl, d), lambda b, t: (b, t, 0)),
        out_shape=jax.ShapeDtypeStruct(x.shape, F32),
        scratch_shapes=[pltpu.VMEM((tl + 2 * HALO, d), BF16),
                        pltpu.VMEM((tl + 2 * HALO, POOL_WIDTH), F32),
                        pltpu.VMEM((tl, POOL_WIDTH + SGU_WIDTH), BF16)],
        compiler_params=pltpu.CompilerParams(dimension_semantics=("parallel", "arbitrary"),
                                             vmem_limit_bytes=VMEM_LIMIT),
        name="ab_layer",
    )(x, x, x, mod, w_in, pool_w, pool_scale, sgu_w, sgu_b, w_out, ln_g, ln_b)


def _dot_nt(a, b):
    return lax.dot_general(a, b, (((1,), (1,)), ((), ())), preferred_element_type=F32)


def _store_kv(ckv, kr_lanes, kvnorm_ref, wk_ref, wvt_ref, k_ref, vt_ref, e=0):
    ckv_n = _rms_rows(ckv, kvnorm_ref[...]).astype(BF16)
    hk = jnp.concatenate([ckv_n, kr_lanes.astype(BF16)], axis=-1)
    kk = _dot(hk, wk_ref[...]).astype(BF16)
    for hd in range(MLA_HEADS):
        k_ref[e, hd] = kk[:, hd * HEAD_PAD:(hd + 1) * HEAD_PAD]
    vt_ref[e] = _dot_nt(wvt_ref[...], ckv_n).astype(BF16)


def _cd_proj_ctx_kernel(x_ref, mod_ref, w_ref, kvnorm_ref, wk_ref, wvt_ref, k_ref, vt_ref):
    shift, scale1, _ = _split_mod(mod_ref)
    for e in range(x_ref.shape[0]):
        h = (x_ref[e] * scale1 + shift).astype(BF16)
        z = _dot(h, w_ref[...])
        _store_kv(z[:, 0:KV_LORA], z[:, KV_LORA:KV_LORA + LANES], kvnorm_ref, wk_ref, wvt_ref, k_ref, vt_ref, e)


def _cd_proj_kernel(x_ref, mod_ref, w_ref, cs_ref, qnorm_ref, wqt_ref, kvnorm_ref, wk_ref, wvt_ref,
                    cosqt_ref, sinqt_ref, cosk_ref, sink_ref,
                    xcs_ref, gf_ref, gd_ref, qt_ref, k_ref, vt_ref):
    shift, scale1, _ = _split_mod(mod_ref)
    h = (x_ref[0] * scale1 + shift).astype(BF16)
    fw = FNET_WIDTH
    o = 2 * fw + MLA_WIDTH
    small = _dot(h, w_ref[:, o:o + Q_LORA + KV_LORA + LANES])
    cq = small[:, 0:Q_LORA]
    ckv = small[:, Q_LORA:Q_LORA + KV_LORA]
    kr = small[:, Q_LORA + KV_LORA:Q_LORA + KV_LORA + LANES]
    f_in = _dot(h, w_ref[:, 0:fw]).astype(BF16)
    gf_ref[0] = _silu(_dot(h, w_ref[:, fw:2 * fw])).astype(BF16)
    gd_ref[0] = _silu(_dot(h, w_ref[:, 2 * fw:2 * fw + MLA_WIDTH])).astype(BF16)

    qn = _rms_rows(cq, qnorm_ref[...]).astype(BF16)
    q2t = _dot_nt(wqt_ref[...], qn)
    head_dim = QK_NOPE + QK_ROPE
    swap0 = MLA_HEADS * head_dim
    cosqt, sinqt = cosqt_ref[...], sinqt_ref[...]
    tl = q2t.shape[1]
    pad = jnp.zeros((HEAD_PAD - head_dim, tl), F32)
    for hd in range(MLA_HEADS):
        nope = q2t[hd * head_dim:hd * head_dim + QK_NOPE, :]
        rope = q2t[hd * head_dim + QK_NOPE:(hd + 1) * head_dim, :]
        rope_sw = q2t[swap0 + hd * QK_ROPE:swap0 + (hd + 1) * QK_ROPE, :]
        qt_ref[0, hd] = jnp.concatenate(
            [nope * Q_SCALE, rope * cosqt + rope_sw * sinqt, pad], axis=0).astype(BF16)

    for hd in range(FNET_HEADS):
        cols = slice(hd * GROUP_DIM, (hd + 1) * GROUP_DIM)
        xcs = _dot(f_in[:, cols], cs_ref[...]).astype(BF16)
        xcs_ref[0, 0, :, cols] = xcs[:, 0:GROUP_DIM]
        xcs_ref[0, 1, :, cols] = xcs[:, GROUP_DIM:2 * GROUP_DIM]

    kr_rot = kr * cosk_ref[...] + pltpu.roll(kr, LANES // 2, 1) * sink_ref[...]
    _store_kv(ckv, kr_rot, kvnorm_ref, wk_ref, wvt_ref, k_ref, vt_ref)


def _const_spec(a):
    nd = a.ndim
    return pl.BlockSpec(a.shape, lambda b, t: (0,) * nd)


def _cd_proj_ctx(ctx, mod, mod_row, w_kv, kv_norm, wk, wvt):
    bsz, seq, d = ctx.shape
    hp = HEAD_PAD
    grp = max(g for g in (4, 2, 1) if bsz % g == 0)
    return pl.pallas_call(
        _cd_proj_ctx_kernel,
        grid=(bsz // grp, 1),
        in_specs=[pl.BlockSpec((grp, seq, d), lambda b, t: (b, 0, 0)),
                  pl.BlockSpec((1, 1, 3 * d), lambda b, t: (mod_row(b), 0, 0)),
                  _const_spec(w_kv), _const_spec(kv_norm), _const_spec(wk), _const_spec(wvt)],
        out_specs=[pl.BlockSpec((grp, MLA_HEADS, seq, hp), lambda b, t: (b, 0, 0, 0)),
                   pl.BlockSpec((grp, MLA_WIDTH, seq), lambda b, t: (b, 0, 0))],
        out_shape=[jax.ShapeDtypeStruct((bsz, MLA_HEADS, seq, hp), BF16),
                   jax.ShapeDtypeStruct((bsz, MLA_WIDTH, seq), BF16)],
        compiler_params=pltpu.CompilerParams(dimension_semantics=("parallel", "arbitrary"),
                                             vmem_limit_bytes=VMEM_LIMIT),
        name="cd_proj_ctx",
    )(ctx, mod, w_kv, kv_norm, wk, wvt)


def _cd_proj(x, mod, w1, cs, q_norm, wqt, kv_norm, wk, wvt, cosqt, sinqt, cosk, sink, *, tl):
    bsz, seq, d = x.shape
    hp = HEAD_PAD
    fw = FNET_WIDTH
    tab = pl.BlockSpec((tl, LANES), lambda b, t: (t, 0))
    tab_t = pl.BlockSpec((QK_ROPE, tl), lambda b, t: (0, t))
    return pl.pallas_call(
        _cd_proj_kernel,
        grid=(bsz, seq // tl),
        in_specs=[pl.BlockSpec((1, tl, d), lambda b, t: (b, t, 0)),
                  pl.BlockSpec((1, 1, 3 * d), lambda b, t: (b, 0, 0)),
                  _const_spec(w1), _const_spec(cs), _const_spec(q_norm), _const_spec(wqt),
                  _const_spec(kv_norm), _const_spec(wk), _const_spec(wvt), tab_t, tab_t, tab, tab],
        out_specs=[pl.BlockSpec((1, 2, tl, fw), lambda b, t: (b, 0, t, 0)),
                   pl.BlockSpec((1, tl, fw), lambda b, t: (b, t, 0)),
                   pl.BlockSpec((1, tl, MLA_WIDTH), lambda b, t: (b, t, 0)),
                   pl.BlockSpec((1, MLA_HEADS, hp, tl), lambda b, t: (b, 0, 0, t)),
                   pl.BlockSpec((1, MLA_HEADS, tl, hp), lambda b, t: (b, 0, t, 0)),
                   pl.BlockSpec((1, MLA_WIDTH, tl), lambda b, t: (b, 0, t))],
        out_shape=[jax.ShapeDtypeStruct((bsz, 2, seq, fw), BF16),
                   jax.ShapeDtypeStruct((bsz, seq, fw), BF16),
                   jax.ShapeDtypeStruct((bsz, seq, MLA_WIDTH), BF16),
                   jax.ShapeDtypeStruct((bsz, MLA_HEADS, hp, seq), BF16),
                   jax.ShapeDtypeStruct((bsz, MLA_HEADS, seq, hp), BF16),
                   jax.ShapeDtypeStruct((bsz, MLA_WIDTH, seq), BF16)],
        compiler_params=pltpu.CompilerParams(dimension_semantics=("parallel", "arbitrary"),
                                             vmem_limit_bytes=VMEM_LIMIT),
        name="cd_proj",
    )(x, mod, w1, cs, q_norm, wqt, kv_norm, wk, wvt, cosqt, sinqt, cosk, sink)


DFT_FINE = 32
FOLD_BLOCK = MXU_TILE
DFT_BLOCK = MXU_TILE
DFT_AFTER_HEADS = (1, 3)
KEY_CHUNK = MXU_TILE
SCORE_LEAD = 3
TILES_PER_STEP = 4


def _fold_rows(s, op):
    while s.shape[0] > SUBLANES:
        half = s.shape[0] // 2
        s = op(s[:half], s[half:])
    return s


def _fold_sequence(xcs_ref, mirror_ref, fold_ref, seq_len):
    half = seq_len // 2
    blk = FOLD_BLOCK
    n_blk = seq_len // blk
    first_row = lax.broadcasted_iota(jnp.int32, (BF16_SUBLANES, FNET_WIDTH), 0) == 0
    for part, sign in ((0, 1.0), (1, -1.0)):
        base = part * seq_len
        for b in range(half // blk):
            low = xcs_ref[0, base + b * blk:base + (b + 1) * blk, :].astype(F32)
            upper = xcs_ref[0, base + (n_blk - 1 - b) * blk:base + (n_blk - b) * blk, :]
            folded = low + sign * _dot(mirror_ref[...], upper)
            if b > 0 or part == 1:
                src = base + (n_blk - b) * blk if b > 0 else half
                partner = xcs_ref[0, src:src + BF16_SUBLANES, :].astype(F32)
                head = low[0:BF16_SUBLANES] + sign * partner if b > 0 else partner
                folded = jnp.concatenate(
                    [jnp.where(first_row, head, folded[0:BF16_SUBLANES]), folded[BF16_SUBLANES:]], axis=0)
            fold_ref[part * half + b * blk:part * half + (b + 1) * blk, :] = folded.astype(BF16)


def _cd_main_kernel(qt_ref, kc_ref, kl_ref, vtc_ref, vtl_ref, xcs_ref, mirror_ref, ac_ref, as_ref,
                    bc_ref, bs_ref, gf_ref, gd_ref, x_ref, mod_ref, fw_ref, wout_ref, lng_ref, lnb_ref,
                    o_ref, cls_ref, fold_ref, attnt_ref, *, tq, seq_len):
    lc = kc_ref.shape[2]
    n_chunks = (lc + seq_len) // KEY_CHUNK
    ctx_chunks = lc // KEY_CHUNK
    half = seq_len // 2
    tiles = qt_ref.shape[3] // tq

    def chunk_of(ctx_ref, lat_ref, hd, c, lanes):
        ref, c = (ctx_ref, c) if c < ctx_chunks else (lat_ref, c - ctx_chunks)
        span = slice(c * KEY_CHUNK, (c + 1) * KEY_CHUNK)
        return ref[0, hd, :, span] if lanes else ref[0, hd, span, :]

    stream = [(tile, hd, c) for tile in range(tiles) for hd in range(MLA_HEADS) for c in range(n_chunks)]

    def scores_of(g):
        tile, hd, c = stream[g]
        q_t = qt_ref[0, hd, :, tile * tq:(tile + 1) * tq]
        return _dot(chunk_of(kc_ref, kl_ref, hd, c, False), q_t)

    ones_rows = jnp.ones((BF16_SUBLANES, KEY_CHUNK), BF16)
    ready = {g: scores_of(g) for g in range(SCORE_LEAD)}

    def head_pass(g0, hd):
        m, acc = None, None
        for c in range(n_chunks):
            g = g0 + c
            if g + SCORE_LEAD < len(stream):
                ready[g + SCORE_LEAD] = scores_of(g + SCORE_LEAD)
            s = ready.pop(g)
            top = jnp.max(_fold_rows(s, jnp.maximum), axis=0, keepdims=True)
            m_new = top if m is None else jnp.maximum(m, top)
            p = jnp.exp2(s - m_new)
            vt_ext = jnp.concatenate([chunk_of(vtc_ref, vtl_ref, hd, c, True), ones_rows], axis=0)
            o_t = _dot(vt_ext, p.astype(BF16))
            acc = o_t if acc is None else acc * jnp.exp2(m - m_new) + o_t
            m = m_new
        attnt_ref[hd] = acc[0:V_DIM] * (1.0 / acc[V_DIM:V_DIM + 1])

    @pl.when(pl.program_id(1) == 0)
    def _():
        _fold_sequence(xcs_ref, mirror_ref, fold_ref, seq_len)

    _, _, gate = _split_mod(mod_ref)
    bc, bs = bc_ref[...], bs_ref[...]
    coarse = tq // DFT_FINE
    for tile in range(tiles):
        span = slice(tile * tq, (tile + 1) * tq)
        for i in range(coarse):
            ac = ac_ref[tile * coarse + i:tile * coarse + i + 1, :]
            asn = as_ref[tile * coarse + i:tile * coarse + i + 1, :]
            rows = slice(i * DFT_FINE, (i + 1) * DFT_FINE)
            cls_ref[tile, rows, 0:half] = (ac * bc - asn * bs).astype(BF16)
            cls_ref[tile, rows, half:2 * half] = (asn * bc + ac * bs).astype(BF16)
        specs, yc = [], None
        for hd in range(MLA_HEADS):
            head_pass((tile * MLA_HEADS + hd) * n_chunks, hd)
            if hd in DFT_AFTER_HEADS:
                j = DFT_AFTER_HEADS.index(hd)
                cols = slice(j * DFT_BLOCK, (j + 1) * DFT_BLOCK)
                specs.append(_dot(cls_ref[tile], fold_ref[:, cols]).astype(BF16))
            if hd == DFT_AFTER_HEADS[-1] + 1:
                yc = _dot(jnp.concatenate(specs, axis=-1), fw_ref[...]) * gf_ref[0, span, :].astype(F32)
        yd = attnt_ref[...].reshape(MLA_WIDTH, tq).T * gd_ref[0, span, :].astype(F32)
        y = _dot(jnp.concatenate([yc.astype(BF16), yd.astype(BF16)], axis=-1), wout_ref[...])
        o_ref[0, span, :] = _deep_norm_rows(x_ref[0, span, :], y, gate * (1.0 / ALPHA), lng_ref[...], lnb_ref[...])


def _cd_main(qt, kc, kl, vtc, vtl, xcs, mirror, a_cos, a_sin, b_cos, b_sin, gf, gd, x, mod, fnet_w, w_out,
             ln_g, ln_b, *, tq):
    bsz, seq, d = x.shape
    lc = kc.shape[2]
    hp = HEAD_PAD
    nh = MLA_HEADS
    per_b4 = lambda b, t: (b, 0, 0, 0)
    per_b3 = lambda b, t: (b, 0, 0)
    kern = functools.partial(_cd_main_kernel, tq=tq, seq_len=seq)
    tiles = TILES_PER_STEP
    rows = tiles * tq
    coarse = rows // DFT_FINE
    return pl.pallas_call(
        kern,
        grid=(bsz, seq // rows),
        in_specs=[pl.BlockSpec((1, nh, hp, rows), lambda b, t: (b, 0, 0, t)),
                  pl.BlockSpec((1, nh, lc, hp), per_b4),
                  pl.BlockSpec((1, nh, seq, hp), per_b4),
                  pl.BlockSpec((1, nh, V_DIM, lc), per_b4),
                  pl.BlockSpec((1, nh, V_DIM, seq), per_b4),
                  pl.BlockSpec((1, 2 * seq, FNET_WIDTH), per_b3),
                  _const_spec(mirror),
                  pl.BlockSpec((coarse, seq // 2), lambda b, t: (t, 0)),
                  pl.BlockSpec((coarse, seq // 2), lambda b, t: (t, 0)),
                  _const_spec(b_cos), _const_spec(b_sin),
                  pl.BlockSpec((1, rows, FNET_WIDTH), lambda b, t: (b, t, 0)),
                  pl.BlockSpec((1, rows, MLA_WIDTH), lambda b, t: (b, t, 0)),
                  pl.BlockSpec((1, rows, d), lambda b, t: (b, t, 0)),
                  pl.BlockSpec((1, 1, 3 * d), lambda b, t: (b, 0, 0)),
                  _const_spec(fnet_w), _const_spec(w_out), _const_spec(ln_g), _const_spec(ln_b)],
        out_specs=pl.BlockSpec((1, rows, d), lambda b, t: (b, t, 0)),
        out_shape=jax.ShapeDtypeStruct(x.shape, F32),
        scratch_shapes=[pltpu.VMEM((tiles, tq, seq), BF16),
                        pltpu.VMEM((seq, FNET_WIDTH), BF16),
                        pltpu.VMEM((nh, V_DIM, tq), F32)],
        compiler_params=pltpu.CompilerParams(dimension_semantics=("parallel", "arbitrary"),
                                             vmem_limit_bytes=VMEM_LIMIT),
        name="cd_main",
    )(qt, kc, kl, vtc, vtl, xcs, mirror, a_cos, a_sin, b_cos, b_sin, gf, gd, x, mod, fnet_w, w_out, ln_g, ln_b)


def _rope_tables(seq):
    pos = np.arange(seq)
    row = (pos // GRID_W).astype(np.float64)
    col = (pos % GRID_W).astype(np.float64)
    half = ROPE_AXIS // 2
    inv = (ROPE_BASE ** (-np.arange(0, ROPE_AXIS, 2, dtype=np.float32) / ROPE_AXIS)).astype(np.float32)
    cos = np.ones((seq, LANES), np.float32)
    sin = np.zeros((seq, LANES), np.float32)
    for j in range(QK_ROPE):
        coord = row if j < ROPE_AXIS else col
        ang = (coord.astype(np.float32) * inv[j % half]).astype(np.float32)
        cos[:, QK_NOPE + j] = np.cos(ang)
        sign = -1.0 if (j % ROPE_AXIS) < half else 1.0
        sin[:, QK_NOPE + j] = sign * np.sin(ang)
    return cos, sin


def _rope_partner(j):
    half = ROPE_AXIS // 2
    return j + half if (j % ROPE_AXIS) < half else j - half


def _dft_tables(seq):
    l = np.arange(seq // 2, dtype=np.float64)
    k1 = np.arange(seq // DFT_FINE, dtype=np.float64)[:, None]
    k0 = np.arange(DFT_FINE, dtype=np.float64)[:, None]
    ang_a = 2.0 * np.pi * ((k1 * DFT_FINE * l) % seq) / seq
    ang_b = 2.0 * np.pi * ((k0 * l) % seq) / seq
    norm = 1.0 / np.sqrt(seq * GROUP_DIM)
    b_sin = np.sin(ang_b) * norm
    b_sin[:, 0] = np.where(np.arange(DFT_FINE) % 2 == 0, norm, -norm)
    return (np.cos(ang_a).astype(np.float32), np.sin(ang_a).astype(np.float32),
            (np.cos(ang_b) * norm).astype(np.float32), b_sin.astype(np.float32))


def _mirror_matrix():
    m = np.zeros((FOLD_BLOCK, FOLD_BLOCK), np.float32)
    i = np.arange(1, FOLD_BLOCK)
    m[i, FOLD_BLOCK - i] = 1.0
    return m


def _channel_dft():
    c = np.arange(GROUP_DIM, dtype=np.float64)
    ang = 2.0 * np.pi * ((c[:, None] * c[None, :]) % GROUP_DIM) / GROUP_DIM
    return np.concatenate([np.cos(ang), -np.sin(ang)], axis=1).astype(np.float32)


def kernel(x, c, ctx, c_ctx, ab_w_mod, ab_b_mod, ab_w_in, ab_pool_w, ab_pool_scale, ab_sgu_w, ab_sgu_b,
           ab_w_out, ab_ln_g, ab_ln_b, cd_w_mod, cd_b_mod, cd_w_in, cd_fnet_w, cd_q_norm, cd_kv_norm,
           cd_w_q_up, cd_w_kv_up, cd_w_out, cd_ln_g, cd_ln_b):
    bsz, seq, d = x.shape
    nh, hp = MLA_HEADS, HEAD_PAD
    ctx_len = ctx.shape[1]
    assert d == D_MODEL and seq % ROW_TILE == 0 and seq % (2 * FOLD_BLOCK) == 0 and seq % GRID_W == 0
    assert seq % (TILES_PER_STEP * QUERY_TILE) == 0 and QUERY_TILE % DFT_FINE == 0
    assert FNET_WIDTH == len(DFT_AFTER_HEADS) * DFT_BLOCK
    assert ctx_len % KEY_CHUNK == 0 and ctx_len % CHUNK == 0 and ROW_TILE % OUT_SLAB == 0 and seq % KEY_CHUNK == 0
    assert max(POOL_WINDOWS) // 2 <= SUBLANES <= HALO

    rows = -(-(bsz + 1) // 8) * 8
    cond = jnp.zeros((rows, d), F32).at[:bsz].set(c).at[bsz].set(c_ctx)
    mod_ab = _modulation(cond, ab_w_mod[0], ab_b_mod[0])
    mod_cd = _modulation(cond, cd_w_mod[0], cd_b_mod[0])
    lat_row = lambda b: b
    ctx_row = lambda b: bsz

    ab_args = (ab_w_in[0].astype(BF16), ab_pool_w[0].astype(BF16), ab_pool_scale[0].reshape(1, -1),
               ab_sgu_w[0].astype(BF16),
               jnp.broadcast_to(ab_sgu_b[0][:, :, None], (SGU_HEADS, CHUNK, GROUP_DIM)),
               ab_w_out[0].astype(BF16), ab_ln_g[0].reshape(1, -1), ab_ln_b[0].reshape(1, -1))
    x1 = _ab_layer(x, mod_ab, lat_row, *ab_args, tl=ROW_TILE)
    ctx1 = _ab_layer(ctx, mod_ab, ctx_row, *ab_args, tl=ctx.shape[1])

    w = cd_w_in[0]
    o_cq = 2 * FNET_WIDTH
    o_ckv = o_cq + Q_LORA
    o_kr = o_ckv + KV_LORA
    o_dg = o_kr + QK_ROPE
    partner = np.array([_rope_partner(j) for j in range(QK_ROPE)])
    zeros = lambda n: jnp.zeros((d, n), F32)
    kr_cols = w[:, o_kr:o_dg]
    kr_lanes = jnp.concatenate([zeros(QK_NOPE), kr_cols, zeros(LANES - QK_NOPE - QK_ROPE)], axis=1)
    kr_both = jnp.concatenate([kr_cols[:, partner], zeros(QK_NOPE - QK_ROPE), kr_cols,
                               zeros(LANES - QK_NOPE - QK_ROPE)], axis=1)
    w1 = jnp.concatenate([w[:, 0:o_cq], w[:, o_dg:], w[:, o_cq:o_kr], kr_both], axis=1).astype(BF16)
    w_kv_ctx = jnp.concatenate([w[:, o_ckv:o_kr], kr_lanes], axis=1).astype(BF16)

    wq = cd_w_q_up[0].reshape(Q_LORA, nh, QK_NOPE + QK_ROPE)
    wqt = jnp.concatenate([wq.reshape(Q_LORA, -1), wq[:, :, QK_NOPE + partner].reshape(Q_LORA, -1)],
                          axis=1).T.astype(BF16)

    wkv = cd_w_kv_up[0].reshape(KV_LORA, nh, QK_NOPE + V_DIM)
    wk_pad = jnp.concatenate([wkv[:, :, :QK_NOPE], jnp.zeros((KV_LORA, nh, hp - QK_NOPE), F32)], axis=-1)
    lane = np.arange(hp)
    sel = np.diag(((lane >= QK_NOPE) & (lane < QK_NOPE + QK_ROPE)).astype(np.float32))
    wk = jnp.concatenate([wk_pad.reshape(KV_LORA, nh * hp),
                          jnp.asarray(np.tile(sel, (1, nh)))], axis=0).astype(BF16)
    wvt = wkv[:, :, QK_NOPE:].reshape(KV_LORA, nh * V_DIM).T.astype(BF16)

    q_norm = cd_q_norm[0].reshape(1, -1)
    kv_norm = cd_kv_norm[0].reshape(1, -1)
    cos, sin = _rope_tables(seq)
    rope_lanes = slice(QK_NOPE, QK_NOPE + QK_ROPE)
    cosq_t = (cos[:, rope_lanes] * np.float32(Q_SCALE)).T.copy()
    sinq_t = (sin[:, rope_lanes] * np.float32(Q_SCALE)).T.copy()
    cs = jnp.asarray(_channel_dft()).astype(BF16)

    kc, vtc = _cd_proj_ctx(ctx1, mod_cd, ctx_row, w_kv_ctx, kv_norm, wk, wvt)
    xcs, gf, gd, qt, kl, vtl = _cd_proj(x1, mod_cd, w1, cs, q_norm, wqt, kv_norm, wk, wvt,
                                        jnp.asarray(cosq_t), jnp.asarray(sinq_t),
                                        jnp.asarray(cos), jnp.asarray(sin), tl=ROW_TILE)
    tq = QUERY_TILE
    a_cos, a_sin, b_cos, b_sin = (jnp.asarray(t) for t in _dft_tables(seq))
    mirror = jnp.asarray(_mirror_matrix()).astype(BF16)
    vtc = vtc.reshape(bsz, nh, V_DIM, -1)
    vtl = vtl.reshape(bsz, nh, V_DIM, -1)
    return _cd_main(qt, kc, kl, vtc, vtl, xcs.reshape(bsz, 2 * seq, FNET_WIDTH), mirror,
                    a_cos, a_sin, b_cos, b_sin,
                    gf, gd, x1, mod_cd, cd_fnet_w[0].astype(BF16), cd_w_out[0].astype(BF16),
                    cd_ln_g[0].reshape(1, -1), cd_ln_b[0].reshape(1, -1), tq=tq)
```

```python
import functools

import numpy as np
import jax
import jax.numpy as jnp
from jax import lax
from jax.experimental import pallas as pl
from jax.experimental.pallas import tpu as pltpu

D_MODEL = 1024
GRID_W = 64
EPS = 1e-6
DEPTH = 2
POOL_WINDOWS = (2, 4, 8, 16)
GROUP_DIM = 128
POOL_WIDTH = 512
SGU_HEADS = 4
SGU_WIDTH = 512
CHUNK = 128
FNET_HEADS = 4
FNET_WIDTH = 512
MLA_HEADS = 8
QK_NOPE = 64
QK_ROPE = 32
V_DIM = 64
Q_LORA = 256
KV_LORA = 128
MLA_WIDTH = 512
ROPE_AXIS = QK_ROPE // 2
ROPE_BASE = 10000.0
ALPHA = (2 * DEPTH) ** 0.25
Q_SCALE = float((QK_NOPE + QK_ROPE) ** -0.5 * np.log2(np.e))

LANES = 128
SUBLANES = 8
BF16_SUBLANES = 16
MXU_TILE = 256
V7X_VMEM_BYTES = 64 * 1024 * 1024
VMEM_LIMIT = V7X_VMEM_BYTES * 7 // 8

HALO = BF16_SUBLANES
HEAD_PAD = LANES
OUT_SLAB = 2 * MXU_TILE
ROW_TILE = 1024
QUERY_TILE = 256
MOD_COL_TILE = 768

F32 = jnp.float32
BF16 = jnp.bfloat16


def _silu(v):
    half = 0.5 * v
    return half * (1.0 + jnp.tanh(half))


def _dot(a, b):
    return jnp.dot(a, b, preferred_element_type=F32)


def _deep_norm_rows(x, y, gate_scaled, g, b):
    r = x + gate_scaled * y
    mu = jnp.mean(r, axis=-1, keepdims=True)
    rc = r - mu
    var = jnp.mean(rc * rc, axis=-1, keepdims=True)
    return rc * lax.rsqrt(var + EPS / (ALPHA * ALPHA)) * g + b


def _rms_rows(v, g):
    return v * lax.rsqrt(jnp.mean(v * v, axis=-1, keepdims=True) + EPS) * g


def _mod_kernel(c_ref, w_ref, b_ref, o_ref):
    s = _silu(c_ref[...]).astype(BF16)
    o_ref[...] = _dot(s, w_ref[...].astype(BF16)) + b_ref[...]


def _modulation(cond, w_mod, b_mod):
    rows, d = cond.shape
    n = w_mod.shape[1]
    tn = MOD_COL_TILE
    out = pl.pallas_call(
        _mod_kernel,
        grid=(n // tn,),
        in_specs=[pl.BlockSpec((rows, d), lambda j: (0, 0)),
                  pl.BlockSpec((d, tn), lambda j: (0, j)),
                  pl.BlockSpec((1, tn), lambda j: (0, j))],
        out_specs=pl.BlockSpec((rows, tn), lambda j: (0, j)),
        out_shape=jax.ShapeDtypeStruct((rows, n), F32),
        name="modulation",
    )(cond, w_mod, b_mod.reshape(1, n))
    return out.reshape(rows, 1, n)


def _split_mod(mod_ref):
    d = D_MODEL
    return mod_ref[0, :, 0:d], 1.0 + mod_ref[0, :, d:2 * d], mod_ref[0, :, 2 * d:3 * d]


def _ab_kernel(x_ref, xp_ref, xn_ref, mod_ref, win_ref, poolw_ref, pscale_ref, sguw_ref, sgub_ref,
               wout_ref, lng_ref, lnb_ref, o_ref, h_ref, a_ref, mix_ref, *, tl, seq_len):
    t = pl.program_id(1)
    nt = pl.num_programs(1)
    shift, scale1, gate = _split_mod(mod_ref)
    x = x_ref[0]
    h_ref[HALO:HALO + tl, :] = (x * scale1 + shift).astype(BF16)
    h_ref[0:HALO, :] = (xp_ref[0] * scale1 + shift).astype(BF16)
    h_ref[HALO + tl:2 * HALO + tl, :] = (xn_ref[0] * scale1 + shift).astype(BF16)

    a_ext = _dot(h_ref[...], win_ref[:, 0:POOL_WIDTH])
    a_ref[HALO:HALO + tl, :] = a_ext[HALO:HALO + tl]
    a_ref[0:HALO, :] = jnp.where(t > 0, a_ext[0:HALO], 0.0)
    a_ref[HALO + tl:2 * HALO + tl, :] = jnp.where(t < nt - 1, a_ext[HALO + tl:2 * HALO + tl], 0.0)

    base = 2 * POOL_WIDTH
    slab = min(OUT_SLAB, tl)
    slabs = [slice(r0, r0 + slab) for r0 in range(0, tl, slab)]

    def wide_projections(rows):
        h = h_ref[HALO + rows.start:HALO + rows.stop, :]
        v = _dot(h, win_ref[:, base + SGU_WIDTH:base + 2 * SGU_WIDTH])
        gate_a = _silu(_dot(h, win_ref[:, POOL_WIDTH:2 * POOL_WIDTH]))
        return v, gate_a

    def mix_and_project(rows, v, gate_a):
        n = rows.stop - rows.start
        h = h_ref[HALO + rows.start:HALO + rows.stop, :]
        u = _dot(h, win_ref[:, base:base + SGU_WIDTH])
        gate_b = _silu(_dot(h, win_ref[:, base + 2 * SGU_WIDTH:base + 3 * SGU_WIDTH]))
        pos = t * tl + rows.start + lax.broadcasted_iota(jnp.int32, (SUBLANES, LANES), 0)
        for g, w in enumerate(POOL_WINDOWS):
            cols = slice(g * GROUP_DIM, (g + 1) * GROUP_DIM)
            first = HALO + rows.start - w // 2
            s = a_ref[first:first + n, cols]
            for j in range(1, w):
                s = s + a_ref[first + j:first + j + n, cols]
            edges = []
            for r0 in (0, n - SUBLANES):
                p = pos + r0
                count = jnp.minimum(p + (w - w // 2), seq_len) - jnp.maximum(p - w // 2, 0)
                edges.append(s[r0:r0 + SUBLANES] / count.astype(F32))
            pooled = jnp.concatenate([edges[0], s[SUBLANES:n - SUBLANES] * (1.0 / w), edges[1]], axis=0)
            diff = pooled - a_ref[HALO + rows.start:HALO + rows.stop, cols]
            ya = _dot(diff.astype(BF16), poolw_ref[g]) * pscale_ref[:, cols] * gate_a[:, cols]
            mix_ref[rows, cols] = ya.astype(BF16)

        for hd in range(SGU_HEADS):
            cols = slice(hd * GROUP_DIM, (hd + 1) * GROUP_DIM)
            vh = v[:, cols]
            mu = jnp.mean(vh, axis=-1, keepdims=True)
            vc = vh - mu
            var = jnp.mean(vc * vc, axis=-1, keepdims=True)
            vn = (vc * lax.rsqrt(var + EPS)).astype(BF16)
            f = jnp.concatenate(
                [_dot(sguw_ref[hd], vn[ci * CHUNK:(ci + 1) * CHUNK, :]) + sgub_ref[hd]
                 for ci in range(n // CHUNK)], axis=0)
            yb = u[:, cols] * f * gate_b[:, cols]
            mix_ref[rows, POOL_WIDTH + hd * GROUP_DIM:POOL_WIDTH + (hd + 1) * GROUP_DIM] = yb.astype(BF16)

        y = _dot(mix_ref[rows, :], wout_ref[...])
        o_ref[0, rows, :] = _deep_norm_rows(x_ref[0, rows, :], y, gate * (1.0 / ALPHA), lng_ref[...], lnb_ref[...])

    pending = None
    for rows in slabs:
        wide = wide_projections(rows)
        if pending is not None:
            mix_and_project(*pending)
        pending = (rows,) + wide
    mix_and_project(*pending)


def _ab_layer(x, mod, mod_row, w_in, pool_w, pool_scale, sgu_w, sgu_b, w_out, ln_g, ln_b, *, tl):
    bsz, seq, d = x.shape
    nt = seq // tl
    hb = tl // HALO
    last_hb = seq // HALO - 1
    const2 = lambda b, t: (0, 0)
    const3 = lambda b, t: (0, 0, 0)
    kern = functools.partial(_ab_kernel, tl=tl, seq_len=seq)
    return pl.pallas_call(
        kern,
        grid=(bsz, nt),
        in_specs=[
            pl.BlockSpec((1, tl, d), lambda b, t: (b, t, 0)),
            pl.BlockSpec((1, HALO, d), lambda b, t: (b, jnp.maximum(t * hb - 1, 0), 0)),
            pl.BlockSpec((1, HALO, d), lambda b, t: (b, jnp.minimum((t + 1) * hb, last_hb), 0)),
            pl.BlockSpec((1, 1, 3 * d), lambda b, t: (mod_row(b), 0, 0)),
            pl.BlockSpec(w_in.shape, const2),
            pl.BlockSpec(pool_w.shape, const3),
            pl.BlockSpec(pool_scale.shape, const2),
            pl.BlockSpec(sgu_w.shape, const3),
            pl.BlockSpec(sgu_b.shape, const3),
            pl.BlockSpec(w_out.shape, const2),
            pl.BlockSpec(ln_g.shape, const2),
            pl.BlockSpec(ln_b.shape, const2),
        ],
        out_specs=pl.BlockSpec((1, tl, d), lambda b, t: (b, t, 0)),
        out_shape=jax.ShapeDtypeStruct(x.shape, F32),
        scratch_shapes=[pltpu.VMEM((tl + 2 * HALO, d), BF16),
                        pltpu.VMEM((tl + 2 * HALO, POOL_WIDTH), F32),
                        pltpu.VMEM((tl, POOL_WIDTH + SGU_WIDTH), BF16)],
        compiler_params=pltpu.CompilerParams(dimension_semantics=("parallel", "arbitrary"),
                                             vmem_limit_bytes=VMEM_LIMIT),
        name="ab_layer",
    )(x, x, x, mod, w_in, pool_w, pool_scale, sgu_w, sgu_b, w_out, ln_g, ln_b)


def _dot_nt(a, b):
    return lax.dot_general(a, b, (((1,), (1,)), ((), ())), preferred_element_type=F32)


def _store_kv(ckv, kr_lanes, kvnorm_ref, wk_ref, wvt_ref, k_ref, vt_ref, e=0):
    ckv_n = _rms_rows(ckv, kvnorm_ref[...]).astype(BF16)
    hk = jnp.concatenate([ckv_n, kr_lanes.astype(BF16)], axis=-1)
    kk = _dot(hk, wk_ref[...]).astype(BF16)
    for hd in range(MLA_HEADS):
        k_ref[e, hd] = kk[:, hd * HEAD_PAD:(hd + 1) * HEAD_PAD]
    vt_ref[e] = _dot_nt(wvt_ref[...], ckv_n).astype(BF16)


def _ab_ctx_kv_kernel(x_ref, xp_ref, xn_ref, mod_ref, win_ref, poolw_ref, pscale_ref, sguw_ref, sgub_ref,
                      wout_ref, lng_ref, lnb_ref, modcd_ref, wkv_ref, kvnorm_ref, wk_ref, wvt_ref,
                      k_ref, vt_ref, h_ref, a_ref, mix_ref, x1_ref, *, tl, seq_len):
    _ab_kernel(x_ref, xp_ref, xn_ref, mod_ref, win_ref, poolw_ref, pscale_ref, sguw_ref, sgub_ref,
               wout_ref, lng_ref, lnb_ref, x1_ref, h_ref, a_ref, mix_ref, tl=tl, seq_len=seq_len)
    shift, scale1, _ = _split_mod(modcd_ref)
    h = (x1_ref[0] * scale1 + shift).astype(BF16)
    z = _dot(h, wkv_ref[...])
    _store_kv(z[:, 0:KV_LORA], z[:, KV_LORA:KV_LORA + LANES], kvnorm_ref, wk_ref, wvt_ref, k_ref, vt_ref)


def _ab_ctx_kv(ctx, mod_ab, mod_cd, mod_row, w_in, pool_w, pool_scale, sgu_w, sgu_b, w_out, ln_g, ln_b,
               w_kv, kv_norm, wk, wvt):
    bsz, seq, d = ctx.shape
    hp = HEAD_PAD
    last_hb = seq // HALO - 1
    row_spec = pl.BlockSpec((1, 1, 3 * d), lambda b, t: (mod_row(b), 0, 0))
    consts = (w_in, pool_w, pool_scale, sgu_w, sgu_b, w_out, ln_g, ln_b)
    kern = functools.partial(_ab_ctx_kv_kernel, tl=seq, seq_len=seq)
    return pl.pallas_call(
        kern,
        grid=(bsz, 1),
        in_specs=[pl.BlockSpec((1, seq, d), lambda b, t: (b, 0, 0)),
                  pl.BlockSpec((1, HALO, d), lambda b, t: (b, 0, 0)),
                  pl.BlockSpec((1, HALO, d), lambda b, t: (b, last_hb, 0)),
                  row_spec] + [_const_spec(a) for a in consts]
                 + [row_spec, _const_spec(w_kv), _const_spec(kv_norm), _const_spec(wk), _const_spec(wvt)],
        out_specs=[pl.BlockSpec((1, MLA_HEADS, seq, hp), lambda b, t: (b, 0, 0, 0)),
                   pl.BlockSpec((1, MLA_WIDTH, seq), lambda b, t: (b, 0, 0))],
        out_shape=[jax.ShapeDtypeStruct((bsz, MLA_HEADS, seq, hp), BF16),
                   jax.ShapeDtypeStruct((bsz, MLA_WIDTH, seq), BF16)],
        scratch_shapes=[pltpu.VMEM((seq + 2 * HALO, d), BF16),
                        pltpu.VMEM((seq + 2 * HALO, POOL_WIDTH), F32),
                        pltpu.VMEM((seq, POOL_WIDTH + SGU_WIDTH), BF16),
                        pltpu.VMEM((1, seq, d), F32)],
        compiler_params=pltpu.CompilerParams(dimension_semantics=("parallel", "arbitrary"),
                                             vmem_limit_bytes=VMEM_LIMIT),
        name="ab_ctx_kv",
    )(ctx, ctx, ctx, mod_ab, *consts, mod_cd, w_kv, kv_norm, wk, wvt)


def _cd_proj_kernel(x_ref, mod_ref, w_ref, cs_ref, qnorm_ref, wqt_ref, kvnorm_ref, wk_ref, wvt_ref,
                    cosqt_ref, sinqt_ref, cosk_ref, sink_ref,
                    xcs_ref, gf_ref, gd_ref, qt_ref, k_ref, vt_ref):
    shift, scale1, _ = _split_mod(mod_ref)
    h = (x_ref[0] * scale1 + shift).astype(BF16)
    fw = FNET_WIDTH
    o = 2 * fw + MLA_WIDTH
    small = _dot(h, w_ref[:, o:o + Q_LORA + KV_LORA + LANES])
    cq = small[:, 0:Q_LORA]
    ckv = small[:, Q_LORA:Q_LORA + KV_LORA]
    kr = small[:, Q_LORA + KV_LORA:Q_LORA + KV_LORA + LANES]
    f_in = _dot(h, w_ref[:, 0:fw]).astype(BF16)
    gf_ref[0] = _silu(_dot(h, w_ref[:, fw:2 * fw])).astype(BF16)
    gd_ref[0] = _silu(_dot(h, w_ref[:, 2 * fw:2 * fw + MLA_WIDTH])).astype(BF16)

    qn = _rms_rows(cq, qnorm_ref[...]).astype(BF16)
    q2t = _dot_nt(wqt_ref[...], qn)
    head_dim = QK_NOPE + QK_ROPE
    swap0 = MLA_HEADS * head_dim
    cosqt, sinqt = cosqt_ref[...], sinqt_ref[...]
    tl = q2t.shape[1]
    pad = jnp.zeros((HEAD_PAD - head_dim, tl), F32)
    for hd in range(MLA_HEADS):
        nope = q2t[hd * head_dim:hd * head_dim + QK_NOPE, :]
        rope = q2t[hd * head_dim + QK_NOPE:(hd + 1) * head_dim, :]
        rope_sw = q2t[swap0 + hd * QK_ROPE:swap0 + (hd + 1) * QK_ROPE, :]
        qt_ref[0, hd] = jnp.concatenate(
            [nope * Q_SCALE, rope * cosqt + rope_sw * sinqt, pad], axis=0).astype(BF16)

    for hd in range(FNET_HEADS):
        cols = slice(hd * GROUP_DIM, (hd + 1) * GROUP_DIM)
        xcs = _dot(f_in[:, cols], cs_ref[...]).astype(BF16)
        xcs_ref[0, 0, :, cols] = xcs[:, 0:GROUP_DIM]
        xcs_ref[0, 1, :, cols] = xcs[:, GROUP_DIM:2 * GROUP_DIM]

    kr_rot = kr * cosk_ref[...] + pltpu.roll(kr, LANES // 2, 1) * sink_ref[...]
    _store_kv(ckv, kr_rot, kvnorm_ref, wk_ref, wvt_ref, k_ref, vt_ref)


def _const_spec(a):
    nd = a.ndim
    return pl.BlockSpec(a.shape, lambda b, t: (0,) * nd)


def _cd_proj(x, mod, w1, cs, q_norm, wqt, kv_norm, wk, wvt, cosqt, sinqt, cosk, sink, *, tl):
    bsz, seq, d = x.shape
    hp = HEAD_PAD
    fw = FNET_WIDTH
    tab = pl.BlockSpec((tl, LANES), lambda b, t: (t, 0))
    tab_t = pl.BlockSpec((QK_ROPE, tl), lambda b, t: (0, t))
    return pl.pallas_call(
        _cd_proj_kernel,
        grid=(bsz, seq // tl),
        in_specs=[pl.BlockSpec((1, tl, d), lambda b, t: (b, t, 0)),
                  pl.BlockSpec((1, 1, 3 * d), lambda b, t: (b, 0, 0)),
                  _const_spec(w1), _const_spec(cs), _const_spec(q_norm), _const_spec(wqt),
                  _const_spec(kv_norm), _const_spec(wk), _const_spec(wvt), tab_t, tab_t, tab, tab],
        out_specs=[pl.BlockSpec((1, 2, tl, fw), lambda b, t: (b, 0, t, 0)),
                   pl.BlockSpec((1, tl, fw), lambda b, t: (b, t, 0)),
                   pl.BlockSpec((1, tl, MLA_WIDTH), lambda b, t: (b, t, 0)),
                   pl.BlockSpec((1, MLA_HEADS, hp, tl), lambda b, t: (b, 0, 0, t)),
                   pl.BlockSpec((1, MLA_HEADS, tl, hp), lambda b, t: (b, 0, t, 0)),
                   pl.BlockSpec((1, MLA_WIDTH, tl), lambda b, t: (b, 0, t))],
        out_shape=[jax.ShapeDtypeStruct((bsz, 2, seq, fw), BF16),
                   jax.ShapeDtypeStruct((bsz, seq, fw), BF16),
                   jax.ShapeDtypeStruct((bsz, seq, MLA_WIDTH), BF16),
                   jax.ShapeDtypeStruct((bsz, MLA_HEADS, hp, seq), BF16),
                   jax.ShapeDtypeStruct((bsz, MLA_HEADS, seq, hp), BF16),
                   jax.ShapeDtypeStruct((bsz, MLA_WIDTH, seq), BF16)],
        compiler_params=pltpu.CompilerParams(dimension_semantics=("parallel", "arbitrary"),
                                             vmem_limit_bytes=VMEM_LIMIT),
        name="cd_proj",
    )(x, mod, w1, cs, q_norm, wqt, kv_norm, wk, wvt, cosqt, sinqt, cosk, sink)


DFT_FINE = 32
FOLD_BLOCK = MXU_TILE
DFT_BLOCK = MXU_TILE
DFT_AFTER_HEADS = (1, 3)
KEY_CHUNK = MXU_TILE
AHEAD = 4
TILES_PER_STEP = 2


def _fold_rows(s, op):
    while s.shape[0] > SUBLANES:
        half = s.shape[0] // 2
        s = op(s[:half], s[half:])
    return s


def _fold_sequence(xcs_ref, mirror_ref, fold_ref, seq_len):
    half = seq_len // 2
    blk = FOLD_BLOCK
    n_blk = seq_len // blk
    first_row = lax.broadcasted_iota(jnp.int32, (BF16_SUBLANES, FNET_WIDTH), 0) == 0
    for part, sign in ((0, 1.0), (1, -1.0)):
        base = part * seq_len
        for b in range(half // blk):
            low = xcs_ref[0, base + b * blk:base + (b + 1) * blk, :].astype(F32)
            upper = xcs_ref[0, base + (n_blk - 1 - b) * blk:base + (n_blk - b) * blk, :]
            folded = low + sign * _dot(mirror_ref[...], upper)
            if b > 0 or part == 1:
                src = base + (n_blk - b) * blk if b > 0 else half
                partner = xcs_ref[0, src:src + BF16_SUBLANES, :].astype(F32)
                head = low[0:BF16_SUBLANES] + sign * partner if b > 0 else partner
                folded = jnp.concatenate(
                    [jnp.where(first_row, head, folded[0:BF16_SUBLANES]), folded[BF16_SUBLANES:]], axis=0)
            fold_ref[part * half + b * blk:part * half + (b + 1) * blk, :] = folded.astype(BF16)


def _cd_main_kernel(qt_ref, qtn_ref, kc_ref, kl_ref, vtc_ref, vtl_ref, xcs_ref, mirror_ref, ac_ref, as_ref,
                    bc_ref, bs_ref, gf_ref, gd_ref, x_ref, mod_ref, fw_ref, wout_ref, lng_ref, lnb_ref,
                    o_ref, cls_ref, fold_ref, attnt_ref, m_ref, *s_refs, tq, seq_len):
    lc = kc_ref.shape[2]
    n_chunks = (lc + seq_len) // KEY_CHUNK
    ctx_chunks = lc // KEY_CHUNK
    half = seq_len // 2
    tiles = qt_ref.shape[3] // tq
    n_items = tiles * MLA_HEADS

    def query_of(item):
        tile, hd = divmod(item, MLA_HEADS)
        if tile < tiles:
            return qt_ref[0, hd, :, tile * tq:(tile + 1) * tq]
        return qtn_ref[0, hd]

    def chunk_of(ctx_ref, lat_ref, hd, c, lanes):
        ref, c = (ctx_ref, c) if c < ctx_chunks else (lat_ref, c - ctx_chunks)
        span = slice(c * KEY_CHUNK, (c + 1) * KEY_CHUNK)
        return ref[0, hd, :, span] if lanes else ref[0, hd, span, :]

    def score_chunk(s_ref, item, c, m8):
        hd = item % MLA_HEADS
        s = _dot(chunk_of(kc_ref, kl_ref, hd, c, False), query_of(item))
        s_ref[c * KEY_CHUNK:(c + 1) * KEY_CHUNK, :] = s
        part = _fold_rows(s, jnp.maximum)
        return part if m8 is None else jnp.maximum(m8, part)

    ones_rows = jnp.ones((BF16_SUBLANES, KEY_CHUNK), BF16)

    def prob_chunk(s, hd, c, m, acc):
        p = jnp.exp2(s - m)
        vt_ext = jnp.concatenate([chunk_of(vtc_ref, vtl_ref, hd, c, True), ones_rows], axis=0)
        o_t = _dot(vt_ext, p.astype(BF16))
        return o_t if acc is None else acc + o_t

    def item_step(item, m8):
        s_ref, hd = s_refs[item % AHEAD], item % MLA_HEADS
        m = jnp.max(m8, axis=0, keepdims=True)
        m8, acc = None, None
        for c in range(n_chunks):
            s_old = s_ref[c * KEY_CHUNK:(c + 1) * KEY_CHUNK, :]
            m8 = score_chunk(s_ref, item + AHEAD, c, m8)
            acc = prob_chunk(s_old, hd, c, m, acc)
        attnt_ref[hd] = acc[0:V_DIM] * (1.0 / acc[V_DIM:V_DIM + 1])
        return m8

    @pl.when(pl.program_id(1) == 0)
    def _():
        _fold_sequence(xcs_ref, mirror_ref, fold_ref, seq_len)
        m8s = [None] * AHEAD
        for c in range(n_chunks):
            for item in range(AHEAD):
                m8s[item] = score_chunk(s_refs[item], item, c, m8s[item])
        for item in range(AHEAD):
            m_ref[item] = m8s[item]

    _, _, gate = _split_mod(mod_ref)
    bc, bs = bc_ref[...], bs_ref[...]
    coarse = tq // DFT_FINE
    m8s = [m_ref[slot] for slot in range(AHEAD)]
    for tile in range(tiles):
        span = slice(tile * tq, (tile + 1) * tq)
        for i in range(coarse):
            ac = ac_ref[tile * coarse + i:tile * coarse + i + 1, :]
            asn = as_ref[tile * coarse + i:tile * coarse + i + 1, :]
            rows = slice(i * DFT_FINE, (i + 1) * DFT_FINE)
            cls_ref[tile, rows, 0:half] = (ac * bc - asn * bs).astype(BF16)
            cls_ref[tile, rows, half:2 * half] = (asn * bc + ac * bs).astype(BF16)
        specs, yc = [], None
        for hd in range(MLA_HEADS):
            item = tile * MLA_HEADS + hd
            m8s[item % AHEAD] = item_step(item, m8s[item % AHEAD])
            if hd in DFT_AFTER_HEADS:
                j = DFT_AFTER_HEADS.index(hd)
                cols = slice(j * DFT_BLOCK, (j + 1) * DFT_BLOCK)
                specs.append(_dot(cls_ref[tile], fold_ref[:, cols]).astype(BF16))
            if hd == DFT_AFTER_HEADS[-1] + 1:
                yc = _dot(jnp.concatenate(specs, axis=-1), fw_ref[...]) * gf_ref[0, span, :].astype(F32)
        yd = attnt_ref[...].reshape(MLA_WIDTH, tq).T * gd_ref[0, span, :].astype(F32)
        y = _dot(jnp.concatenate([yc.astype(BF16), yd.astype(BF16)], axis=-1), wout_ref[...])
        o_ref[0, span, :] = _deep_norm_rows(x_ref[0, span, :], y, gate * (1.0 / ALPHA), lng_ref[...], lnb_ref[...])
    assert n_items % AHEAD == 0
    for slot in range(AHEAD):
        m_ref[slot] = m8s[slot]


def _cd_main(qt, kc, kl, vtc, vtl, xcs, mirror, a_cos, a_sin, b_cos, b_sin, gf, gd, x, mod, fnet_w, w_out,
             ln_g, ln_b, *, tq):
    bsz, seq, d = x.shape
    lc = kc.shape[2]
    hp = HEAD_PAD
    nh = MLA_HEADS
    per_b4 = lambda b, t: (b, 0, 0, 0)
    per_b3 = lambda b, t: (b, 0, 0)
    kern = functools.partial(_cd_main_kernel, tq=tq, seq_len=seq)
    tiles = TILES_PER_STEP
    rows = tiles * tq
    coarse = rows // DFT_FINE
    last_tile = seq // tq - 1
    return pl.pallas_call(
        kern,
        grid=(bsz, seq // rows),
        in_specs=[pl.BlockSpec((1, nh, hp, rows), lambda b, t: (b, 0, 0, t)),
                  pl.BlockSpec((1, nh, hp, tq), lambda b, t: (b, 0, 0, jnp.minimum(tiles * (t + 1), last_tile))),
                  pl.BlockSpec((1, nh, lc, hp), per_b4),
                  pl.BlockSpec((1, nh, seq, hp), per_b4),
                  pl.BlockSpec((1, nh, V_DIM, lc), per_b4),
                  pl.BlockSpec((1, nh, V_DIM, seq), per_b4),
                  pl.BlockSpec((1, 2 * seq, FNET_WIDTH), per_b3),
                  _const_spec(mirror),
                  pl.BlockSpec((coarse, seq // 2), lambda b, t: (t, 0)),
                  pl.BlockSpec((coarse, seq // 2), lambda b, t: (t, 0)),
                  _const_spec(b_cos), _const_spec(b_sin),
                  pl.BlockSpec((1, rows, FNET_WIDTH), lambda b, t: (b, t, 0)),
                  pl.BlockSpec((1, rows, MLA_WIDTH), lambda b, t: (b, t, 0)),
                  pl.BlockSpec((1, rows, d), lambda b, t: (b, t, 0)),
                  pl.BlockSpec((1, 1, 3 * d), lambda b, t: (b, 0, 0)),
                  _const_spec(fnet_w), _const_spec(w_out), _const_spec(ln_g), _const_spec(ln_b)],
        out_specs=pl.BlockSpec((1, rows, d), lambda b, t: (b, t, 0)),
        out_shape=jax.ShapeDtypeStruct(x.shape, F32),
        scratch_shapes=[pltpu.VMEM((tiles, tq, seq), BF16),
                        pltpu.VMEM((seq, FNET_WIDTH), BF16),
                        pltpu.VMEM((nh, V_DIM, tq), F32),
                        pltpu.VMEM((AHEAD, SUBLANES, tq), F32)]
                       + [pltpu.VMEM((lc + seq, tq), F32)] * AHEAD,
        compiler_params=pltpu.CompilerParams(dimension_semantics=("parallel", "arbitrary"),
                                             vmem_limit_bytes=VMEM_LIMIT),
        name="cd_main",
    )(qt, qt, kc, kl, vtc, vtl, xcs, mirror, a_cos, a_sin, b_cos, b_sin, gf, gd, x, mod, fnet_w, w_out, ln_g, ln_b)


def _rope_tables(seq):
    pos = np.arange(seq)
    row = (pos // GRID_W).astype(np.float64)
    col = (pos % GRID_W).astype(np.float64)
    half = ROPE_AXIS // 2
    inv = (ROPE_BASE ** (-np.arange(0, ROPE_AXIS, 2, dtype=np.float32) / ROPE_AXIS)).astype(np.float32)
    cos = np.ones((seq, LANES), np.float32)
    sin = np.zeros((seq, LANES), np.float32)
    for j in range(QK_ROPE):
        coord = row if j < ROPE_AXIS else col
        ang = (coord.astype(np.float32) * inv[j % half]).astype(np.float32)
        cos[:, QK_NOPE + j] = np.cos(ang)
        sign = -1.0 if (j % ROPE_AXIS) < half else 1.0
        sin[:, QK_NOPE + j] = sign * np.sin(ang)
    return cos, sin


def _rope_partner(j):
    half = ROPE_AXIS // 2
    return j + half if (j % ROPE_AXIS) < half else j - half


def _dft_tables(seq):
    l = np.arange(seq // 2, dtype=np.float64)
    k1 = np.arange(seq // DFT_FINE, dtype=np.float64)[:, None]
    k0 = np.arange(DFT_FINE, dtype=np.float64)[:, None]
    ang_a = 2.0 * np.pi * ((k1 * DFT_FINE * l) % seq) / seq
    ang_b = 2.0 * np.pi * ((k0 * l) % seq) / seq
    norm = 1.0 / np.sqrt(seq * GROUP_DIM)
    b_sin = np.sin(ang_b) * norm
    b_sin[:, 0] = np.where(np.arange(DFT_FINE) % 2 == 0, norm, -norm)
    return (np.cos(ang_a).astype(np.float32), np.sin(ang_a).astype(np.float32),
            (np.cos(ang_b) * norm).astype(np.float32), b_sin.astype(np.float32))


def _mirror_matrix():
    m = np.zeros((FOLD_BLOCK, FOLD_BLOCK), np.float32)
    i = np.arange(1, FOLD_BLOCK)
    m[i, FOLD_BLOCK - i] = 1.0
    return m


def _channel_dft():
    c = np.arange(GROUP_DIM, dtype=np.float64)
    ang = 2.0 * np.pi * ((c[:, None] * c[None, :]) % GROUP_DIM) / GROUP_DIM
    return np.concatenate([np.cos(ang), -np.sin(ang)], axis=1).astype(np.float32)


def kernel(x, c, ctx, c_ctx, ab_w_mod, ab_b_mod, ab_w_in, ab_pool_w, ab_pool_scale, ab_sgu_w, ab_sgu_b,
           ab_w_out, ab_ln_g, ab_ln_b, cd_w_mod, cd_b_mod, cd_w_in, cd_fnet_w, cd_q_norm, cd_kv_norm,
           cd_w_q_up, cd_w_kv_up, cd_w_out, cd_ln_g, cd_ln_b):
    bsz, seq, d = x.shape
    nh, hp = MLA_HEADS, HEAD_PAD
    ctx_len = ctx.shape[1]
    assert d == D_MODEL and seq % ROW_TILE == 0 and seq % (2 * FOLD_BLOCK) == 0 and seq % GRID_W == 0
    assert seq % (TILES_PER_STEP * QUERY_TILE) == 0 and QUERY_TILE % DFT_FINE == 0
    assert FNET_WIDTH == len(DFT_AFTER_HEADS) * DFT_BLOCK and (TILES_PER_STEP * MLA_HEADS) % AHEAD == 0
    assert ctx_len % KEY_CHUNK == 0 and ctx_len % CHUNK == 0 and ROW_TILE % OUT_SLAB == 0 and seq % KEY_CHUNK == 0
    assert max(POOL_WINDOWS) // 2 <= SUBLANES <= HALO and MLA_HEADS % AHEAD == 0

    rows = -(-(bsz + 1) // 8) * 8
    cond = jnp.zeros((rows, d), F32).at[:bsz].set(c).at[bsz].set(c_ctx)
    mod_ab = _modulation(cond, ab_w_mod[0], ab_b_mod[0])
    mod_cd = _modulation(cond, cd_w_mod[0], cd_b_mod[0])
    lat_row = lambda b: b
    ctx_row = lambda b: bsz

    ab_args = (ab_w_in[0].astype(BF16), ab_pool_w[0].astype(BF16), ab_pool_scale[0].reshape(1, -1),
               ab_sgu_w[0].astype(BF16),
               jnp.broadcast_to(ab_sgu_b[0][:, :, None], (SGU_HEADS, CHUNK, GROUP_DIM)),
               ab_w_out[0].astype(BF16), ab_ln_g[0].reshape(1, -1), ab_ln_b[0].reshape(1, -1))
    x1 = _ab_layer(x, mod_ab, lat_row, *ab_args, tl=ROW_TILE)

    w = cd_w_in[0]
    o_cq = 2 * FNET_WIDTH
    o_ckv = o_cq + Q_LORA
    o_kr = o_ckv + KV_LORA
    o_dg = o_kr + QK_ROPE
    partner = np.array([_rope_partner(j) for j in range(QK_ROPE)])
    zeros = lambda n: jnp.zeros((d, n), F32)
    kr_cols = w[:, o_kr:o_dg]
    kr_lanes = jnp.concatenate([zeros(QK_NOPE), kr_cols, zeros(LANES - QK_NOPE - QK_ROPE)], axis=1)
    kr_both = jnp.concatenate([kr_cols[:, partner], zeros(QK_NOPE - QK_ROPE), kr_cols,
                               zeros(LANES - QK_NOPE - QK_ROPE)], axis=1)
    w1 = jnp.concatenate([w[:, 0:o_cq], w[:, o_dg:], w[:, o_cq:o_kr], kr_both], axis=1).astype(BF16)
    w_kv_ctx = jnp.concatenate([w[:, o_ckv:o_kr], kr_lanes], axis=1).astype(BF16)

    wq = cd_w_q_up[0].reshape(Q_LORA, nh, QK_NOPE + QK_ROPE)
    wqt = jnp.concatenate([wq.reshape(Q_LORA, -1), wq[:, :, QK_NOPE + partner].reshape(Q_LORA, -1)],
                          axis=1).T.astype(BF16)

    wkv = cd_w_kv_up[0].reshape(KV_LORA, nh, QK_NOPE + V_DIM)
    wk_pad = jnp.concatenate([wkv[:, :, :QK_NOPE], jnp.zeros((KV_LORA, nh, hp - QK_NOPE), F32)], axis=-1)
    lane = np.arange(hp)
    sel = np.diag(((lane >= QK_NOPE) & (lane < QK_NOPE + QK_ROPE)).astype(np.float32))
    wk = jnp.concatenate([wk_pad.reshape(KV_LORA, nh * hp),
                          jnp.asarray(np.tile(sel, (1, nh)))], axis=0).astype(BF16)
    wvt = wkv[:, :, QK_NOPE:].reshape(KV_LORA, nh * V_DIM).T.astype(BF16)

    q_norm = cd_q_norm[0].reshape(1, -1)
    kv_norm = cd_kv_norm[0].reshape(1, -1)
    cos, sin = _rope_tables(seq)
    rope_lanes = slice(QK_NOPE, QK_NOPE + QK_ROPE)
    cosq_t = (cos[:, rope_lanes] * np.float32(Q_SCALE)).T.copy()
    sinq_t = (sin[:, rope_lanes] * np.float32(Q_SCALE)).T.copy()
    cs = jnp.asarray(_channel_dft()).astype(BF16)

    kc, vtc = _ab_ctx_kv(ctx, mod_ab, mod_cd, ctx_row, *ab_args, w_kv_ctx, kv_norm, wk, wvt)
    xcs, gf, gd, qt, kl, vtl = _cd_proj(x1, mod_cd, w1, cs, q_norm, wqt, kv_norm, wk, wvt,
                                        jnp.asarray(cosq_t), jnp.asarray(sinq_t),
                                        jnp.asarray(cos), jnp.asarray(sin), tl=ROW_TILE)
    tq = QUERY_TILE
    a_cos, a_sin, b_cos, b_sin = (jnp.asarray(t) for t in _dft_tables(seq))
    mirror = jnp.asarray(_mirror_matrix()).astype(BF16)
    vtc = vtc.reshape(bsz, nh, V_DIM, -1)
    vtl = vtl.reshape(bsz, nh, V_DIM, -1)
    return _cd_main(qt, kc, kl, vtc, vtl, xcs.reshape(bsz, 2 * seq, FNET_WIDTH), mirror,
                    a_cos, a_sin, b_cos, b_sin,
                    gf, gd, x1, mod_cd, cd_fnet_w[0].astype(BF16), cd_w_out[0].astype(BF16),
                    cd_ln_g[0].reshape(1, -1), cd_ln_b[0].reshape(1, -1), tq=tq)
```

```python
import functools

import numpy as np
import jax
import jax.numpy as jnp
from jax import lax
from jax.experimental import pallas as pl
from jax.experimental.pallas import tpu as pltpu

D_MODEL = 1024
GRID_W = 64
EPS = 1e-6
DEPTH = 2
POOL_WINDOWS = (2, 4, 8, 16)
GROUP_DIM = 128
POOL_WIDTH = 512
SGU_HEADS = 4
SGU_WIDTH = 512
CHUNK = 128
FNET_HEADS = 4
FNET_WIDTH = 512
MLA_HEADS = 8
QK_NOPE = 64
QK_ROPE = 32
V_DIM = 64
Q_LORA = 256
KV_LORA = 128
MLA_WIDTH = 512
ROPE_AXIS = QK_ROPE // 2
ROPE_BASE = 10000.0
ALPHA = (2 * DEPTH) ** 0.25
Q_SCALE = float((QK_NOPE + QK_ROPE) ** -0.5 * np.log2(np.e))

LANES = 128
SUBLANES = 8
BF16_SUBLANES = 16
MXU_TILE = 256
V7X_VMEM_BYTES = 64 * 1024 * 1024
VMEM_LIMIT = V7X_VMEM_BYTES * 7 // 8

HALO = BF16_SUBLANES
HEAD_PAD = LANES
OUT_SLAB = 2 * MXU_TILE
ROW_TILE = 1024
QUERY_TILE = 256
MOD_COL_TILE = 768

F32 = jnp.float32
BF16 = jnp.bfloat16


def _silu(v):
    half = 0.5 * v
    return half * (1.0 + jnp.tanh(half))


def _dot(a, b):
    return jnp.dot(a, b, preferred_element_type=F32)


def _deep_norm_rows(x, y, gate_scaled, g, b):
    r = x + gate_scaled * y
    mu = jnp.mean(r, axis=-1, keepdims=True)
    rc = r - mu
    var = jnp.mean(rc * rc, axis=-1, keepdims=True)
    return rc * lax.rsqrt(var + EPS / (ALPHA * ALPHA)) * g + b


def _rms_rows(v, g):
    return v * lax.rsqrt(jnp.mean(v * v, axis=-1, keepdims=True) + EPS) * g


def _mod_pair_kernel(c_ref, wa_ref, ba_ref, wb_ref, bb_ref, o_ref):
    s = _silu(c_ref[...]).astype(BF16)

    @pl.when(pl.program_id(0) == 0)
    def _():
        o_ref[0] = _dot(s, wa_ref[...].astype(BF16)) + ba_ref[...]

    @pl.when(pl.program_id(0) == 1)
    def _():
        o_ref[0] = _dot(s, wb_ref[...].astype(BF16)) + bb_ref[...]


def _modulation_pair(cond, w_a, b_a, w_b, b_b):
    rows, d = cond.shape
    n = w_a.shape[1]
    tn = MOD_COL_TILE
    last = n // tn - 1
    out = pl.pallas_call(
        _mod_pair_kernel,
        grid=(2, n // tn),
        in_specs=[pl.BlockSpec((rows, d), lambda l, j: (0, 0)),
                  pl.BlockSpec((d, tn), lambda l, j: (0, jnp.where(l == 0, j, last))),
                  pl.BlockSpec((1, tn), lambda l, j: (0, jnp.where(l == 0, j, last))),
                  pl.BlockSpec((d, tn), lambda l, j: (0, jnp.where(l == 1, j, 0))),
                  pl.BlockSpec((1, tn), lambda l, j: (0, jnp.where(l == 1, j, 0)))],
        out_specs=pl.BlockSpec((1, rows, tn), lambda l, j: (l, 0, j)),
        out_shape=jax.ShapeDtypeStruct((2, rows, n), F32),
        name="modulation",
    )(cond, w_a, b_a.reshape(1, n), w_b, b_b.reshape(1, n))
    return out[0].reshape(rows, 1, n), out[1].reshape(rows, 1, n)


def _split_mod(mod_ref):
    d = D_MODEL
    return mod_ref[0, :, 0:d], 1.0 + mod_ref[0, :, d:2 * d], mod_ref[0, :, 2 * d:3 * d]


def _ab_kernel(x_ref, xp_ref, xn_ref, mod_ref, win_ref, poolw_ref, pscale_ref, sguw_ref, sgub_ref,
               wout_ref, lng_ref, lnb_ref, o_ref, h_ref, a_ref, mix_ref, *, tl, seq_len):
    t = pl.program_id(1)
    nt = pl.num_programs(1)
    shift, scale1, gate = _split_mod(mod_ref)
    x = x_ref[0]
    h_ref[HALO:HALO + tl, :] = (x * scale1 + shift).astype(BF16)
    h_ref[0:HALO, :] = (xp_ref[0] * scale1 + shift).astype(BF16)
    h_ref[HALO + tl:2 * HALO + tl, :] = (xn_ref[0] * scale1 + shift).astype(BF16)

    a_ext = _dot(h_ref[...], win_ref[:, 0:POOL_WIDTH])
    a_ref[HALO:HALO + tl, :] = a_ext[HALO:HALO + tl]
    a_ref[0:HALO, :] = jnp.where(t > 0, a_ext[0:HALO], 0.0)
    a_ref[HALO + tl:2 * HALO + tl, :] = jnp.where(t < nt - 1, a_ext[HALO + tl:2 * HALO + tl], 0.0)

    base = 2 * POOL_WIDTH
    slab = min(OUT_SLAB, tl)
    slabs = [slice(r0, r0 + slab) for r0 in range(0, tl, slab)]

    def wide_projections(rows):
        h = h_ref[HALO + rows.start:HALO + rows.stop, :]
        v = _dot(h, win_ref[:, base + SGU_WIDTH:base + 2 * SGU_WIDTH])
        gate_a = _silu(_dot(h, win_ref[:, POOL_WIDTH:2 * POOL_WIDTH]))
        return v, gate_a

    def mix_and_project(rows, v, gate_a):
        n = rows.stop - rows.start
        h = h_ref[HALO + rows.start:HALO + rows.stop, :]
        u = _dot(h, win_ref[:, base:base + SGU_WIDTH])
        gate_b = _silu(_dot(h, win_ref[:, base + 2 * SGU_WIDTH:base + 3 * SGU_WIDTH]))
        pos = t * tl + rows.start + lax.broadcasted_iota(jnp.int32, (SUBLANES, LANES), 0)
        for g, w in enumerate(POOL_WINDOWS):
            cols = slice(g * GROUP_DIM, (g + 1) * GROUP_DIM)
            first = HALO + rows.start - w // 2
            s = a_ref[first:first + n, cols]
            for j in range(1, w):
                s = s + a_ref[first + j:first + j + n, cols]
            edges = []
            for r0 in (0, n - SUBLANES):
                p = pos + r0
                count = jnp.minimum(p + (w - w // 2), seq_len) - jnp.maximum(p - w // 2, 0)
                edges.append(s[r0:r0 + SUBLANES] / count.astype(F32))
            pooled = jnp.concatenate([edges[0], s[SUBLANES:n - SUBLANES] * (1.0 / w), edges[1]], axis=0)
            diff = pooled - a_ref[HALO + rows.start:HALO + rows.stop, cols]
            ya = _dot(diff.astype(BF16), poolw_ref[g]) * pscale_ref[:, cols] * gate_a[:, cols]
            mix_ref[rows, cols] = ya.astype(BF16)

        for hd in range(SGU_HEADS):
            cols = slice(hd * GROUP_DIM, (hd + 1) * GROUP_DIM)
            vh = v[:, cols]
            mu = jnp.mean(vh, axis=-1, keepdims=True)
            vc = vh - mu
            var = jnp.mean(vc * vc, axis=-1, keepdims=True)
            vn = (vc * lax.rsqrt(var + EPS)).astype(BF16)
            f = jnp.concatenate(
                [_dot(sguw_ref[hd], vn[ci * CHUNK:(ci + 1) * CHUNK, :]) + sgub_ref[hd]
                 for ci in range(n // CHUNK)], axis=0)
            yb = u[:, cols] * f * gate_b[:, cols]
            mix_ref[rows, POOL_WIDTH + hd * GROUP_DIM:POOL_WIDTH + (hd + 1) * GROUP_DIM] = yb.astype(BF16)

        y = _dot(mix_ref[rows, :], wout_ref[...])
        o_ref[0, rows, :] = _deep_norm_rows(x_ref[0, rows, :], y, gate * (1.0 / ALPHA), lng_ref[...], lnb_ref[...])

    pending = None
    for rows in slabs:
        wide = wide_projections(rows)
        if pending is not None:
            mix_and_project(*pending)
        pending = (rows,) + wide
    mix_and_project(*pending)


def _ab_layer(x, mod, mod_row, w_in, pool_w, pool_scale, sgu_w, sgu_b, w_out, ln_g, ln_b, *, tl):
    bsz, seq, d = x.shape
    nt = seq // tl
    hb = tl // HALO
    last_hb = seq // HALO - 1
    const2 = lambda b, t: (0, 0)
    const3 = lambda b, t: (0, 0, 0)
    kern = functools.partial(_ab_kernel, tl=tl, seq_len=seq)
    return pl.pallas_call(
        kern,
        grid=(bsz, nt),
        in_specs=[
            pl.BlockSpec((1, tl, d), lambda b, t: (b, t, 0)),
            pl.BlockSpec((1, HALO, d), lambda b, t: (b, jnp.maximum(t * hb - 1, 0), 0)),
            pl.BlockSpec((1, HALO, d), lambda b, t: (b, jnp.minimum((t + 1) * hb, last_hb), 0)),
            pl.BlockSpec((1, 1, 3 * d), lambda b, t: (mod_row(b), 0, 0)),
            pl.BlockSpec(w_in.shape, const2),
            pl.BlockSpec(pool_w.shape, const3),
            pl.BlockSpec(pool_scale.shape, const2),
            pl.BlockSpec(sgu_w.shape, const3),
            pl.BlockSpec(sgu_b.shape, const3),
            pl.BlockSpec(w_out.shape, const2),
            pl.BlockSpec(ln_g.shape, const2),
            pl.BlockSpec(ln_b.shape, const2),
        ],
        out_specs=pl.BlockSpec((1, tl, d), lambda b, t: (b, t, 0)),
        out_shape=jax.ShapeDtypeStruct(x.shape, F32),
        scratch_shapes=[pltpu.VMEM((tl + 2 * HALO, d), BF16),
                        pltpu.VMEM((tl + 2 * HALO, POOL_WIDTH), F32),
                        pltpu.VMEM((tl, POOL_WIDTH + SGU_WIDTH), BF16)],
        compiler_params=pltpu.CompilerParams(dimension_semantics=("parallel", "arbitrary"),
                                             vmem_limit_bytes=VMEM_LIMIT),
        name="ab_layer",
    )(x, x, x, mod, w_in, pool_w, pool_scale, sgu_w, sgu_b, w_out, ln_g, ln_b)


def _dot_nt(a, b):
    return lax.dot_general(a, b, (((1,), (1,)), ((), ())), preferred_element_type=F32)


def _store_kv(ckv, kr_lanes, kvnorm_ref, wk_ref, wvt_ref, k_ref, vt_ref, e=0):
    ckv_n = _rms_rows(ckv, kvnorm_ref[...]).astype(BF16)
    hk = jnp.concatenate([ckv_n, kr_lanes.astype(BF16)], axis=-1)
    kk = _dot(hk, wk_ref[...]).astype(BF16)
    for hd in range(MLA_HEADS):
        k_ref[e, hd] = kk[:, hd * HEAD_PAD:(hd + 1) * HEAD_PAD]
    vt_ref[e] = _dot_nt(wvt_ref[...], ckv_n).astype(BF16)


def _ab_ctx_kv_kernel(x_ref, xp_ref, xn_ref, mod_ref, win_ref, poolw_ref, pscale_ref, sguw_ref, sgub_ref,
                      wout_ref, lng_ref, lnb_ref, modcd_ref, wkv_ref, kvnorm_ref, wk_ref, wvt_ref,
                      k_ref, vt_ref, h_ref, a_ref, mix_ref, x1_ref, *, tl, seq_len):
    _ab_kernel(x_ref, xp_ref, xn_ref, mod_ref, win_ref, poolw_ref, pscale_ref, sguw_ref, sgub_ref,
               wout_ref, lng_ref, lnb_ref, x1_ref, h_ref, a_ref, mix_ref, tl=tl, seq_len=seq_len)
    shift, scale1, _ = _split_mod(modcd_ref)
    h = (x1_ref[0] * scale1 + shift).astype(BF16)
    z = _dot(h, wkv_ref[...])
    _store_kv(z[:, 0:KV_LORA], z[:, KV_LORA:KV_LORA + LANES], kvnorm_ref, wk_ref, wvt_ref, k_ref, vt_ref)


def _ab_ctx_kv(ctx, mod_ab, mod_cd, mod_row, w_in, pool_w, pool_scale, sgu_w, sgu_b, w_out, ln_g, ln_b,
               w_kv, kv_norm, wk, wvt):
    bsz, seq, d = ctx.shape
    hp = HEAD_PAD
    last_hb = seq // HALO - 1
    row_spec = pl.BlockSpec((1, 1, 3 * d), lambda b, t: (mod_row(b), 0, 0))
    consts = (w_in, pool_w, pool_scale, sgu_w, sgu_b, w_out, ln_g, ln_b)
    kern = functools.partial(_ab_ctx_kv_kernel, tl=seq, seq_len=seq)
    return pl.pallas_call(
        kern,
        grid=(bsz, 1),
        in_specs=[pl.BlockSpec((1, seq, d), lambda b, t: (b, 0, 0)),
                  pl.BlockSpec((1, HALO, d), lambda b, t: (b, 0, 0)),
                  pl.BlockSpec((1, HALO, d), lambda b, t: (b, last_hb, 0)),
                  row_spec] + [_const_spec(a) for a in consts]
                 + [row_spec, _const_spec(w_kv), _const_spec(kv_norm), _const_spec(wk), _const_spec(wvt)],
        out_specs=[pl.BlockSpec((1, MLA_HEADS, seq, hp), lambda b, t: (b, 0, 0, 0)),
                   pl.BlockSpec((1, MLA_WIDTH, seq), lambda b, t: (b, 0, 0))],
        out_shape=[jax.ShapeDtypeStruct((bsz, MLA_HEADS, seq, hp), BF16),
                   jax.ShapeDtypeStruct((bsz, MLA_WIDTH, seq), BF16)],
        scratch_shapes=[pltpu.VMEM((seq + 2 * HALO, d), BF16),
                        pltpu.VMEM((seq + 2 * HALO, POOL_WIDTH), F32),
                        pltpu.VMEM((seq, POOL_WIDTH + SGU_WIDTH), BF16),
                        pltpu.VMEM((1, seq, d), F32)],
        compiler_params=pltpu.CompilerParams(dimension_semantics=("parallel", "arbitrary"),
                                             vmem_limit_bytes=VMEM_LIMIT),
        name="ab_ctx_kv",
    )(ctx, ctx, ctx, mod_ab, *consts, mod_cd, w_kv, kv_norm, wk, wvt)


def _cd_proj_kernel(x_ref, mod_ref, w_ref, cs_ref, qnorm_ref, wqt_ref, kvnorm_ref, wk_ref, wvt_ref,
                    cosqt_ref, sinqt_ref, cosk_ref, sink_ref,
                    xcs_ref, gf_ref, gd_ref, qt_ref, k_ref, vt_ref):
    shift, scale1, _ = _split_mod(mod_ref)
    h = (x_ref[0] * scale1 + shift).astype(BF16)
    fw = FNET_WIDTH
    o = 2 * fw + MLA_WIDTH
    small = _dot(h, w_ref[:, o:o + Q_LORA + KV_LORA + LANES])
    cq = small[:, 0:Q_LORA]
    ckv = small[:, Q_LORA:Q_LORA + KV_LORA]
    kr = small[:, Q_LORA + KV_LORA:Q_LORA + KV_LORA + LANES]
    f_in = _dot(h, w_ref[:, 0:fw]).astype(BF16)
    gf_ref[0] = _silu(_dot(h, w_ref[:, fw:2 * fw])).astype(BF16)
    gd_ref[0] = _silu(_dot(h, w_ref[:, 2 * fw:2 * fw + MLA_WIDTH])).astype(BF16)

    qn = _rms_rows(cq, qnorm_ref[...]).astype(BF16)
    q2t = _dot_nt(wqt_ref[...], qn)
    head_dim = QK_NOPE + QK_ROPE
    swap0 = MLA_HEADS * head_dim
    cosqt, sinqt = cosqt_ref[...], sinqt_ref[...]
    tl = q2t.shape[1]
    pad = jnp.zeros((HEAD_PAD - head_dim, tl), F32)
    for hd in range(MLA_HEADS):
        nope = q2t[hd * head_dim:hd * head_dim + QK_NOPE, :]
        rope = q2t[hd * head_dim + QK_NOPE:(hd + 1) * head_dim, :]
        rope_sw = q2t[swap0 + hd * QK_ROPE:swap0 + (hd + 1) * QK_ROPE, :]
        qt_ref[0, hd] = jnp.concatenate(
            [nope * Q_SCALE, rope * cosqt + rope_sw * sinqt, pad], axis=0).astype(BF16)

    for hd in range(FNET_HEADS):
        cols = slice(hd * GROUP_DIM, (hd + 1) * GROUP_DIM)
        xcs = _dot(f_in[:, cols], cs_ref[...]).astype(BF16)
        xcs_ref[0, 0, :, cols] = xcs[:, 0:GROUP_DIM]
        xcs_ref[0, 1, :, cols] = xcs[:, GROUP_DIM:2 * GROUP_DIM]

    kr_rot = kr * cosk_ref[...] + pltpu.roll(kr, LANES // 2, 1) * sink_ref[...]
    _store_kv(ckv, kr_rot, kvnorm_ref, wk_ref, wvt_ref, k_ref, vt_ref)


def _const_spec(a):
    nd = a.ndim
    return pl.BlockSpec(a.shape, lambda b, t: (0,) * nd)


def _cd_proj(x, mod, w1, cs, q_norm, wqt, kv_norm, wk, wvt, cosqt, sinqt, cosk, sink, *, tl):
    bsz, seq, d = x.shape
    hp = HEAD_PAD
    fw = FNET_WIDTH
    tab = pl.BlockSpec((tl, LANES), lambda b, t: (t, 0))
    tab_t = pl.BlockSpec((QK_ROPE, tl), lambda b, t: (0, t))
    return pl.pallas_call(
        _cd_proj_kernel,
        grid=(bsz, seq // tl),
        in_specs=[pl.BlockSpec((1, tl, d), lambda b, t: (b, t, 0)),
                  pl.BlockSpec((1, 1, 3 * d), lambda b, t: (b, 0, 0)),
                  _const_spec(w1), _const_spec(cs), _const_spec(q_norm), _const_spec(wqt),
                  _const_spec(kv_norm), _const_spec(wk), _const_spec(wvt), tab_t, tab_t, tab, tab],
        out_specs=[pl.BlockSpec((1, 2, tl, fw), lambda b, t: (b, 0, t, 0)),
                   pl.BlockSpec((1, tl, fw), lambda b, t: (b, t, 0)),
                   pl.BlockSpec((1, tl, MLA_WIDTH), lambda b, t: (b, t, 0)),
                   pl.BlockSpec((1, MLA_HEADS, hp, tl), lambda b, t: (b, 0, 0, t)),
                   pl.BlockSpec((1, MLA_HEADS, tl, hp), lambda b, t: (b, 0, t, 0)),
                   pl.BlockSpec((1, MLA_WIDTH, tl), lambda b, t: (b, 0, t))],
        out_shape=[jax.ShapeDtypeStruct((bsz, 2, seq, fw), BF16),
                   jax.ShapeDtypeStruct((bsz, seq, fw), BF16),
                   jax.ShapeDtypeStruct((bsz, seq, MLA_WIDTH), BF16),
                   jax.ShapeDtypeStruct((bsz, MLA_HEADS, hp, seq), BF16),
                   jax.ShapeDtypeStruct((bsz, MLA_HEADS, seq, hp), BF16),
                   jax.ShapeDtypeStruct((bsz, MLA_WIDTH, seq), BF16)],
        compiler_params=pltpu.CompilerParams(dimension_semantics=("parallel", "arbitrary"),
                                             vmem_limit_bytes=VMEM_LIMIT),
        name="cd_proj",
    )(x, mod, w1, cs, q_norm, wqt, kv_norm, wk, wvt, cosqt, sinqt, cosk, sink)


DFT_FINE = 32
FOLD_BLOCK = MXU_TILE
DFT_BLOCK = MXU_TILE
DFT_AFTER_HEADS = (1, 3)
KEY_CHUNK = MXU_TILE
AHEAD = 4
TILES_PER_STEP = 2


def _fold_rows(s, op):
    while s.shape[0] > SUBLANES:
        half = s.shape[0] // 2
        s = op(s[:half], s[half:])
    return s


def _fold_sequence(xcs_ref, mirror_ref, fold_ref, seq_len):
    half = seq_len // 2
    blk = FOLD_BLOCK
    n_blk = seq_len // blk
    first_row = lax.broadcasted_iota(jnp.int32, (BF16_SUBLANES, FNET_WIDTH), 0) == 0
    for part, sign in ((0, 1.0), (1, -1.0)):
        base = part * seq_len
        for b in range(half // blk):
            low = xcs_ref[0, base + b * blk:base + (b + 1) * blk, :].astype(F32)
            upper = xcs_ref[0, base + (n_blk - 1 - b) * blk:base + (n_blk - b) * blk, :]
            folded = low + sign * _dot(mirror_ref[...], upper)
            if b > 0 or part == 1:
                src = base + (n_blk - b) * blk if b > 0 else half
                partner = xcs_ref[0, src:src + BF16_SUBLANES, :].astype(F32)
                head = low[0:BF16_SUBLANES] + sign * partner if b > 0 else partner
                folded = jnp.concatenate(
                    [jnp.where(first_row, head, folded[0:BF16_SUBLANES]), folded[BF16_SUBLANES:]], axis=0)
            fold_ref[part * half + b * blk:part * half + (b + 1) * blk, :] = folded.astype(BF16)


def _cd_main_kernel(qt_ref, qtn_ref, kc_ref, kl_ref, vtc_ref, vtl_ref, xcs_ref, mirror_ref, ac_ref, as_ref,
                    bc_ref, bs_ref, gf_ref, gd_ref, x_ref, mod_ref, fw_ref, wout_ref, lng_ref, lnb_ref,
                    o_ref, cls_ref, fold_ref, attnt_ref, m_ref, *s_refs, tq, seq_len):
    lc = kc_ref.shape[2]
    n_chunks = (lc + seq_len) // KEY_CHUNK
    ctx_chunks = lc // KEY_CHUNK
    half = seq_len // 2
    tiles = qt_ref.shape[3] // tq
    n_items = tiles * MLA_HEADS

    def query_of(item):
        tile, hd = divmod(item, MLA_HEADS)
        if tile < tiles:
            return qt_ref[0, hd, :, tile * tq:(tile + 1) * tq]
        return qtn_ref[0, hd]

    def chunk_of(ctx_ref, lat_ref, hd, c, lanes):
        ref, c = (ctx_ref, c) if c < ctx_chunks else (lat_ref, c - ctx_chunks)
        span = slice(c * KEY_CHUNK, (c + 1) * KEY_CHUNK)
        return ref[0, hd, :, span] if lanes else ref[0, hd, span, :]

    def score_chunk(s_ref, item, c, m8):
        hd = item % MLA_HEADS
        s = _dot(chunk_of(kc_ref, kl_ref, hd, c, False), query_of(item))
        s_ref[c * KEY_CHUNK:(c + 1) * KEY_CHUNK, :] = s
        part = _fold_rows(s, jnp.maximum)
        return part if m8 is None else jnp.maximum(m8, part)

    ones_rows = jnp.ones((BF16_SUBLANES, KEY_CHUNK), BF16)

    def prob_chunk(s, hd, c, m, acc):
        p = jnp.exp2(s - m)
        vt_ext = jnp.concatenate([chunk_of(vtc_ref, vtl_ref, hd, c, True), ones_rows], axis=0)
        o_t = _dot(vt_ext, p.astype(BF16))
        return o_t if acc is None else acc + o_t

    def item_step(item, m8):
        s_ref, hd = s_refs[item % AHEAD], item % MLA_HEADS
        m = jnp.max(m8, axis=0, keepdims=True)
        m8, acc = None, None
        for c in range(n_chunks):
            s_old = s_ref[c * KEY_CHUNK:(c + 1) * KEY_CHUNK, :]
            m8 = score_chunk(s_ref, item + AHEAD, c, m8)
            acc = prob_chunk(s_old, hd, c, m, acc)
        attnt_ref[hd] = acc[0:V_DIM] * (1.0 / acc[V_DIM:V_DIM + 1])
        return m8

    @pl.when(pl.program_id(1) == 0)
    def _():
        _fold_sequence(xcs_ref, mirror_ref, fold_ref, seq_len)
        m8s = [None] * AHEAD
        for c in range(n_chunks):
            for item in range(AHEAD):
                m8s[item] = score_chunk(s_refs[item], item, c, m8s[item])
        for item in range(AHEAD):
            m_ref[item] = m8s[item]

    _, _, gate = _split_mod(mod_ref)
    bc, bs = bc_ref[...], bs_ref[...]
    coarse = tq // DFT_FINE
    m8s = [m_ref[slot] for slot in range(AHEAD)]
    for tile in range(tiles):
        span = slice(tile * tq, (tile + 1) * tq)
        for i in range(coarse):
            ac = ac_ref[tile * coarse + i:tile * coarse + i + 1, :]
            asn = as_ref[tile * coarse + i:tile * coarse + i + 1, :]
            rows = slice(i * DFT_FINE, (i + 1) * DFT_FINE)
            cls_ref[tile, rows, 0:half] = (ac * bc - asn * bs).astype(BF16)
            cls_ref[tile, rows, half:2 * half] = (asn * bc + ac * bs).astype(BF16)
        specs, yc = [], None
        for hd in range(MLA_HEADS):
            item = tile * MLA_HEADS + hd
            m8s[item % AHEAD] = item_step(item, m8s[item % AHEAD])
            if hd in DFT_AFTER_HEADS:
                j = DFT_AFTER_HEADS.index(hd)
                cols = slice(j * DFT_BLOCK, (j + 1) * DFT_BLOCK)
                specs.append(_dot(cls_ref[tile], fold_ref[:, cols]).astype(BF16))
            if hd == DFT_AFTER_HEADS[-1] + 1:
                yc = _dot(jnp.concatenate(specs, axis=-1), fw_ref[...]) * gf_ref[0, span, :].astype(F32)
        yd = attnt_ref[...].reshape(MLA_WIDTH, tq).T * gd_ref[0, span, :].astype(F32)
        y = _dot(jnp.concatenate([yc.astype(BF16), yd.astype(BF16)], axis=-1), wout_ref[...])
        o_ref[0, span, :] = _deep_norm_rows(x_ref[0, span, :], y, gate * (1.0 / ALPHA), lng_ref[...], lnb_ref[...])
    assert n_items % AHEAD == 0
    for slot in range(AHEAD):
        m_ref[slot] = m8s[slot]


def _cd_main(qt, kc, kl, vtc, vtl, xcs, mirror, a_cos, a_sin, b_cos, b_sin, gf, gd, x, mod, fnet_w, w_out,
             ln_g, ln_b, *, tq):
    bsz, seq, d = x.shape
    lc = kc.shape[2]
    hp = HEAD_PAD
    nh = MLA_HEADS
    per_b4 = lambda b, t: (b, 0, 0, 0)
    per_b3 = lambda b, t: (b, 0, 0)
    kern = functools.partial(_cd_main_kernel, tq=tq, seq_len=seq)
    tiles = TILES_PER_STEP
    rows = tiles * tq
    coarse = rows // DFT_FINE
    last_tile = seq // tq - 1
    return pl.pallas_call(
        kern,
        grid=(bsz, seq // rows),
        in_specs=[pl.BlockSpec((1, nh, hp, rows), lambda b, t: (b, 0, 0, t)),
                  pl.BlockSpec((1, nh, hp, tq), lambda b, t: (b, 0, 0, jnp.minimum(tiles * (t + 1), last_tile))),
                  pl.BlockSpec((1, nh, lc, hp), per_b4),
                  pl.BlockSpec((1, nh, seq, hp), per_b4),
                  pl.BlockSpec((1, nh, V_DIM, lc), per_b4),
                  pl.BlockSpec((1, nh, V_DIM, seq), per_b4),
                  pl.BlockSpec((1, 2 * seq, FNET_WIDTH), per_b3),
                  _const_spec(mirror),
                  pl.BlockSpec((coarse, seq // 2), lambda b, t: (t, 0)),
                  pl.BlockSpec((coarse, seq // 2), lambda b, t: (t, 0)),
                  _const_spec(b_cos), _const_spec(b_sin),
                  pl.BlockSpec((1, rows, FNET_WIDTH), lambda b, t: (b, t, 0)),
                  pl.BlockSpec((1, rows, MLA_WIDTH), lambda b, t: (b, t, 0)),
                  pl.BlockSpec((1, rows, d), lambda b, t: (b, t, 0)),
                  pl.BlockSpec((1, 1, 3 * d), lambda b, t: (b, 0, 0)),
                  _const_spec(fnet_w), _const_spec(w_out), _const_spec(ln_g), _const_spec(ln_b)],
        out_specs=pl.BlockSpec((1, rows, d), lambda b, t: (b, t, 0)),
        out_shape=jax.ShapeDtypeStruct(x.shape, F32),
        scratch_shapes=[pltpu.VMEM((tiles, tq, seq), BF16),
                        pltpu.VMEM((seq, FNET_WIDTH), BF16),
                        pltpu.VMEM((nh, V_DIM, tq), F32),
                        pltpu.VMEM((AHEAD, SUBLANES, tq), F32)]
                       + [pltpu.VMEM((lc + seq, tq), F32)] * AHEAD,
        compiler_params=pltpu.CompilerParams(dimension_semantics=("parallel", "arbitrary"),
                                             vmem_limit_bytes=VMEM_LIMIT),
        name="cd_main",
    )(qt, qt, kc, kl, vtc, vtl, xcs, mirror, a_cos, a_sin, b_cos, b_sin, gf, gd, x, mod, fnet_w, w_out, ln_g, ln_b)


def _rope_tables(seq):
    pos = np.arange(seq)
    row = (pos // GRID_W).astype(np.float64)
    col = (pos % GRID_W).astype(np.float64)
    half = ROPE_AXIS // 2
    inv = (ROPE_BASE ** (-np.arange(0, ROPE_AXIS, 2, dtype=np.float32) / ROPE_AXIS)).astype(np.float32)
    cos = np.ones((seq, LANES), np.float32)
    sin = np.zeros((seq, LANES), np.float32)
    for j in range(QK_ROPE):
        coord = row if j < ROPE_AXIS else col
        ang = (coord.astype(np.float32) * inv[j % half]).astype(np.float32)
        cos[:, QK_NOPE + j] = np.cos(ang)
        sign = -1.0 if (j % ROPE_AXIS) < half else 1.0
        sin[:, QK_NOPE + j] = sign * np.sin(ang)
    return cos, sin


def _rope_partner(j):
    half = ROPE_AXIS // 2
    return j + half if (j % ROPE_AXIS) < half else j - half


def _dft_tables(seq):
    l = np.arange(seq // 2, dtype=np.float64)
    k1 = np.arange(seq // DFT_FINE, dtype=np.float64)[:, None]
    k0 = np.arange(DFT_FINE, dtype=np.float64)[:, None]
    ang_a = 2.0 * np.pi * ((k1 * DFT_FINE * l) % seq) / seq
    ang_b = 2.0 * np.pi * ((k0 * l) % seq) / seq
    norm = 1.0 / np.sqrt(seq * GROUP_DIM)
    b_sin = np.sin(ang_b) * norm
    b_sin[:, 0] = np.where(np.arange(DFT_FINE) % 2 == 0, norm, -norm)
    return (np.cos(ang_a).astype(np.float32), np.sin(ang_a).astype(np.float32),
            (np.cos(ang_b) * norm).astype(np.float32), b_sin.astype(np.float32))


def _mirror_matrix():
    m = np.zeros((FOLD_BLOCK, FOLD_BLOCK), np.float32)
    i = np.arange(1, FOLD_BLOCK)
    m[i, FOLD_BLOCK - i] = 1.0
    return m


def _channel_dft():
    c = np.arange(GROUP_DIM, dtype=np.float64)
    ang = 2.0 * np.pi * ((c[:, None] * c[None, :]) % GROUP_DIM) / GROUP_DIM
    return np.concatenate([np.cos(ang), -np.sin(ang)], axis=1).astype(np.float32)


def kernel(x, c, ctx, c_ctx, ab_w_mod, ab_b_mod, ab_w_in, ab_pool_w, ab_pool_scale, ab_sgu_w, ab_sgu_b,
           ab_w_out, ab_ln_g, ab_ln_b, cd_w_mod, cd_b_mod, cd_w_in, cd_fnet_w, cd_q_norm, cd_kv_norm,
           cd_w_q_up, cd_w_kv_up, cd_w_out, cd_ln_g, cd_ln_b):
    bsz, seq, d = x.shape
    nh, hp = MLA_HEADS, HEAD_PAD
    ctx_len = ctx.shape[1]
    assert d == D_MODEL and seq % ROW_TILE == 0 and seq % (2 * FOLD_BLOCK) == 0 and seq % GRID_W == 0
    assert seq % (TILES_PER_STEP * QUERY_TILE) == 0 and QUERY_TILE % DFT_FINE == 0
    assert FNET_WIDTH == len(DFT_AFTER_HEADS) * DFT_BLOCK and (TILES_PER_STEP * MLA_HEADS) % AHEAD == 0
    assert ctx_len % KEY_CHUNK == 0 and ctx_len % CHUNK == 0 and ROW_TILE % OUT_SLAB == 0 and seq % KEY_CHUNK == 0
    assert max(POOL_WINDOWS) // 2 <= SUBLANES <= HALO and MLA_HEADS % AHEAD == 0

    rows = -(-(bsz + 1) // 8) * 8
    cond = jnp.zeros((rows, d), F32).at[:bsz].set(c).at[bsz].set(c_ctx)
    mod_ab, mod_cd = _modulation_pair(cond, ab_w_mod[0], ab_b_mod[0], cd_w_mod[0], cd_b_mod[0])
    lat_row = lambda b: b
    ctx_row = lambda b: bsz

    ab_args = (ab_w_in[0].astype(BF16), ab_pool_w[0].astype(BF16), ab_pool_scale[0].reshape(1, -1),
               ab_sgu_w[0].astype(BF16),
               jnp.broadcast_to(ab_sgu_b[0][:, :, None], (SGU_HEADS, CHUNK, GROUP_DIM)),
               ab_w_out[0].astype(BF16), ab_ln_g[0].reshape(1, -1), ab_ln_b[0].reshape(1, -1))
    x1 = _ab_layer(x, mod_ab, lat_row, *ab_args, tl=ROW_TILE)

    w = cd_w_in[0]
    o_cq = 2 * FNET_WIDTH
    o_ckv = o_cq + Q_LORA
    o_kr = o_ckv + KV_LORA
    o_dg = o_kr + QK_ROPE
    partner = np.array([_rope_partner(j) for j in range(QK_ROPE)])
    zeros = lambda n: jnp.zeros((d, n), F32)
    kr_cols = w[:, o_kr:o_dg]
    kr_lanes = jnp.concatenate([zeros(QK_NOPE), kr_cols, zeros(LANES - QK_NOPE - QK_ROPE)], axis=1)
    kr_both = jnp.concatenate([kr_cols[:, partner], zeros(QK_NOPE - QK_ROPE), kr_cols,
                               zeros(LANES - QK_NOPE - QK_ROPE)], axis=1)
    w1 = jnp.concatenate([w[:, 0:o_cq], w[:, o_dg:], w[:, o_cq:o_kr], kr_both], axis=1).astype(BF16)
    w_kv_ctx = jnp.concatenate([w[:, o_ckv:o_kr], kr_lanes], axis=1).astype(BF16)

    wq = cd_w_q_up[0].reshape(Q_LORA, nh, QK_NOPE + QK_ROPE)
    wqt = jnp.concatenate([wq.reshape(Q_LORA, -1), wq[:, :, QK_NOPE + partner].reshape(Q_LORA, -1)],
                          axis=1).T.astype(BF16)

    wkv = cd_w_kv_up[0].reshape(KV_LORA, nh, QK_NOPE + V_DIM)
    wk_pad = jnp.concatenate([wkv[:, :, :QK_NOPE], jnp.zeros((KV_LORA, nh, hp - QK_NOPE), F32)], axis=-1)
    lane = np.arange(hp)
    sel = np.diag(((lane >= QK_NOPE) & (lane < QK_NOPE + QK_ROPE)).astype(np.float32))
    wk = jnp.concatenate([wk_pad.reshape(KV_LORA, nh * hp),
                          jnp.asarray(np.tile(sel, (1, nh)))], axis=0).astype(BF16)
    wvt = wkv[:, :, QK_NOPE:].reshape(KV_LORA, nh * V_DIM).T.astype(BF16)

    q_norm = cd_q_norm[0].reshape(1, -1)
    kv_norm = cd_kv_norm[0].reshape(1, -1)
    cos, sin = _rope_tables(seq)
    rope_lanes = slice(QK_NOPE, QK_NOPE + QK_ROPE)
    cosq_t = (cos[:, rope_lanes] * np.float32(Q_SCALE)).T.copy()
    sinq_t = (sin[:, rope_lanes] * np.float32(Q_SCALE)).T.copy()
    cs = jnp.asarray(_channel_dft()).astype(BF16)

    kc, vtc = _ab_ctx_kv(ctx, mod_ab, mod_cd, ctx_row, *ab_args, w_kv_ctx, kv_norm, wk, wvt)
    xcs, gf, gd, qt, kl, vtl = _cd_proj(x1, mod_cd, w1, cs, q_norm, wqt, kv_norm, wk, wvt,
                                        jnp.asarray(cosq_t), jnp.asarray(sinq_t),
                                        jnp.asarray(cos), jnp.asarray(sin), tl=ROW_TILE)
    tq = QUERY_TILE
    a_cos, a_sin, b_cos, b_sin = (jnp.asarray(t) for t in _dft_tables(seq))
    mirror = jnp.asarray(_mirror_matrix()).astype(BF16)
    vtc = vtc.reshape(bsz, nh, V_DIM, -1)
    vtl = vtl.reshape(bsz, nh, V_DIM, -1)
    return _cd_main(qt, kc, kl, vtc, vtl, xcs.reshape(bsz, 2 * seq, FNET_WIDTH), mirror,
                    a_cos, a_sin, b_cos, b_sin,
                    gf, gd, x1, mod_cd, cd_fnet_w[0].astype(BF16), cd_w_out[0].astype(BF16),
                    cd_ln_g[0].reshape(1, -1), cd_ln_b[0].reshape(1, -1), tq=tq)
```
